```python
import jax, jax.numpy as jnp
from jax import lax
import numpy as np

D_MODEL = 1024
BATCH = 2
SEQ = 8192
DEPTH = 1
DEC_BATCH = 8
DEC_SEQ = 4096
PAST_LEN = 128

N_HEADS = 8
QK_NOPE = 64
QK_ROPE = 32
QK_HEAD = QK_NOPE + QK_ROPE
V_HEAD = 64
Q_LORA = 384
KV_LORA = 256
ATTN_W = N_HEADS * V_HEAD
ROPE_THETA = 10000.0
Q_BLOCK = 128
FNET_W = D_MODEL - ATTN_W
FNET_GROUPS = 4
FNET_CH = FNET_W // FNET_GROUPS
IN_W = Q_LORA + KV_LORA + QK_ROPE + FNET_W
PEER_HEADS = 8
PEER_NKEYS = 128
PEER_EXPERTS = PEER_NKEYS * PEER_NKEYS
PEER_DKEY = 256
PEER_HALF = PEER_DKEY // 2
PEER_TOPK = 16
PEER_CHUNK = 128
EPS = 1e-6

kernel_name = 'hybrid_mla_fnet_peer_encoder'


def _rmsnorm(x, g):
    xf = x.astype(jnp.float32)
    y = xf * lax.rsqrt(jnp.mean(xf * xf, axis=-1, keepdims=True) + EPS)
    return (y * g.astype(jnp.float32)).astype(x.dtype)


def _rope(x, pos):
    half = QK_ROPE // 2
    freqs = 1.0 / (ROPE_THETA ** (jnp.arange(half, dtype=jnp.float32) / half))
    ang = pos[:, None] * freqs[None, :]
    cos = jnp.cos(ang)[None, :, None, :]
    sin = jnp.sin(ang)[None, :, None, :]
    xf = x.astype(jnp.float32)
    x1, x2 = xf[..., :half], xf[..., half:]
    return jnp.concatenate([x1 * cos - x2 * sin, x2 * cos + x1 * sin], axis=-1).astype(x.dtype)


def _mla(c_q, c_kv, k_r, q_lat_g, w_uq, kv_lat_g, w_ukv, q_head_g, k_head_g):
    B, S, _ = c_q.shape
    q = (_rmsnorm(c_q, q_lat_g) @ w_uq).reshape(B, S, N_HEADS, QK_HEAD)
    kv = (_rmsnorm(c_kv, kv_lat_g) @ w_ukv).reshape(B, S, N_HEADS, QK_NOPE + V_HEAD)
    k_nope, v = kv[..., :QK_NOPE], kv[..., QK_NOPE:]
    k = jnp.concatenate([k_nope, jnp.broadcast_to(k_r[:, :, None, :], (B, S, N_HEADS, QK_ROPE))], axis=-1)
    q = _rmsnorm(q, q_head_g)
    k = _rmsnorm(k, k_head_g)
    pos = jnp.arange(S, dtype=jnp.float32)
    q = jnp.concatenate([q[..., :QK_NOPE], _rope(q[..., QK_NOPE:], pos)], axis=-1)
    k = jnp.concatenate([k[..., :QK_NOPE], _rope(k[..., QK_NOPE:], pos)], axis=-1)
    nb = S // Q_BLOCK
    qb = q.reshape(B, nb, Q_BLOCK, N_HEADS, QK_HEAD).transpose(1, 0, 2, 3, 4)
    scale = QK_HEAD ** -0.5

    def block(qi):
        s = jnp.einsum('bqhd,bkhd->bhqk', qi, k).astype(jnp.float32) * scale
        p = jax.nn.softmax(s, axis=-1).astype(v.dtype)
        return jnp.einsum('bhqk,bkhd->bqhd', p, v)

    o = lax.map(block, qb)
    return o.transpose(1, 0, 2, 3, 4).reshape(B, S, ATTN_W)


def _fnet(f):
    B, S, _ = f.shape
    z = f.reshape(B, S, FNET_GROUPS, FNET_CH).astype(jnp.float32)
    y = jnp.fft.fft2(z, axes=(1, 3), norm='ortho').real
    return y.reshape(B, S, FNET_W).astype(f.dtype)


def _peer(h, w_pq, sub_keys, u, v):
    B, S, D = h.shape
    T = B * S
    t = h.reshape(T, D)
    q = (t @ w_pq).reshape(T, PEER_HEADS, 2, PEER_HALF)
    s = jnp.einsum('thcd,hcnd->thcn', q, sub_keys)
    sv, si = lax.top_k(s, PEER_TOPK)
    cand = (sv[:, :, 0, :, None] + sv[:, :, 1, None, :]).reshape(T, PEER_HEADS, PEER_TOPK * PEER_TOPK)
    cs, ci = lax.top_k(cand, PEER_TOPK)
    i1 = jnp.take_along_axis(si[:, :, 0, :], ci // PEER_TOPK, axis=-1)
    i2 = jnp.take_along_axis(si[:, :, 1, :], ci % PEER_TOPK, axis=-1)
    idx = (i1 * PEER_NKEYS + i2).reshape(T, PEER_HEADS * PEER_TOPK)
    g = jax.nn.softmax(cs.astype(jnp.float32), axis=-1).astype(h.dtype).reshape(T, PEER_HEADS * PEER_TOPK)
    nc = T // PEER_CHUNK
    tc = t.reshape(nc, PEER_CHUNK, D)
    ic = idx.reshape(nc, PEER_CHUNK, PEER_HEADS * PEER_TOPK)
    gc = g.reshape(nc, PEER_CHUNK, PEER_HEADS * PEER_TOPK)

    def retrieve(args):
        xt, it, gt = args
        a = jnp.einsum('td,tkd->tk', xt, u[it])
        a = jax.nn.gelu(a) * gt
        return jnp.einsum('tk,tkd->td', a, v[it])

    out = lax.map(retrieve, (tc, ic, gc))
    return out.reshape(B, S, D)


def _layer(x, attn_norm_g, w_in, q_lat_g, w_uq, kv_lat_g, w_ukv, q_head_g, k_head_g,
           attn_out_g, fnet_out_g, w_out, ffn_norm_g, peer_w_q, peer_sub_keys, peer_u, peer_v):
    h = _rmsnorm(x, attn_norm_g)
    p = h @ w_in
    o0 = Q_LORA
    o1 = o0 + KV_LORA
    o2 = o1 + QK_ROPE
    c_q, c_kv, k_r, f = p[..., :o0], p[..., o0:o1], p[..., o1:o2], p[..., o2:]
    a = _mla(c_q, c_kv, k_r, q_lat_g, w_uq, kv_lat_g, w_ukv, q_head_g, k_head_g)
    fm = _fnet(f)
    mixed = jnp.concatenate([_rmsnorm(a, attn_out_g), _rmsnorm(fm, fnet_out_g)], axis=-1)
    x = x + mixed @ w_out
    x = x + _peer(_rmsnorm(x, ffn_norm_g), peer_w_q, peer_sub_keys, peer_u, peer_v)
    return x


def _trunk(x, attn_norm_g, w_in, q_lat_g, w_uq, kv_lat_g, w_ukv, q_head_g, k_head_g,
           attn_out_g, fnet_out_g, w_out, ffn_norm_g, peer_w_q, peer_sub_keys, peer_u, peer_v):
    for l in range(DEPTH):
        x = _layer(x, attn_norm_g[l], w_in[l], q_lat_g[l], w_uq[l], kv_lat_g[l], w_ukv[l],
                   q_head_g[l], k_head_g[l], attn_out_g[l], fnet_out_g[l], w_out[l],
                   ffn_norm_g[l], peer_w_q[l], peer_sub_keys[l], peer_u[l], peer_v[l])
    return x


def setup_inputs(seed: int = 0) -> dict:
    key = jax.random.key(seed)
    ks = jax.random.split(key, 20)
    f32 = jnp.float32

    def nrm(k, shape, std):
        return jax.random.normal(k, shape, f32) * std

    def gain(k, n):
        return 1.0 + 0.02 * jax.random.normal(k, (DEPTH, n), f32)

    return {
        'x_prompt': jax.random.normal(ks[0], (BATCH, SEQ, D_MODEL), f32),
        'x_sample': jax.random.normal(ks[1], (DEC_BATCH, DEC_SEQ, D_MODEL), f32),
        'attn_norm_g': gain(ks[2], D_MODEL),
        'w_in': nrm(ks[3], (DEPTH, D_MODEL, IN_W), D_MODEL ** -0.5),
        'q_lat_g': gain(ks[4], Q_LORA),
        'w_uq': nrm(ks[5], (DEPTH, Q_LORA, N_HEADS * QK_HEAD), Q_LORA ** -0.5),
        'kv_lat_g': gain(ks[6], KV_LORA),
        'w_ukv': nrm(ks[7], (DEPTH, KV_LORA, N_HEADS * (QK_NOPE + V_HEAD)), KV_LORA ** -0.5),
        'q_head_g': gain(ks[8], QK_HEAD),
        'k_head_g': gain(ks[9], QK_HEAD),
        'attn_out_g': gain(ks[10], ATTN_W),
        'fnet_out_g': gain(ks[11], FNET_W),
        'w_out': nrm(ks[12], (DEPTH, D_MODEL, D_MODEL), D_MODEL ** -0.5),
        'ffn_norm_g': gain(ks[13], D_MODEL),
        'peer_w_q': nrm(ks[14], (DEPTH, D_MODEL, PEER_HEADS * PEER_DKEY), D_MODEL ** -0.5),
        'peer_sub_keys': nrm(ks[15], (DEPTH, PEER_HEADS, 2, PEER_NKEYS, PEER_HALF), PEER_HALF ** -0.5),
        'peer_u': nrm(ks[16], (DEPTH, PEER_EXPERTS, D_MODEL), D_MODEL ** -0.5),
        'peer_v': nrm(ks[17], (DEPTH, PEER_EXPERTS, D_MODEL), (PEER_HEADS * PEER_TOPK) ** -0.5),
    }


def reference(x_prompt, x_sample, attn_norm_g, w_in, q_lat_g, w_uq, kv_lat_g, w_ukv, q_head_g,
              k_head_g, attn_out_g, fnet_out_g, w_out, ffn_norm_g, peer_w_q, peer_sub_keys,
              peer_u, peer_v):
    y_prompt = _trunk(x_prompt, attn_norm_g, w_in, q_lat_g, w_uq, kv_lat_g, w_ukv, q_head_g,
                      k_head_g, attn_out_g, fnet_out_g, w_out, ffn_norm_g, peer_w_q,
                      peer_sub_keys, peer_u, peer_v)
    y_sample = _trunk(x_sample, attn_norm_g, w_in, q_lat_g, w_uq, kv_lat_g, w_ukv, q_head_g,
                      k_head_g, attn_out_g, fnet_out_g, w_out, ffn_norm_g, peer_w_q,
                      peer_sub_keys, peer_u, peer_v)
    return (y_prompt, y_sample)
```

```python
import functools

import jax
import jax.numpy as jnp
import numpy as np
from jax import lax
from jax.experimental import pallas as pl
from jax.experimental.pallas import tpu as pltpu

D_MODEL = 1024
N_HEADS = 8
QK_NOPE = 64
QK_ROPE = 32
QK_HEAD = QK_NOPE + QK_ROPE
V_HEAD = 64
Q_LORA = 384
KV_LORA = 256
ATTN_W = N_HEADS * V_HEAD
ROPE_THETA = 10000.0
FNET_W = D_MODEL - ATTN_W
FNET_GROUPS = 4
FNET_CH = FNET_W // FNET_GROUPS
PEER_HEADS = 8
PEER_NKEYS = 128
PEER_EXPERTS = PEER_NKEYS * PEER_NKEYS
PEER_HALF = 128
PEER_TOPK = 16
EPS = 1e-6

LANES = 128
VMEM_LIMIT = 48 * 1024 * 1024

F32 = jnp.float32
BF16 = jnp.bfloat16


def _rmsnorm(x, g):
    xf = x.astype(F32)
    y = xf * lax.rsqrt(jnp.mean(xf * xf, axis=-1, keepdims=True) + EPS)
    return y * g.astype(F32)


def _rope(x, pos):
    half = QK_ROPE // 2
    freqs = 1.0 / (ROPE_THETA ** (jnp.arange(half, dtype=F32) / half))
    ang = pos[:, None] * freqs[None, :]
    cos = jnp.cos(ang)[None, :, None, :]
    sin = jnp.sin(ang)[None, :, None, :]
    x1, x2 = x[..., :half], x[..., half:]
    return jnp.concatenate([x1 * cos - x2 * sin, x2 * cos + x1 * sin], axis=-1)


def _flash_kernel(q_ref, k_ref, v_ref, o_ref, *, tk):
    seq = k_ref.shape[1]
    tq = q_ref.shape[1]
    nkv = seq // tk
    outs = []
    for hh in range(2):
        q = q_ref[0, :, hh * LANES:(hh + 1) * LANES]

        def body(j, carry, hh=hh, q=q):
            m, l, acc = carry
            start = pl.multiple_of(j * tk, tk)
            k = k_ref[0, pl.ds(start, tk), hh * LANES:(hh + 1) * LANES]
            v = v_ref[0, pl.ds(start, tk), :]
            s = lax.dot_general(q, k, (((1,), (1,)), ((), ())), preferred_element_type=F32)
            m_new = jnp.maximum(m, jnp.max(s, axis=1, keepdims=True))
            alpha = jnp.exp(m - m_new)
            p = jnp.exp(s - m_new)
            l_new = alpha * l + jnp.sum(p, axis=1, keepdims=True)
            acc_new = alpha * acc + jnp.dot(p.astype(BF16), v, preferred_element_type=F32)
            return m_new, l_new, acc_new

        init = (jnp.full((tq, 1), -jnp.inf, F32), jnp.zeros((tq, 1), F32),
                jnp.zeros((tq, LANES), F32))
        m, l, acc = lax.fori_loop(0, nkv, body, init)
        outs.append(acc / l)
    lane = lax.broadcasted_iota(jnp.int32, (tq, LANES), 1)
    o_ref[0] = jnp.where(lane < V_HEAD, outs[0], outs[1])


def _flash_attention(q, k, v, *, tq=256, tk=512):
    b, s, _ = q.shape
    grid = (b, N_HEADS // 2, s // tq)
    return pl.pallas_call(
        functools.partial(_flash_kernel, tk=tk),
        grid=grid,
        in_specs=[
            pl.BlockSpec((1, tq, 2 * LANES), lambda bi, hp, qi: (bi, qi, hp)),
            pl.BlockSpec((1, s, 2 * LANES), lambda bi, hp, qi: (bi, 0, hp)),
            pl.BlockSpec((1, s, LANES), lambda bi, hp, qi: (bi, 0, hp)),
        ],
        out_specs=pl.BlockSpec((1, tq, LANES), lambda bi, hp, qi: (bi, qi, hp)),
        out_shape=jax.ShapeDtypeStruct((b, s, ATTN_W), F32),
        compiler_params=pltpu.CompilerParams(
            dimension_semantics=("arbitrary", "arbitrary", "arbitrary"),
            vmem_limit_bytes=VMEM_LIMIT),
        name="flash_attention",
    )(q, k, v)


def _gelu_tanh(x):
    c = np.float32(np.sqrt(2.0 / np.pi))
    return 0.5 * x * (1.0 + jnp.tanh(c * (x + np.float32(0.044715) * (x * x * x))))


def _peer_kernel(hn_ref, u_ref, vt_ref, s1_ref, e1_ref, s2_ref, e2_ref, thr_ref, x_ref,
                 y_ref, acc_ref, p_ref, *, n_i1):
    eb = pl.program_id(1)
    tm = hn_ref.shape[0]

    @pl.when(eb == 0)
    def _():
        acc_ref[...] = jnp.zeros_like(acc_ref)

    a_t = lax.dot_general(u_ref[...], hn_ref[...], (((1,), (1,)), ((), ())),
                          preferred_element_type=F32)
    for j in range(n_i1):
        for c in range(tm // LANES):
            cs = slice(c * LANES, (c + 1) * LANES)
            g = jnp.zeros((PEER_NKEYS, LANES), F32)
            for h in range(PEER_HEADS):
                s1row = s1_ref[h, 0, j:j + 1, cs]
                e1row = e1_ref[h, 0, j:j + 1, cs]
                mask = (s1row + s2_ref[h, :, cs]) >= thr_ref[h, :, cs]
                g = g + jnp.where(mask, e2_ref[h, :, cs], 0.0) * e1row
            a = a_t[j * PEER_NKEYS:(j + 1) * PEER_NKEYS, cs]
            p_ref[j * PEER_NKEYS:(j + 1) * PEER_NKEYS, cs] = (_gelu_tanh(a) * g).astype(BF16)
    acc_ref[...] += jnp.dot(vt_ref[...], p_ref[...], preferred_element_type=F32)

    @pl.when(eb == pl.num_programs(1) - 1)
    def _():
        y_ref[...] = x_ref[...] + acc_ref[...].T


def _peer_dense(hn, u, vt, s1t, e1t, s2t, e2t, thr, x1, *, tm=512, n_i1=4):
    t = hn.shape[0]
    te = n_i1 * PEER_NKEYS
    nblk = PEER_NKEYS // n_i1
    s1r = s1t.reshape(PEER_HEADS, nblk, n_i1, t)
    e1r = e1t.reshape(PEER_HEADS, nblk, n_i1, t)
    grid = (t // tm, nblk)
    return pl.pallas_call(
        functools.partial(_peer_kernel, n_i1=n_i1),
        grid=grid,
        in_specs=[
            pl.BlockSpec((tm, D_MODEL), lambda ti, eb: (ti, 0)),
            pl.BlockSpec((te, D_MODEL), lambda ti, eb: (eb, 0)),
            pl.BlockSpec((D_MODEL, te), lambda ti, eb: (0, eb)),
            pl.BlockSpec((PEER_HEADS, 1, n_i1, tm), lambda ti, eb: (0, eb, 0, ti)),
            pl.BlockSpec((PEER_HEADS, 1, n_i1, tm), lambda ti, eb: (0, eb, 0, ti)),
            pl.BlockSpec((PEER_HEADS, PEER_NKEYS, tm), lambda ti, eb: (0, 0, ti)),
            pl.BlockSpec((PEER_HEADS, PEER_NKEYS, tm), lambda ti, eb: (0, 0, ti)),
            pl.BlockSpec((PEER_HEADS, 1, tm), lambda ti, eb: (0, 0, ti)),
            pl.BlockSpec((tm, D_MODEL), lambda ti, eb: (ti, 0)),
        ],
        out_specs=pl.BlockSpec((tm, D_MODEL), lambda ti, eb: (ti, 0)),
        out_shape=jax.ShapeDtypeStruct((t, D_MODEL), F32),
        scratch_shapes=[pltpu.VMEM((D_MODEL, tm), F32), pltpu.VMEM((te, tm), BF16)],
        compiler_params=pltpu.CompilerParams(
            dimension_semantics=("arbitrary", "arbitrary"),
            vmem_limit_bytes=VMEM_LIMIT),
        name="peer_dense",
    )(hn, u, vt, s1r, e1r, s2t, e2t, thr, x1)


def _qkv(x, attn_norm_g, w_in, q_lat_g, w_uq, kv_lat_g, w_ukv, q_head_g, k_head_g):
    b, s, _ = x.shape
    h = _rmsnorm(x, attn_norm_g)
    p = h @ w_in
    o0, o1, o2 = Q_LORA, Q_LORA + KV_LORA, Q_LORA + KV_LORA + QK_ROPE
    c_q, c_kv, k_r, f = p[..., :o0], p[..., o0:o1], p[..., o1:o2], p[..., o2:]
    q = (_rmsnorm(c_q, q_lat_g) @ w_uq).reshape(b, s, N_HEADS, QK_HEAD)
    kv = (_rmsnorm(c_kv, kv_lat_g) @ w_ukv).reshape(b, s, N_HEADS, QK_NOPE + V_HEAD)
    k_nope, v = kv[..., :QK_NOPE], kv[..., QK_NOPE:]
    k = jnp.concatenate([k_nope, jnp.broadcast_to(k_r[:, :, None, :], (b, s, N_HEADS, QK_ROPE))], axis=-1)
    q = _rmsnorm(q, q_head_g) * (QK_HEAD ** -0.5)
    k = _rmsnorm(k, k_head_g)
    pos = jnp.arange(s, dtype=F32)
    q = jnp.concatenate([q[..., :QK_NOPE], _rope(q[..., QK_NOPE:], pos)], axis=-1)
    k = jnp.concatenate([k[..., :QK_NOPE], _rope(k[..., QK_NOPE:], pos)], axis=-1)
    pad = ((0, 0), (0, 0), (0, 0), (0, LANES - QK_HEAD))
    q = jnp.pad(q, pad).reshape(b, s, N_HEADS * LANES).astype(BF16)
    k = jnp.pad(k, pad).reshape(b, s, N_HEADS * LANES).astype(BF16)
    v = v.reshape(b, s, ATTN_W).astype(BF16)
    return q, k, v, f


def _fnet(f):
    b, s, _ = f.shape
    z = f.reshape(b, s, FNET_GROUPS, FNET_CH)
    y = jnp.fft.fft2(z, axes=(1, 3), norm='ortho').real
    return y.reshape(b, s, FNET_W)


def _mixer(x, attn_norm_g, w_in, q_lat_g, w_uq, kv_lat_g, w_ukv, q_head_g, k_head_g,
           attn_out_g, fnet_out_g, w_out):
    q, k, v, f = _qkv(x, attn_norm_g, w_in, q_lat_g, w_uq, kv_lat_g, w_ukv, q_head_g, k_head_g)
    a = _flash_attention(q, k, v)
    fm = _fnet(f)
    mixed = jnp.concatenate([_rmsnorm(a, attn_out_g), _rmsnorm(fm, fnet_out_g)], axis=-1)
    return x + mixed @ w_out


def _peer_gates(hn, w_pq, sub_keys):
    t = hn.shape[0]
    q = (hn @ w_pq).reshape(t, PEER_HEADS, 2, PEER_HALF)
    s = jnp.einsum('thcd,hcnd->thcn', q, sub_keys)
    sv, _ = lax.top_k(s, PEER_TOPK)
    cand = (sv[:, :, 0, :, None] + sv[:, :, 1, None, :]).reshape(t, PEER_HEADS, PEER_TOPK * PEER_TOPK)
    cs, _ = lax.top_k(cand, PEER_TOPK)
    thr = cs[..., PEER_TOPK - 1]
    top = cs[..., 0]
    z = jnp.sum(jnp.exp(cs - top[..., None]), axis=-1)
    s1, s2 = s[:, :, 0, :], s[:, :, 1, :]
    e1 = jnp.exp(s1 - jnp.max(s1, axis=-1, keepdims=True))
    e2 = jnp.exp(s2 - jnp.max(s2, axis=-1, keepdims=True)) / z[..., None]
    tr = lambda a: jnp.transpose(a, (1, 2, 0))
    return tr(s1), tr(e1), tr(s2), tr(e2), jnp.transpose(thr, (1, 0))[:, None, :]


def kernel(x_prompt, x_sample, attn_norm_g, w_in, q_lat_g, w_uq, kv_lat_g, w_ukv, q_head_g,
           k_head_g, attn_out_g, fnet_out_g, w_out, ffn_norm_g, peer_w_q, peer_sub_keys,
           peer_u, peer_v):
    l = 0
    mix = functools.partial(
        _mixer, attn_norm_g=attn_norm_g[l], w_in=w_in[l], q_lat_g=q_lat_g[l], w_uq=w_uq[l],
        kv_lat_g=kv_lat_g[l], w_ukv=w_ukv[l], q_head_g=q_head_g[l], k_head_g=k_head_g[l],
        attn_out_g=attn_out_g[l], fnet_out_g=fnet_out_g[l], w_out=w_out[l])
    x1p = mix(x_prompt)
    x1s = mix(x_sample)
    x1 = jnp.concatenate([x1p.reshape(-1, D_MODEL), x1s.reshape(-1, D_MODEL)], axis=0)
    hn = _rmsnorm(x1, ffn_norm_g[l])
    s1t, e1t, s2t, e2t, thr = _peer_gates(hn, peer_w_q[l], peer_sub_keys[l])
    u = peer_u[l].astype(BF16)
    vt = peer_v[l].astype(BF16).T
    y = _peer_dense(hn.astype(BF16), u, vt, s1t, e1t, s2t, e2t, thr, x1)
    np_ = x_prompt.shape[0] * x_prompt.shape[1]
    return (y[:np_].reshape(x_prompt.shape), y[np_:].reshape(x_sample.shape))
```

```python
import functools

import jax
import jax.numpy as jnp
import numpy as np
from jax import lax
from jax.experimental import pallas as pl
from jax.experimental.pallas import tpu as pltpu

D_MODEL = 1024
N_HEADS = 8
QK_NOPE = 64
QK_ROPE = 32
QK_HEAD = QK_NOPE + QK_ROPE
V_HEAD = 64
Q_LORA = 384
KV_LORA = 256
ATTN_W = N_HEADS * V_HEAD
ROPE_THETA = 10000.0
FNET_W = D_MODEL - ATTN_W
FNET_GROUPS = 4
FNET_CH = FNET_W // FNET_GROUPS
PEER_HEADS = 8
PEER_NKEYS = 128
PEER_EXPERTS = PEER_NKEYS * PEER_NKEYS
PEER_HALF = 128
PEER_TOPK = 16
EPS = 1e-6

LANES = 128
VMEM_LIMIT = 48 * 1024 * 1024

F32 = jnp.float32
BF16 = jnp.bfloat16


def _rmsnorm(x, g):
    xf = x.astype(F32)
    y = xf * lax.rsqrt(jnp.mean(xf * xf, axis=-1, keepdims=True) + EPS)
    return y * g.astype(F32)


def _rope(x, pos):
    half = QK_ROPE // 2
    freqs = 1.0 / (ROPE_THETA ** (jnp.arange(half, dtype=F32) / half))
    ang = pos[:, None] * freqs[None, :]
    cos = jnp.cos(ang)[None, :, None, :]
    sin = jnp.sin(ang)[None, :, None, :]
    x1, x2 = x[..., :half], x[..., half:]
    return jnp.concatenate([x1 * cos - x2 * sin, x2 * cos + x1 * sin], axis=-1)


def _flash_kernel(q_ref, k_ref, v_ref, o_ref, *, tk):
    seq = k_ref.shape[1]
    tq = q_ref.shape[1]
    nkv = seq // tk
    outs = []
    for hh in range(2):
        q = q_ref[0, :, hh * LANES:(hh + 1) * LANES]

        def body(j, carry, hh=hh, q=q):
            m, l, acc = carry
            start = pl.multiple_of(j * tk, tk)
            k = k_ref[0, pl.ds(start, tk), hh * LANES:(hh + 1) * LANES]
            v = v_ref[0, pl.ds(start, tk), :]
            s = lax.dot_general(q, k, (((1,), (1,)), ((), ())), preferred_element_type=F32)
            m_new = jnp.maximum(m, jnp.max(s, axis=1, keepdims=True))
            alpha = jnp.exp(m - m_new)
            p = jnp.exp(s - m_new)
            l_new = alpha * l + jnp.sum(p, axis=1, keepdims=True)
            acc_new = alpha * acc + jnp.dot(p.astype(BF16), v, preferred_element_type=F32)
            return m_new, l_new, acc_new

        init = (jnp.full((tq, 1), -jnp.inf, F32), jnp.zeros((tq, 1), F32),
                jnp.zeros((tq, LANES), F32))
        m, l, acc = lax.fori_loop(0, nkv, body, init)
        outs.append(acc / l)
    lane = lax.broadcasted_iota(jnp.int32, (tq, LANES), 1)
    o_ref[0] = jnp.where(lane < V_HEAD, outs[0], outs[1])


def _flash_attention(q, k, v, *, tq=256, tk=512):
    b, s, _ = q.shape
    grid = (b, N_HEADS // 2, s // tq)
    return pl.pallas_call(
        functools.partial(_flash_kernel, tk=tk),
        grid=grid,
        in_specs=[
            pl.BlockSpec((1, tq, 2 * LANES), lambda bi, hp, qi: (bi, qi, hp)),
            pl.BlockSpec((1, s, 2 * LANES), lambda bi, hp, qi: (bi, 0, hp)),
            pl.BlockSpec((1, s, LANES), lambda bi, hp, qi: (bi, 0, hp)),
        ],
        out_specs=pl.BlockSpec((1, tq, LANES), lambda bi, hp, qi: (bi, qi, hp)),
        out_shape=jax.ShapeDtypeStruct((b, s, ATTN_W), F32),
        compiler_params=pltpu.CompilerParams(
            dimension_semantics=("arbitrary", "arbitrary", "arbitrary"),
            vmem_limit_bytes=VMEM_LIMIT),
        name="flash_attention",
    )(q, k, v)


_CAND_PAIRS = tuple((a, b) for a in range(PEER_TOPK) for b in range(PEER_TOPK)
                    if (a + 1) * (b + 1) <= PEER_TOPK)
_NEG_INF = float("-inf")


def _tree_max(xs):
    xs = list(xs)
    while len(xs) > 1:
        nxt = [jnp.maximum(xs[i], xs[i + 1]) for i in range(0, len(xs) - 1, 2)]
        if len(xs) % 2:
            nxt.append(xs[-1])
        xs = nxt
    return xs[0]


def _gate_kernel(hn_ref, wq_ref, keys_ref, rank2_ref, e2_ref, n1_ref, e1_ref, s_scr, vals_scr):
    tm = hn_ref.shape[0]
    ncol = tm // LANES
    q = jnp.dot(hn_ref[...], wq_ref[...], preferred_element_type=F32).astype(BF16)
    for hc in range(2 * PEER_HEADS):
        s_scr[hc] = lax.dot_general(keys_ref[hc], q[:, hc * PEER_HALF:(hc + 1) * PEER_HALF],
                                    (((1,), (1,)), ((), ())), preferred_element_type=F32)

    for hc in range(2 * PEER_HEADS):
        h, c = divmod(hc, 2)

        def col_body(col, carry, hc=hc, h=h, c=c):
            cs = pl.ds(pl.multiple_of(col * LANES, LANES), LANES)
            cur = s_scr[hc, :, cs]
            rank = jnp.full((PEER_NKEYS, LANES), float(PEER_TOPK), F32)
            for r in range(PEER_TOPK):
                m = jnp.max(cur, axis=0, keepdims=True)
                hit = cur == m
                if c == 1:
                    rank = jnp.where(hit, float(r), rank)
                cur = jnp.where(hit, _NEG_INF, cur)
                vals_scr[c, r, h:h + 1, cs] = m
            if c == 1:
                rank2_ref[h, :, cs] = rank
            return carry

        lax.fori_loop(0, ncol, col_body, 0)

    def fin_body(col, carry):
        cs = pl.ds(pl.multiple_of(col * LANES, LANES), LANES)
        v1 = [vals_scr[0, a, :, cs] for a in range(PEER_TOPK)]
        v2 = [vals_scr[1, b, :, cs] for b in range(PEER_TOPK)]
        cands = [v1[a] + v2[b] for a, b in _CAND_PAIRS]
        top = cands[0]
        z = jnp.zeros_like(top)
        m = top
        for r in range(PEER_TOPK):
            m = _tree_max(cands)
            z = z + jnp.exp(m - top)
            if r + 1 < PEER_TOPK:
                cands = [jnp.where(cd == m, _NEG_INF, cd) for cd in cands]
        thr = m
        rz = 1.0 / z
        for h in range(PEER_HEADS):
            thr_h = thr[h:h + 1, :]
            s1 = s_scr[2 * h, :, cs]
            n1 = jnp.zeros((PEER_NKEYS, LANES), F32)
            for b in range(PEER_TOPK):
                n1 = n1 + jnp.where((s1 + v2[b][h:h + 1, :]) >= thr_h, 1.0, 0.0)
            n1_ref[h, :, cs] = n1
            e1_ref[h, :, cs] = jnp.exp(s1 - v1[0][h:h + 1, :])
            s2 = s_scr[2 * h + 1, :, cs]
            e2_ref[h, :, cs] = jnp.exp(s2 - v2[0][h:h + 1, :]) * rz[h:h + 1, :]
        return carry

    lax.fori_loop(0, ncol, fin_body, 0)


def _peer_gate(hn, wq, keys, *, tm=256):
    t = hn.shape[0]
    out = jax.ShapeDtypeStruct((PEER_HEADS, PEER_NKEYS, t), F32)
    ospec = pl.BlockSpec((PEER_HEADS, PEER_NKEYS, tm), lambda ti: (0, 0, ti))
    return pl.pallas_call(
        _gate_kernel,
        grid=(t // tm,),
        in_specs=[
            pl.BlockSpec((tm, D_MODEL), lambda ti: (ti, 0)),
            pl.BlockSpec((D_MODEL, 2 * PEER_HEADS * PEER_HALF), lambda ti: (0, 0)),
            pl.BlockSpec((2 * PEER_HEADS, PEER_NKEYS, PEER_HALF), lambda ti: (0, 0, 0)),
        ],
        out_specs=[ospec, ospec, ospec, ospec],
        out_shape=[out, out, out, out],
        scratch_shapes=[pltpu.VMEM((2 * PEER_HEADS, PEER_NKEYS, tm), F32),
                        pltpu.VMEM((2, PEER_TOPK, PEER_HEADS, tm), F32)],
        compiler_params=pltpu.CompilerParams(
            dimension_semantics=("arbitrary",), vmem_limit_bytes=VMEM_LIMIT),
        name="peer_gate",
    )(hn, wq, keys)


def _gelu_tanh(x):
    c = np.float32(np.sqrt(2.0 / np.pi))
    return 0.5 * x * (1.0 + jnp.tanh(c * (x + np.float32(0.044715) * (x * x * x))))


def _peer_kernel(hn_ref, u_ref, vt_ref, n1_ref, e1_ref, rank2_ref, e2_ref, x_ref,
                 y_ref, acc_ref, p_ref, *, n_i1):
    eb = pl.program_id(1)
    tm = hn_ref.shape[0]

    @pl.when(eb == 0)
    def _():
        acc_ref[...] = jnp.zeros_like(acc_ref)

    a_t = lax.dot_general(u_ref[...], hn_ref[...], (((1,), (1,)), ((), ())),
                          preferred_element_type=F32)
    for j in range(n_i1):
        for c in range(tm // LANES):
            cs = slice(c * LANES, (c + 1) * LANES)
            g = jnp.zeros((PEER_NKEYS, LANES), F32)
            for h in range(PEER_HEADS):
                n1row = n1_ref[h, 0, j:j + 1, cs]
                e1row = e1_ref[h, 0, j:j + 1, cs]
                mask = rank2_ref[h, :, cs] < n1row
                g = g + jnp.where(mask, e2_ref[h, :, cs], 0.0) * e1row
            a = a_t[j * PEER_NKEYS:(j + 1) * PEER_NKEYS, cs]
            p_ref[j * PEER_NKEYS:(j + 1) * PEER_NKEYS, cs] = (_gelu_tanh(a) * g).astype(BF16)
    acc_ref[...] += jnp.dot(vt_ref[...], p_ref[...], preferred_element_type=F32)

    @pl.when(eb == pl.num_programs(1) - 1)
    def _():
        y_ref[...] = x_ref[...] + acc_ref[...].T


def _peer_dense(hn, u, vt, n1t, e1t, rank2t, e2t, x1, *, tm=512, n_i1=4):
    t = hn.shape[0]
    te = n_i1 * PEER_NKEYS
    nblk = PEER_NKEYS // n_i1
    n1r = n1t.reshape(PEER_HEADS, nblk, n_i1, t)
    e1r = e1t.reshape(PEER_HEADS, nblk, n_i1, t)
    grid = (t // tm, nblk)
    return pl.pallas_call(
        functools.partial(_peer_kernel, n_i1=n_i1),
        grid=grid,
        in_specs=[
            pl.BlockSpec((tm, D_MODEL), lambda ti, eb: (ti, 0)),
            pl.BlockSpec((te, D_MODEL), lambda ti, eb: (eb, 0)),
            pl.BlockSpec((D_MODEL, te), lambda ti, eb: (0, eb)),
            pl.BlockSpec((PEER_HEADS, 1, n_i1, tm), lambda ti, eb: (0, eb, 0, ti)),
            pl.BlockSpec((PEER_HEADS, 1, n_i1, tm), lambda ti, eb: (0, eb, 0, ti)),
            pl.BlockSpec((PEER_HEADS, PEER_NKEYS, tm), lambda ti, eb: (0, 0, ti)),
            pl.BlockSpec((PEER_HEADS, PEER_NKEYS, tm), lambda ti, eb: (0, 0, ti)),
            pl.BlockSpec((tm, D_MODEL), lambda ti, eb: (ti, 0)),
        ],
        out_specs=pl.BlockSpec((tm, D_MODEL), lambda ti, eb: (ti, 0)),
        out_shape=jax.ShapeDtypeStruct((t, D_MODEL), F32),
        scratch_shapes=[pltpu.VMEM((D_MODEL, tm), F32), pltpu.VMEM((te, tm), BF16)],
        compiler_params=pltpu.CompilerParams(
            dimension_semantics=("arbitrary", "arbitrary"),
            vmem_limit_bytes=VMEM_LIMIT),
        name="peer_dense",
    )(hn, u, vt, n1r, e1r, rank2t, e2t, x1)


def _qkv(x, attn_norm_g, w_in, q_lat_g, w_uq, kv_lat_g, w_ukv, q_head_g, k_head_g):
    b, s, _ = x.shape
    h = _rmsnorm(x, attn_norm_g)
    p = h @ w_in
    o0, o1, o2 = Q_LORA, Q_LORA + KV_LORA, Q_LORA + KV_LORA + QK_ROPE
    c_q, c_kv, k_r, f = p[..., :o0], p[..., o0:o1], p[..., o1:o2], p[..., o2:]
    q = (_rmsnorm(c_q, q_lat_g) @ w_uq).reshape(b, s, N_HEADS, QK_HEAD)
    kv = (_rmsnorm(c_kv, kv_lat_g) @ w_ukv).reshape(b, s, N_HEADS, QK_NOPE + V_HEAD)
    k_nope, v = kv[..., :QK_NOPE], kv[..., QK_NOPE:]
    k = jnp.concatenate([k_nope, jnp.broadcast_to(k_r[:, :, None, :], (b, s, N_HEADS, QK_ROPE))], axis=-1)
    q = _rmsnorm(q, q_head_g) * (QK_HEAD ** -0.5)
    k = _rmsnorm(k, k_head_g)
    pos = jnp.arange(s, dtype=F32)
    q = jnp.concatenate([q[..., :QK_NOPE], _rope(q[..., QK_NOPE:], pos)], axis=-1)
    k = jnp.concatenate([k[..., :QK_NOPE], _rope(k[..., QK_NOPE:], pos)], axis=-1)
    pad = ((0, 0), (0, 0), (0, 0), (0, LANES - QK_HEAD))
    q = jnp.pad(q, pad).reshape(b, s, N_HEADS * LANES).astype(BF16)
    k = jnp.pad(k, pad).reshape(b, s, N_HEADS * LANES).astype(BF16)
    v = v.reshape(b, s, ATTN_W).astype(BF16)
    return q, k, v, f


def _fnet(f):
    b, s, _ = f.shape
    z = f.reshape(b, s, FNET_GROUPS, FNET_CH)
    y = jnp.fft.fft2(z, axes=(1, 3), norm='ortho').real
    return y.reshape(b, s, FNET_W)


def _mixer(x, attn_norm_g, w_in, q_lat_g, w_uq, kv_lat_g, w_ukv, q_head_g, k_head_g,
           attn_out_g, fnet_out_g, w_out):
    q, k, v, f = _qkv(x, attn_norm_g, w_in, q_lat_g, w_uq, kv_lat_g, w_ukv, q_head_g, k_head_g)
    a = _flash_attention(q, k, v)
    fm = _fnet(f)
    mixed = jnp.concatenate([_rmsnorm(a, attn_out_g), _rmsnorm(fm, fnet_out_g)], axis=-1)
    return x + mixed @ w_out


def kernel(x_prompt, x_sample, attn_norm_g, w_in, q_lat_g, w_uq, kv_lat_g, w_ukv, q_head_g,
           k_head_g, attn_out_g, fnet_out_g, w_out, ffn_norm_g, peer_w_q, peer_sub_keys,
           peer_u, peer_v):
    l = 0
    mix = functools.partial(
        _mixer, attn_norm_g=attn_norm_g[l], w_in=w_in[l], q_lat_g=q_lat_g[l], w_uq=w_uq[l],
        kv_lat_g=kv_lat_g[l], w_ukv=w_ukv[l], q_head_g=q_head_g[l], k_head_g=k_head_g[l],
        attn_out_g=attn_out_g[l], fnet_out_g=fnet_out_g[l], w_out=w_out[l])
    x1p = mix(x_prompt)
    x1s = mix(x_sample)
    x1 = jnp.concatenate([x1p.reshape(-1, D_MODEL), x1s.reshape(-1, D_MODEL)], axis=0)
    hn = _rmsnorm(x1, ffn_norm_g[l]).astype(BF16)
    keys = peer_sub_keys[l].reshape(2 * PEER_HEADS, PEER_NKEYS, PEER_HALF).astype(BF16)
    rank2t, e2t, n1t, e1t = _peer_gate(hn, peer_w_q[l].astype(BF16), keys)
    u = peer_u[l].astype(BF16)
    vt = peer_v[l].astype(BF16).T
    y = _peer_dense(hn, u, vt, n1t, e1t, rank2t, e2t, x1)
    np_ = x_prompt.shape[0] * x_prompt.shape[1]
    return (y[:np_].reshape(x_prompt.shape), y[np_:].reshape(x_sample.shape))
```

```python
import functools

import jax
import jax.numpy as jnp
import numpy as np
from jax import lax
from jax.experimental import pallas as pl
from jax.experimental.pallas import tpu as pltpu

D_MODEL = 1024
N_HEADS = 8
QK_NOPE = 64
QK_ROPE = 32
QK_HEAD = QK_NOPE + QK_ROPE
V_HEAD = 64
Q_LORA = 384
KV_LORA = 256
ATTN_W = N_HEADS * V_HEAD
ROPE_THETA = 10000.0
FNET_W = D_MODEL - ATTN_W
FNET_GROUPS = 4
FNET_CH = FNET_W // FNET_GROUPS
PEER_HEADS = 8
PEER_NKEYS = 128
PEER_EXPERTS = PEER_NKEYS * PEER_NKEYS
PEER_HALF = 128
PEER_TOPK = 16
EPS = 1e-6

LANES = 128
BF16_ROWS = 16
VMEM_LIMIT = 48 * 1024 * 1024

F32 = jnp.float32
BF16 = jnp.bfloat16


def _rmsnorm(x, g):
    xf = x.astype(F32)
    y = xf * lax.rsqrt(jnp.mean(xf * xf, axis=-1, keepdims=True) + EPS)
    return y * g.astype(F32)


def _rope(x, pos):
    half = QK_ROPE // 2
    freqs = 1.0 / (ROPE_THETA ** (jnp.arange(half, dtype=F32) / half))
    ang = pos[:, None] * freqs[None, :]
    cos = jnp.cos(ang)[None, :, None, :]
    sin = jnp.sin(ang)[None, :, None, :]
    x1, x2 = x[..., :half], x[..., half:]
    return jnp.concatenate([x1 * cos - x2 * sin, x2 * cos + x1 * sin], axis=-1)


def _flash_kernel(q_ref, k_ref, v_ref, o_ref, *, tk):
    seq = k_ref.shape[1]
    tq = q_ref.shape[1]
    nkv = seq // tk
    outs = []
    for hh in range(2):
        q = q_ref[0, :, hh * LANES:(hh + 1) * LANES]

        def body(j, carry, hh=hh, q=q):
            m, l, acc = carry
            start = pl.multiple_of(j * tk, tk)
            k = k_ref[0, pl.ds(start, tk), hh * LANES:(hh + 1) * LANES]
            v = v_ref[0, pl.ds(start, tk), :]
            s = lax.dot_general(q, k, (((1,), (1,)), ((), ())), preferred_element_type=F32)
            m_new = jnp.maximum(m, jnp.max(s, axis=1, keepdims=True))
            alpha = jnp.exp(m - m_new)
            p = jnp.exp(s - m_new)
            l_new = alpha * l + jnp.sum(p, axis=1, keepdims=True)
            acc_new = alpha * acc + jnp.dot(p.astype(BF16), v, preferred_element_type=F32)
            return m_new, l_new, acc_new

        init = (jnp.full((tq, 1), -jnp.inf, F32), jnp.zeros((tq, 1), F32),
                jnp.zeros((tq, LANES), F32))
        m, l, acc = lax.fori_loop(0, nkv, body, init)
        outs.append(acc / l)
    lane = lax.broadcasted_iota(jnp.int32, (tq, LANES), 1)
    o_ref[0] = jnp.where(lane < V_HEAD, outs[0], outs[1])


def _flash_attention(q, k, v, *, tq=256, tk=512):
    b, s, _ = q.shape
    grid = (b, N_HEADS // 2, s // tq)
    return pl.pallas_call(
        functools.partial(_flash_kernel, tk=tk),
        grid=grid,
        in_specs=[
            pl.BlockSpec((1, tq, 2 * LANES), lambda bi, hp, qi: (bi, qi, hp)),
            pl.BlockSpec((1, s, 2 * LANES), lambda bi, hp, qi: (bi, 0, hp)),
            pl.BlockSpec((1, s, LANES), lambda bi, hp, qi: (bi, 0, hp)),
        ],
        out_specs=pl.BlockSpec((1, tq, LANES), lambda bi, hp, qi: (bi, qi, hp)),
        out_shape=jax.ShapeDtypeStruct((b, s, ATTN_W), F32),
        compiler_params=pltpu.CompilerParams(
            dimension_semantics=("arbitrary", "arbitrary", "arbitrary"),
            vmem_limit_bytes=VMEM_LIMIT),
        name="flash_attention",
    )(q, k, v)


_CAND_PAIRS = tuple((a, b) for a in range(PEER_TOPK) for b in range(PEER_TOPK)
                    if (a + 1) * (b + 1) <= PEER_TOPK)
_NEG_INF = float("-inf")


def _tree_max(xs):
    xs = list(xs)
    while len(xs) > 1:
        nxt = [jnp.maximum(xs[i], xs[i + 1]) for i in range(0, len(xs) - 1, 2)]
        if len(xs) % 2:
            nxt.append(xs[-1])
        xs = nxt
    return xs[0]


def _gate_kernel(hn_ref, wq_ref, keys_ref, rank2_ref, e2_ref, n1_ref, e1_ref, s_scr, vals_scr):
    tm = hn_ref.shape[0]
    ncol = tm // LANES
    q = jnp.dot(hn_ref[...], wq_ref[...], preferred_element_type=F32).astype(BF16)
    for hc in range(2 * PEER_HEADS):
        s_scr[hc] = lax.dot_general(keys_ref[hc], q[:, hc * PEER_HALF:(hc + 1) * PEER_HALF],
                                    (((1,), (1,)), ((), ())), preferred_element_type=F32)

    for hc in range(2 * PEER_HEADS):
        h, c = divmod(hc, 2)

        def col_body(col, carry, hc=hc, h=h, c=c):
            cs = pl.ds(pl.multiple_of(col * LANES, LANES), LANES)
            cur = s_scr[hc, :, cs]
            rank = jnp.full((PEER_NKEYS, LANES), float(PEER_TOPK), F32)
            for r in range(PEER_TOPK):
                m = jnp.max(cur, axis=0, keepdims=True)
                hit = cur == m
                if c == 1:
                    rank = jnp.where(hit, float(r), rank)
                cur = jnp.where(hit, _NEG_INF, cur)
                vals_scr[c, r, h:h + 1, cs] = m
            if c == 1:
                rank2_ref[h, :, cs] = rank
            return carry

        lax.fori_loop(0, ncol, col_body, 0)

    def fin_body(col, carry):
        cs = pl.ds(pl.multiple_of(col * LANES, LANES), LANES)
        v1 = [vals_scr[0, a, :, cs] for a in range(PEER_TOPK)]
        v2 = [vals_scr[1, b, :, cs] for b in range(PEER_TOPK)]
        cands = [v1[a] + v2[b] for a, b in _CAND_PAIRS]
        top = cands[0]
        z = jnp.zeros_like(top)
        m = top
        for r in range(PEER_TOPK):
            m = _tree_max(cands)
            z = z + jnp.exp(m - top)
            if r + 1 < PEER_TOPK:
                cands = [jnp.where(cd == m, _NEG_INF, cd) for cd in cands]
        thr = m
        rz = 1.0 / z
        for h in range(PEER_HEADS):
            thr_h = thr[h:h + 1, :]
            s1 = s_scr[2 * h, :, cs]
            n1 = jnp.zeros((PEER_NKEYS, LANES), F32)
            for b in range(PEER_TOPK):
                n1 = n1 + jnp.where((s1 + v2[b][h:h + 1, :]) >= thr_h, 1.0, 0.0)
            n1_ref[h, :, cs] = n1
            e1_ref[h, :, cs] = jnp.exp(s1 - v1[0][h:h + 1, :])
            s2 = s_scr[2 * h + 1, :, cs]
            e2_ref[h, :, cs] = jnp.exp(s2 - v2[0][h:h + 1, :]) * rz[h:h + 1, :]
        return carry

    lax.fori_loop(0, ncol, fin_body, 0)


def _peer_gate(hn, wq, keys, *, tm=256):
    t = hn.shape[0]
    out = jax.ShapeDtypeStruct((PEER_HEADS, PEER_NKEYS, t), F32)
    ospec = pl.BlockSpec((PEER_HEADS, PEER_NKEYS, tm), lambda ti: (0, 0, ti))
    return pl.pallas_call(
        _gate_kernel,
        grid=(t // tm,),
        in_specs=[
            pl.BlockSpec((tm, D_MODEL), lambda ti: (ti, 0)),
            pl.BlockSpec((D_MODEL, 2 * PEER_HEADS * PEER_HALF), lambda ti: (0, 0)),
            pl.BlockSpec((2 * PEER_HEADS, PEER_NKEYS, PEER_HALF), lambda ti: (0, 0, 0)),
        ],
        out_specs=[ospec, ospec, ospec, ospec],
        out_shape=[out, out, out, out],
        scratch_shapes=[pltpu.VMEM((2 * PEER_HEADS, PEER_NKEYS, tm), F32),
                        pltpu.VMEM((2, PEER_TOPK, PEER_HEADS, tm), F32)],
        compiler_params=pltpu.CompilerParams(
            dimension_semantics=("arbitrary",), vmem_limit_bytes=VMEM_LIMIT),
        name="peer_gate",
    )(hn, wq, keys)


def _gelu_tanh(x):
    c = np.float32(np.sqrt(2.0 / np.pi))
    return 0.5 * x * (1.0 + jnp.tanh(c * (x + np.float32(0.044715) * (x * x * x))))


def _peer_kernel(hn_ref, u_ref, vt_ref, n1_ref, e1_ref, rank2_ref, e2_ref, x_ref,
                 y_ref, acc_ref, a0_scr, a1_scr, g0_scr, g1_scr, r2_scr, e2_scr, *, n_i1):
    eb = pl.program_id(1)
    tm = hn_ref.shape[0]
    rows = PEER_NKEYS // BF16_ROWS

    @pl.when(eb == 0)
    def _():
        acc_ref[...] = jnp.zeros_like(acc_ref)
        a1_scr[...] = jnp.zeros_like(a1_scr)
        g1_scr[...] = jnp.zeros_like(g1_scr)
        r2_scr[...] = rank2_ref[...].astype(BF16)
        e2_scr[...] = e2_ref[...].astype(BF16)

    def step(a_prev, g_prev, a_cur, g_cur):
        half = tm // 2

        def scores(hf):
            return lax.dot_general(u_ref[...], hn_ref[hf * half:(hf + 1) * half, :],
                                   (((1,), (1,)), ((), ())), preferred_element_type=F32)

        def activate(c):
            cs = slice(c * LANES, (c + 1) * LANES)
            return (_gelu_tanh(a_prev[:, cs]) * g_prev[:, cs].astype(F32)).astype(BF16)

        def retrieve(p):
            return jnp.dot(vt_ref[...], p, preferred_element_type=F32)

        def build_gate(c):
            cs = slice(c * LANES, (c + 1) * LANES)
            gs = [jnp.zeros((rows, BF16_ROWS, LANES), BF16) for _ in range(n_i1)]
            for h in range(PEER_HEADS):
                r2 = r2_scr[h, :, cs].reshape(rows, BF16_ROWS, LANES)
                e2 = e2_scr[h, :, cs].reshape(rows, BF16_ROWS, LANES)
                for j in range(n_i1):
                    n1row = jnp.broadcast_to(n1_ref[h, 0, j:j + 1, cs], (BF16_ROWS, LANES)).astype(BF16)
                    e1row = jnp.broadcast_to(e1_ref[h, 0, j:j + 1, cs], (BF16_ROWS, LANES)).astype(BF16)
                    gs[j] = gs[j] + jnp.where(r2 < n1row[None], e2, jnp.zeros_like(e2)) * e1row[None]
            for j in range(n_i1):
                g_cur[j * PEER_NKEYS:(j + 1) * PEER_NKEYS, cs] = gs[j].reshape(PEER_NKEYS, LANES)

        ncol = half // LANES
        a_new0 = scores(0)
        p0 = jnp.concatenate([activate(c) for c in range(ncol)], axis=1)
        out0 = retrieve(p0)
        p1 = jnp.concatenate([activate(c) for c in range(ncol, 2 * ncol)], axis=1)
        build_gate(0)
        out1 = retrieve(p1)
        build_gate(1)
        build_gate(2)
        a_new1 = scores(1)
        build_gate(3)
        acc_ref[:, :half] += out0
        acc_ref[:, half:] += out1
        a_cur[:, :half] = a_new0
        a_cur[:, half:] = a_new1

    @pl.when(eb % 2 == 0)
    def _():
        step(a1_scr, g1_scr, a0_scr, g0_scr)

    @pl.when(eb % 2 == 1)
    def _():
        step(a0_scr, g0_scr, a1_scr, g1_scr)

    @pl.when(eb == pl.num_programs(1) - 1)
    def _():
        y_ref[...] = x_ref[...] + acc_ref[...].T


def _peer_dense(hn, u, vt, n1t, e1t, rank2t, e2t, x1, *, tm=512, n_i1=4):
    t = hn.shape[0]
    te = n_i1 * PEER_NKEYS
    nblk = PEER_NKEYS // n_i1
    n1r = n1t.reshape(PEER_HEADS, nblk, n_i1, t)
    e1r = e1t.reshape(PEER_HEADS, nblk, n_i1, t)
    grid = (t // tm, nblk + 1)
    this_blk = lambda eb: jnp.minimum(eb, nblk - 1)
    prev_blk = lambda eb: jnp.maximum(eb - 1, 0)
    return pl.pallas_call(
        functools.partial(_peer_kernel, n_i1=n_i1),
        grid=grid,
        in_specs=[
            pl.BlockSpec((tm, D_MODEL), lambda ti, eb: (ti, 0)),
            pl.BlockSpec((te, D_MODEL), lambda ti, eb: (this_blk(eb), 0)),
            pl.BlockSpec((D_MODEL, te), lambda ti, eb: (0, prev_blk(eb))),
            pl.BlockSpec((PEER_HEADS, 1, n_i1, tm), lambda ti, eb: (0, this_blk(eb), 0, ti)),
            pl.BlockSpec((PEER_HEADS, 1, n_i1, tm), lambda ti, eb: (0, this_blk(eb), 0, ti)),
            pl.BlockSpec((PEER_HEADS, PEER_NKEYS, tm), lambda ti, eb: (0, 0, ti)),
            pl.BlockSpec((PEER_HEADS, PEER_NKEYS, tm), lambda ti, eb: (0, 0, ti)),
            pl.BlockSpec((tm, D_MODEL), lambda ti, eb: (ti, 0)),
        ],
        out_specs=pl.BlockSpec((tm, D_MODEL), lambda ti, eb: (ti, 0)),
        out_shape=jax.ShapeDtypeStruct((t, D_MODEL), F32),
        scratch_shapes=[pltpu.VMEM((D_MODEL, tm), F32),
                        pltpu.VMEM((te, tm), F32), pltpu.VMEM((te, tm), F32),
                        pltpu.VMEM((te, tm), BF16), pltpu.VMEM((te, tm), BF16),
                        pltpu.VMEM((PEER_HEADS, PEER_NKEYS, tm), BF16),
                        pltpu.VMEM((PEER_HEADS, PEER_NKEYS, tm), BF16)],
        compiler_params=pltpu.CompilerParams(
            dimension_semantics=("arbitrary", "arbitrary"),
            vmem_limit_bytes=VMEM_LIMIT),
        name="peer_dense",
    )(hn, u, vt, n1r, e1r, rank2t, e2t, x1)


def _qkv(x, attn_norm_g, w_in, q_lat_g, w_uq, kv_lat_g, w_ukv, q_head_g, k_head_g):
    b, s, _ = x.shape
    h = _rmsnorm(x, attn_norm_g)
    p = h @ w_in
    o0, o1, o2 = Q_LORA, Q_LORA + KV_LORA, Q_LORA + KV_LORA + QK_ROPE
    c_q, c_kv, k_r, f = p[..., :o0], p[..., o0:o1], p[..., o1:o2], p[..., o2:]
    q = (_rmsnorm(c_q, q_lat_g) @ w_uq).reshape(b, s, N_HEADS, QK_HEAD)
    kv = (_rmsnorm(c_kv, kv_lat_g) @ w_ukv).reshape(b, s, N_HEADS, QK_NOPE + V_HEAD)
    k_nope, v = kv[..., :QK_NOPE], kv[..., QK_NOPE:]
    k = jnp.concatenate([k_nope, jnp.broadcast_to(k_r[:, :, None, :], (b, s, N_HEADS, QK_ROPE))], axis=-1)
    q = _rmsnorm(q, q_head_g) * (QK_HEAD ** -0.5)
    k = _rmsnorm(k, k_head_g)
    pos = jnp.arange(s, dtype=F32)
    q = jnp.concatenate([q[..., :QK_NOPE], _rope(q[..., QK_NOPE:], pos)], axis=-1)
    k = jnp.concatenate([k[..., :QK_NOPE], _rope(k[..., QK_NOPE:], pos)], axis=-1)
    pad = ((0, 0), (0, 0), (0, 0), (0, LANES - QK_HEAD))
    q = jnp.pad(q, pad).reshape(b, s, N_HEADS * LANES).astype(BF16)
    k = jnp.pad(k, pad).reshape(b, s, N_HEADS * LANES).astype(BF16)
    v = v.reshape(b, s, ATTN_W).astype(BF16)
    return q, k, v, f


def _fnet(f):
    b, s, _ = f.shape
    z = f.reshape(b, s, FNET_GROUPS, FNET_CH)
    y = jnp.fft.fft2(z, axes=(1, 3), norm='ortho').real
    return y.reshape(b, s, FNET_W)


def _mixer(x, attn_norm_g, w_in, q_lat_g, w_uq, kv_lat_g, w_ukv, q_head_g, k_head_g,
           attn_out_g, fnet_out_g, w_out):
    q, k, v, f = _qkv(x, attn_norm_g, w_in, q_lat_g, w_uq, kv_lat_g, w_ukv, q_head_g, k_head_g)
    a = _flash_attention(q, k, v)
    fm = _fnet(f)
    mixed = jnp.concatenate([_rmsnorm(a, attn_out_g), _rmsnorm(fm, fnet_out_g)], axis=-1)
    return x + mixed @ w_out


def kernel(x_prompt, x_sample, attn_norm_g, w_in, q_lat_g, w_uq, kv_lat_g, w_ukv, q_head_g,
           k_head_g, attn_out_g, fnet_out_g, w_out, ffn_norm_g, peer_w_q, peer_sub_keys,
           peer_u, peer_v):
    l = 0
    mix = functools.partial(
        _mixer, attn_norm_g=attn_norm_g[l], w_in=w_in[l], q_lat_g=q_lat_g[l], w_uq=w_uq[l],
        kv_lat_g=kv_lat_g[l], w_ukv=w_ukv[l], q_head_g=q_head_g[l], k_head_g=k_head_g[l],
        attn_out_g=attn_out_g[l], fnet_out_g=fnet_out_g[l], w_out=w_out[l])
    x1p = mix(x_prompt)
    x1s = mix(x_sample)
    x1 = jnp.concatenate([x1p.reshape(-1, D_MODEL), x1s.reshape(-1, D_MODEL)], axis=0)
    hn = _rmsnorm(x1, ffn_norm_g[l]).astype(BF16)
    keys = peer_sub_keys[l].reshape(2 * PEER_HEADS, PEER_NKEYS, PEER_HALF).astype(BF16)
    rank2t, e2t, n1t, e1t = _peer_gate(hn, peer_w_q[l].astype(BF16), keys)
    u = peer_u[l].astype(BF16)
    vt = peer_v[l].astype(BF16).T
    y = _peer_dense(hn, u, vt, n1t, e1t, rank2t, e2t, x1)
    np_ = x_prompt.shape[0] * x_prompt.shape[1]
    return (y[:np_].reshape(x_prompt.shape), y[np_:].reshape(x_sample.shape))
```

```python
import functools

import jax
import jax.numpy as jnp
import numpy as np
from jax import lax
from jax.experimental import pallas as pl
from jax.experimental.pallas import tpu as pltpu

D_MODEL = 1024
N_HEADS = 8
QK_NOPE = 64
QK_ROPE = 32
QK_HEAD = QK_NOPE + QK_ROPE
V_HEAD = 64
Q_LORA = 384
KV_LORA = 256
ATTN_W = N_HEADS * V_HEAD
ROPE_THETA = 10000.0
FNET_W = D_MODEL - ATTN_W
FNET_GROUPS = 4
FNET_CH = FNET_W // FNET_GROUPS
PEER_HEADS = 8
PEER_NKEYS = 128
PEER_EXPERTS = PEER_NKEYS * PEER_NKEYS
PEER_HALF = 128
PEER_TOPK = 16
EPS = 1e-6
LOG2_E = 1.4426950408889634

LANES = 128
BF16_ROWS = 16
VMEM_LIMIT = 48 * 1024 * 1024

F32 = jnp.float32
BF16 = jnp.bfloat16


def _rmsnorm(x, g):
    xf = x.astype(F32)
    y = xf * lax.rsqrt(jnp.mean(xf * xf, axis=-1, keepdims=True) + EPS)
    return y * g.astype(F32)


def _rope(x, pos):
    half = QK_ROPE // 2
    freqs = 1.0 / (ROPE_THETA ** (jnp.arange(half, dtype=F32) / half))
    ang = pos[:, None] * freqs[None, :]
    cos = jnp.cos(ang)[None, :, None, :]
    sin = jnp.sin(ang)[None, :, None, :]
    x1, x2 = x[..., :half], x[..., half:]
    return jnp.concatenate([x1 * cos - x2 * sin, x2 * cos + x1 * sin], axis=-1)


def _flash_kernel(q_ref, kt_ref, v_ref, o_ref, m_scr, acc_scr, *, tk):
    seq = v_ref.shape[1]
    tq = q_ref.shape[1]
    nkv = seq // tk
    m_scr[...] = jnp.full(m_scr.shape, -jnp.inf, F32)
    acc_scr[...] = jnp.zeros(acc_scr.shape, F32)

    def body(j, carry):
        start = pl.multiple_of(j * tk, tk)
        heads = [slice(hh * LANES, (hh + 1) * LANES) for hh in range(2)]
        scores = [jnp.dot(q_ref[0, :, hs], kt_ref[0, hs, pl.ds(start, tk)],
                          preferred_element_type=F32) for hs in heads]
        for hh, hs in enumerate(heads):
            cols = [scores[hh][:, c * LANES:(c + 1) * LANES] for c in range(tk // LANES)]
            m_old = m_scr[hh]
            m_new = jnp.maximum(m_old, jnp.max(functools.reduce(jnp.maximum, cols),
                                               axis=1, keepdims=True))
            p = jnp.concatenate([jnp.exp2(c - m_new) for c in cols], axis=1).astype(BF16)
            pv = jnp.dot(p, v_ref[0, pl.ds(start, tk), hs], preferred_element_type=F32)
            acc_scr[hh] = jnp.exp2(m_old - m_new) * acc_scr[hh] + pv
            m_scr[hh] = m_new
        return carry

    lax.fori_loop(0, nkv, body, 0)
    outs = []
    for hh in range(2):
        acc = acc_scr[hh]
        outs.append(acc[:, :V_HEAD] / acc[:, V_HEAD:V_HEAD + 1])
    o_ref[0] = jnp.concatenate(outs, axis=1)


def _flash_attention(q, kt, v, *, tq=512, tk=512):
    b, s, _ = q.shape
    grid = (b, N_HEADS // 2, s // tq)
    return pl.pallas_call(
        functools.partial(_flash_kernel, tk=tk),
        grid=grid,
        in_specs=[
            pl.BlockSpec((1, tq, 2 * LANES), lambda bi, hp, qi: (bi, qi, hp)),
            pl.BlockSpec((1, 2 * LANES, s), lambda bi, hp, qi: (bi, hp, 0)),
            pl.BlockSpec((1, s, 2 * LANES), lambda bi, hp, qi: (bi, 0, hp)),
        ],
        out_specs=pl.BlockSpec((1, tq, LANES), lambda bi, hp, qi: (bi, qi, hp)),
        out_shape=jax.ShapeDtypeStruct((b, s, ATTN_W), F32),
        scratch_shapes=[pltpu.VMEM((2, tq, LANES), F32), pltpu.VMEM((2, tq, LANES), F32)],
        compiler_params=pltpu.CompilerParams(
            dimension_semantics=("arbitrary", "arbitrary", "arbitrary"),
            vmem_limit_bytes=VMEM_LIMIT),
        name="flash_attention",
    )(q, kt, v)


_CAND_PAIRS = tuple((a, b) for a in range(PEER_TOPK) for b in range(PEER_TOPK)
                    if (a + 1) * (b + 1) <= PEER_TOPK)
_NEG_INF = float("-inf")


def _tree_max(xs):
    xs = list(xs)
    while len(xs) > 1:
        nxt = [jnp.maximum(xs[i], xs[i + 1]) for i in range(0, len(xs) - 1, 2)]
        if len(xs) % 2:
            nxt.append(xs[-1])
        xs = nxt
    return xs[0]


def _gate_kernel(hn_ref, wq_ref, keys_ref, rank2_ref, e2_ref, n1_ref, e1_ref, s_scr, vals_scr):
    tm = hn_ref.shape[0]
    ncol = tm // LANES
    q = jnp.dot(hn_ref[...], wq_ref[...], preferred_element_type=F32).astype(BF16)
    for hc in range(2 * PEER_HEADS):
        s_scr[hc] = lax.dot_general(keys_ref[hc], q[:, hc * PEER_HALF:(hc + 1) * PEER_HALF],
                                    (((1,), (1,)), ((), ())), preferred_element_type=F32)

    for hc in range(2 * PEER_HEADS):
        h, c = divmod(hc, 2)

        def col_body(col, carry, hc=hc, h=h, c=c):
            cs = pl.ds(pl.multiple_of(col * LANES, LANES), LANES)
            cur = s_scr[hc, :, cs]
            rank = jnp.full((PEER_NKEYS, LANES), float(PEER_TOPK), F32)
            for r in range(PEER_TOPK):
                m = jnp.max(cur, axis=0, keepdims=True)
                hit = cur == m
                if c == 1:
                    rank = jnp.where(hit, float(r), rank)
                cur = jnp.where(hit, _NEG_INF, cur)
                vals_scr[c, r, h:h + 1, cs] = m
            if c == 1:
                rank2_ref[h, :, cs] = rank
            return carry

        lax.fori_loop(0, ncol, col_body, 0)

    def fin_body(col, carry):
        cs = pl.ds(pl.multiple_of(col * LANES, LANES), LANES)
        v1 = [vals_scr[0, a, :, cs] for a in range(PEER_TOPK)]
        v2 = [vals_scr[1, b, :, cs] for b in range(PEER_TOPK)]
        cands = [v1[a] + v2[b] for a, b in _CAND_PAIRS]
        top = cands[0]
        z = jnp.zeros_like(top)
        m = top
        for r in range(PEER_TOPK):
            m = _tree_max(cands)
            z = z + jnp.exp(m - top)
            if r + 1 < PEER_TOPK:
                cands = [jnp.where(cd == m, _NEG_INF, cd) for cd in cands]
        thr = m
        rz = 1.0 / z
        for h in range(PEER_HEADS):
            thr_h = thr[h:h + 1, :]
            s1 = s_scr[2 * h, :, cs]
            n1 = jnp.zeros((PEER_NKEYS, LANES), F32)
            for b in range(PEER_TOPK):
                n1 = n1 + jnp.where((s1 + v2[b][h:h + 1, :]) >= thr_h, 1.0, 0.0)
            n1_ref[h, :, cs] = n1
            e1_ref[h, :, cs] = jnp.exp(s1 - v1[0][h:h + 1, :])
            s2 = s_scr[2 * h + 1, :, cs]
            e2_ref[h, :, cs] = jnp.exp(s2 - v2[0][h:h + 1, :]) * rz[h:h + 1, :]
        return carry

    lax.fori_loop(0, ncol, fin_body, 0)


def _peer_gate(hn, wq, keys, *, tm=256):
    t = hn.shape[0]
    out = jax.ShapeDtypeStruct((PEER_HEADS, PEER_NKEYS, t), F32)
    ospec = pl.BlockSpec((PEER_HEADS, PEER_NKEYS, tm), lambda ti: (0, 0, ti))
    return pl.pallas_call(
        _gate_kernel,
        grid=(t // tm,),
        in_specs=[
            pl.BlockSpec((tm, D_MODEL), lambda ti: (ti, 0)),
            pl.BlockSpec((D_MODEL, 2 * PEER_HEADS * PEER_HALF), lambda ti: (0, 0)),
            pl.BlockSpec((2 * PEER_HEADS, PEER_NKEYS, PEER_HALF), lambda ti: (0, 0, 0)),
        ],
        out_specs=[ospec, ospec, ospec, ospec],
        out_shape=[out, out, out, out],
        scratch_shapes=[pltpu.VMEM((2 * PEER_HEADS, PEER_NKEYS, tm), F32),
                        pltpu.VMEM((2, PEER_TOPK, PEER_HEADS, tm), F32)],
        compiler_params=pltpu.CompilerParams(
            dimension_semantics=("arbitrary",), vmem_limit_bytes=VMEM_LIMIT),
        name="peer_gate",
    )(hn, wq, keys)


def _gelu_tanh(x):
    c = np.float32(np.sqrt(2.0 / np.pi))
    return 0.5 * x * (1.0 + jnp.tanh(c * (x + np.float32(0.044715) * (x * x * x))))


def _peer_kernel(hn_ref, u_ref, vt_ref, n1_ref, e1_ref, rank2_ref, e2_ref, x_ref,
                 y_ref, acc_ref, a0_scr, a1_scr, g0_scr, g1_scr, r2_scr, e2_scr, *, n_i1):
    eb = pl.program_id(1)
    tm = hn_ref.shape[0]
    rows = PEER_NKEYS // BF16_ROWS

    @pl.when(eb == 0)
    def _():
        acc_ref[...] = jnp.zeros_like(acc_ref)
        a1_scr[...] = jnp.zeros_like(a1_scr)
        g1_scr[...] = jnp.zeros_like(g1_scr)
        r2_scr[...] = rank2_ref[...].astype(BF16)
        e2_scr[...] = e2_ref[...].astype(BF16)

    def step(a_prev, g_prev, a_cur, g_cur):
        half = tm // 2

        def scores(hf):
            return lax.dot_general(u_ref[...], hn_ref[hf * half:(hf + 1) * half, :],
                                   (((1,), (1,)), ((), ())), preferred_element_type=F32)

        def activate(c):
            cs = slice(c * LANES, (c + 1) * LANES)
            return (_gelu_tanh(a_prev[:, cs]) * g_prev[:, cs].astype(F32)).astype(BF16)

        def retrieve(p):
            return jnp.dot(vt_ref[...], p, preferred_element_type=F32)

        def build_gate(c):
            cs = slice(c * LANES, (c + 1) * LANES)
            gs = [jnp.zeros((rows, BF16_ROWS, LANES), BF16) for _ in range(n_i1)]
            for h in range(PEER_HEADS):
                r2 = r2_scr[h, :, cs].reshape(rows, BF16_ROWS, LANES)
                e2 = e2_scr[h, :, cs].reshape(rows, BF16_ROWS, LANES)
                for j in range(n_i1):
                    n1row = jnp.broadcast_to(n1_ref[h, 0, j:j + 1, cs], (BF16_ROWS, LANES)).astype(BF16)
                    e1row = jnp.broadcast_to(e1_ref[h, 0, j:j + 1, cs], (BF16_ROWS, LANES)).astype(BF16)
                    gs[j] = gs[j] + jnp.where(r2 < n1row[None], e2, jnp.zeros_like(e2)) * e1row[None]
            for j in range(n_i1):
                g_cur[j * PEER_NKEYS:(j + 1) * PEER_NKEYS, cs] = gs[j].reshape(PEER_NKEYS, LANES)

        ncol = half // LANES
        a_new0 = scores(0)
        p0 = jnp.concatenate([activate(c) for c in range(ncol)], axis=1)
        out0 = retrieve(p0)
        p1 = jnp.concatenate([activate(c) for c in range(ncol, 2 * ncol)], axis=1)
        build_gate(0)
        out1 = retrieve(p1)
        build_gate(1)
        build_gate(2)
        a_new1 = scores(1)
        build_gate(3)
        acc_ref[:, :half] += out0
        acc_ref[:, half:] += out1
        a_cur[:, :half] = a_new0
        a_cur[:, half:] = a_new1

    @pl.when(eb % 2 == 0)
    def _():
        step(a1_scr, g1_scr, a0_scr, g0_scr)

    @pl.when(eb % 2 == 1)
    def _():
        step(a0_scr, g0_scr, a1_scr, g1_scr)

    @pl.when(eb == pl.num_programs(1) - 1)
    def _():
        y_ref[...] = x_ref[...] + acc_ref[...].T


def _peer_dense(hn, u, vt, n1t, e1t, rank2t, e2t, x1, *, tm=512, n_i1=4):
    t = hn.shape[0]
    te = n_i1 * PEER_NKEYS
    nblk = PEER_NKEYS // n_i1
    n1r = n1t.reshape(PEER_HEADS, nblk, n_i1, t)
    e1r = e1t.reshape(PEER_HEADS, nblk, n_i1, t)
    grid = (t // tm, nblk + 1)
    this_blk = lambda eb: jnp.minimum(eb, nblk - 1)
    prev_blk = lambda eb: jnp.maximum(eb - 1, 0)
    return pl.pallas_call(
        functools.partial(_peer_kernel, n_i1=n_i1),
        grid=grid,
        in_specs=[
            pl.BlockSpec((tm, D_MODEL), lambda ti, eb: (ti, 0)),
            pl.BlockSpec((te, D_MODEL), lambda ti, eb: (this_blk(eb), 0)),
            pl.BlockSpec((D_MODEL, te), lambda ti, eb: (0, prev_blk(eb))),
            pl.BlockSpec((PEER_HEADS, 1, n_i1, tm), lambda ti, eb: (0, this_blk(eb), 0, ti)),
            pl.BlockSpec((PEER_HEADS, 1, n_i1, tm), lambda ti, eb: (0, this_blk(eb), 0, ti)),
            pl.BlockSpec((PEER_HEADS, PEER_NKEYS, tm), lambda ti, eb: (0, 0, ti)),
            pl.BlockSpec((PEER_HEADS, PEER_NKEYS, tm), lambda ti, eb: (0, 0, ti)),
            pl.BlockSpec((tm, D_MODEL), lambda ti, eb: (ti, 0)),
        ],
        out_specs=pl.BlockSpec((tm, D_MODEL), lambda ti, eb: (ti, 0)),
        out_shape=jax.ShapeDtypeStruct((t, D_MODEL), F32),
        scratch_shapes=[pltpu.VMEM((D_MODEL, tm), F32),
                        pltpu.VMEM((te, tm), F32), pltpu.VMEM((te, tm), F32),
                        pltpu.VMEM((te, tm), BF16), pltpu.VMEM((te, tm), BF16),
                        pltpu.VMEM((PEER_HEADS, PEER_NKEYS, tm), BF16),
                        pltpu.VMEM((PEER_HEADS, PEER_NKEYS, tm), BF16)],
        compiler_params=pltpu.CompilerParams(
            dimension_semantics=("arbitrary", "arbitrary"),
            vmem_limit_bytes=VMEM_LIMIT),
        name="peer_dense",
    )(hn, u, vt, n1r, e1r, rank2t, e2t, x1)


def _qkv(x, attn_norm_g, w_in, q_lat_g, w_uq, kv_lat_g, w_ukv, q_head_g, k_head_g):
    b, s, _ = x.shape
    h = _rmsnorm(x, attn_norm_g)
    p = h @ w_in
    o0, o1, o2 = Q_LORA, Q_LORA + KV_LORA, Q_LORA + KV_LORA + QK_ROPE
    c_q, c_kv, k_r, f = p[..., :o0], p[..., o0:o1], p[..., o1:o2], p[..., o2:]
    q = (_rmsnorm(c_q, q_lat_g) @ w_uq).reshape(b, s, N_HEADS, QK_HEAD)
    kv = (_rmsnorm(c_kv, kv_lat_g) @ w_ukv).reshape(b, s, N_HEADS, QK_NOPE + V_HEAD)
    k_nope, v = kv[..., :QK_NOPE], kv[..., QK_NOPE:]
    k = jnp.concatenate([k_nope, jnp.broadcast_to(k_r[:, :, None, :], (b, s, N_HEADS, QK_ROPE))], axis=-1)
    q = _rmsnorm(q, q_head_g) * (QK_HEAD ** -0.5 * LOG2_E)
    k = _rmsnorm(k, k_head_g)
    pos = jnp.arange(s, dtype=F32)
    q = jnp.concatenate([q[..., :QK_NOPE], _rope(q[..., QK_NOPE:], pos)], axis=-1)
    k = jnp.concatenate([k[..., :QK_NOPE], _rope(k[..., QK_NOPE:], pos)], axis=-1)
    pad = ((0, 0), (0, 0), (0, 0), (0, LANES - QK_HEAD))
    q = jnp.pad(q, pad).reshape(b, s, N_HEADS * LANES).astype(BF16)
    k = jnp.pad(k, pad).reshape(b, s, N_HEADS * LANES).astype(BF16).transpose(0, 2, 1)
    ones = jnp.ones((b, s, N_HEADS, 1), F32)
    v = jnp.concatenate([v, ones], axis=-1)
    v = jnp.pad(v, ((0, 0), (0, 0), (0, 0), (0, LANES - V_HEAD - 1)))
    v = v.reshape(b, s, N_HEADS * LANES).astype(BF16)
    return q, k, v, f


def _fnet(f):
    b, s, _ = f.shape
    z = f.reshape(b, s, FNET_GROUPS, FNET_CH)
    y = jnp.fft.fft2(z, axes=(1, 3), norm='ortho').real
    return y.reshape(b, s, FNET_W)


def _mixer(x, attn_norm_g, w_in, q_lat_g, w_uq, kv_lat_g, w_ukv, q_head_g, k_head_g,
           attn_out_g, fnet_out_g, w_out):
    q, k, v, f = _qkv(x, attn_norm_g, w_in, q_lat_g, w_uq, kv_lat_g, w_ukv, q_head_g, k_head_g)
    a = _flash_attention(q, k, v)
    fm = _fnet(f)
    mixed = jnp.concatenate([_rmsnorm(a, attn_out_g), _rmsnorm(fm, fnet_out_g)], axis=-1)
    return x + mixed @ w_out


def kernel(x_prompt, x_sample, attn_norm_g, w_in, q_lat_g, w_uq, kv_lat_g, w_ukv, q_head_g,
           k_head_g, attn_out_g, fnet_out_g, w_out, ffn_norm_g, peer_w_q, peer_sub_keys,
           peer_u, peer_v):
    l = 0
    mix = functools.partial(
        _mixer, attn_norm_g=attn_norm_g[l], w_in=w_in[l], q_lat_g=q_lat_g[l], w_uq=w_uq[l],
        kv_lat_g=kv_lat_g[l], w_ukv=w_ukv[l], q_head_g=q_head_g[l], k_head_g=k_head_g[l],
        attn_out_g=attn_out_g[l], fnet_out_g=fnet_out_g[l], w_out=w_out[l])
    x1p = mix(x_prompt)
    x1s = mix(x_sample)
    x1 = jnp.concatenate([x1p.reshape(-1, D_MODEL), x1s.reshape(-1, D_MODEL)], axis=0)
    hn = _rmsnorm(x1, ffn_norm_g[l]).astype(BF16)
    keys = peer_sub_keys[l].reshape(2 * PEER_HEADS, PEER_NKEYS, PEER_HALF).astype(BF16)
    rank2t, e2t, n1t, e1t = _peer_gate(hn, peer_w_q[l].astype(BF16), keys)
    u = peer_u[l].astype(BF16)
    vt = peer_v[l].astype(BF16).T
    y = _peer_dense(hn, u, vt, n1t, e1t, rank2t, e2t, x1)
    np_ = x_prompt.shape[0] * x_prompt.shape[1]
    return (y[:np_].reshape(x_prompt.shape), y[np_:].reshape(x_sample.shape))
```

```python
import functools

import jax
import jax.numpy as jnp
import numpy as np
from jax import lax
from jax.experimental import pallas as pl
from jax.experimental.pallas import tpu as pltpu

D_MODEL = 1024
N_HEADS = 8
QK_NOPE = 64
QK_ROPE = 32
QK_HEAD = QK_NOPE + QK_ROPE
V_HEAD = 64
Q_LORA = 384
KV_LORA = 256
ATTN_W = N_HEADS * V_HEAD
ROPE_THETA = 10000.0
FNET_W = D_MODEL - ATTN_W
FNET_GROUPS = 4
FNET_CH = FNET_W // FNET_GROUPS
PEER_HEADS = 8
PEER_NKEYS = 128
PEER_EXPERTS = PEER_NKEYS * PEER_NKEYS
PEER_HALF = 128
PEER_TOPK = 16
EPS = 1e-6
LOG2_E = 1.4426950408889634

LANES = 128
BF16_ROWS = 16
VMEM_LIMIT = 48 * 1024 * 1024

F32 = jnp.float32
BF16 = jnp.bfloat16


def _rmsnorm(x, g):
    xf = x.astype(F32)
    y = xf * lax.rsqrt(jnp.mean(xf * xf, axis=-1, keepdims=True) + EPS)
    return y * g.astype(F32)


def _rope(x, pos):
    half = QK_ROPE // 2
    freqs = 1.0 / (ROPE_THETA ** (jnp.arange(half, dtype=F32) / half))
    ang = pos[:, None] * freqs[None, :]
    cos = jnp.cos(ang)[None, :, None, :]
    sin = jnp.sin(ang)[None, :, None, :]
    x1, x2 = x[..., :half], x[..., half:]
    return jnp.concatenate([x1 * cos - x2 * sin, x2 * cos + x1 * sin], axis=-1)


def _flash_kernel(q_ref, kt_ref, v_ref, o_ref, m_scr, acc_scr, *, tk):
    seq = v_ref.shape[1]
    tq = q_ref.shape[1]
    nkv = seq // tk
    m_scr[...] = jnp.full(m_scr.shape, -jnp.inf, F32)
    acc_scr[...] = jnp.zeros(acc_scr.shape, F32)

    def body(j, carry):
        start = pl.multiple_of(j * tk, tk)
        heads = [slice(hh * LANES, (hh + 1) * LANES) for hh in range(2)]
        scores = [jnp.dot(q_ref[0, :, hs], kt_ref[0, hs, pl.ds(start, tk)],
                          preferred_element_type=F32) for hs in heads]
        for hh, hs in enumerate(heads):
            cols = [scores[hh][:, c * LANES:(c + 1) * LANES] for c in range(tk // LANES)]
            m_old = m_scr[hh]
            m_new = jnp.maximum(m_old, jnp.max(functools.reduce(jnp.maximum, cols),
                                               axis=1, keepdims=True))
            p = jnp.concatenate([jnp.exp2(c - m_new) for c in cols], axis=1).astype(BF16)
            pv = jnp.dot(p, v_ref[0, pl.ds(start, tk), hs], preferred_element_type=F32)
            acc_scr[hh] = jnp.exp2(m_old - m_new) * acc_scr[hh] + pv
            m_scr[hh] = m_new
        return carry

    lax.fori_loop(0, nkv, body, 0)
    outs = []
    for hh in range(2):
        acc = acc_scr[hh]
        outs.append(acc[:, :V_HEAD] / acc[:, V_HEAD:V_HEAD + 1])
    o_ref[0] = jnp.concatenate(outs, axis=1)


def _flash_attention(q, kt, v, *, tq=512, tk=512):
    b, s, _ = q.shape
    grid = (b, N_HEADS // 2, s // tq)
    return pl.pallas_call(
        functools.partial(_flash_kernel, tk=tk),
        grid=grid,
        in_specs=[
            pl.BlockSpec((1, tq, 2 * LANES), lambda bi, hp, qi: (bi, qi, hp)),
            pl.BlockSpec((1, 2 * LANES, s), lambda bi, hp, qi: (bi, hp, 0)),
            pl.BlockSpec((1, s, 2 * LANES), lambda bi, hp, qi: (bi, 0, hp)),
        ],
        out_specs=pl.BlockSpec((1, tq, LANES), lambda bi, hp, qi: (bi, qi, hp)),
        out_shape=jax.ShapeDtypeStruct((b, s, ATTN_W), F32),
        scratch_shapes=[pltpu.VMEM((2, tq, LANES), F32), pltpu.VMEM((2, tq, LANES), F32)],
        compiler_params=pltpu.CompilerParams(
            dimension_semantics=("arbitrary", "arbitrary", "arbitrary"),
            vmem_limit_bytes=VMEM_LIMIT),
        name="flash_attention",
    )(q, kt, v)


_CAND_PAIRS = tuple((a, b) for a in range(PEER_TOPK) for b in range(PEER_TOPK)
                    if (a + 1) * (b + 1) <= PEER_TOPK)
_NEG_INF = float("-inf")


def _tree_max(xs):
    xs = list(xs)
    while len(xs) > 1:
        nxt = [jnp.maximum(xs[i], xs[i + 1]) for i in range(0, len(xs) - 1, 2)]
        if len(xs) % 2:
            nxt.append(xs[-1])
        xs = nxt
    return xs[0]


def _gate_kernel(hn_ref, wq_ref, keys_ref, rank2_ref, e2_ref, n1_ref, e1_ref, s_scr, vals_scr):
    tm = hn_ref.shape[0]
    ncol = tm // LANES
    q = jnp.dot(hn_ref[...], wq_ref[...], preferred_element_type=F32).astype(BF16)
    for hc in range(2 * PEER_HEADS):
        s_scr[hc] = lax.dot_general(keys_ref[hc], q[:, hc * PEER_HALF:(hc + 1) * PEER_HALF],
                                    (((1,), (1,)), ((), ())), preferred_element_type=F32)

    for hc in range(2 * PEER_HEADS):
        h, c = divmod(hc, 2)

        def col_body(col, carry, hc=hc, h=h, c=c):
            cs = pl.ds(pl.multiple_of(col * LANES, LANES), LANES)
            cur = s_scr[hc, :, cs]
            rank = jnp.full((PEER_NKEYS, LANES), float(PEER_TOPK), F32)
            for r in range(PEER_TOPK):
                m = jnp.max(cur, axis=0, keepdims=True)
                hit = cur == m
                if c == 1:
                    rank = jnp.where(hit, float(r), rank)
                cur = jnp.where(hit, _NEG_INF, cur)
                vals_scr[c, r, h:h + 1, cs] = m
            if c == 1:
                rank2_ref[h, :, cs] = rank
            return carry

        lax.fori_loop(0, ncol, col_body, 0)

    def fin_body(col, carry):
        cs = pl.ds(pl.multiple_of(col * LANES, LANES), LANES)
        v1 = [vals_scr[0, a, :, cs] for a in range(PEER_TOPK)]
        v2 = [vals_scr[1, b, :, cs] for b in range(PEER_TOPK)]
        cands = [v1[a] + v2[b] for a, b in _CAND_PAIRS]
        top = cands[0]
        z = jnp.zeros_like(top)
        m = top
        for r in range(PEER_TOPK):
            m = _tree_max(cands)
            z = z + jnp.exp(m - top)
            if r + 1 < PEER_TOPK:
                cands = [jnp.where(cd == m, _NEG_INF, cd) for cd in cands]
        thr = m
        rz = 1.0 / z
        for h in range(PEER_HEADS):
            thr_h = thr[h:h + 1, :]
            s1 = s_scr[2 * h, :, cs]
            n1 = jnp.zeros((PEER_NKEYS, LANES), F32)
            for b in range(PEER_TOPK):
                n1 = n1 + jnp.where((s1 + v2[b][h:h + 1, :]) >= thr_h, 1.0, 0.0)
            n1_ref[h, :, cs] = n1
            e1_ref[h, :, cs] = jnp.exp(s1 - v1[0][h:h + 1, :])
            s2 = s_scr[2 * h + 1, :, cs]
            e2_ref[h, :, cs] = jnp.exp(s2 - v2[0][h:h + 1, :]) * rz[h:h + 1, :]
        return carry

    lax.fori_loop(0, ncol, fin_body, 0)


def _peer_gate(hn, wq, keys, *, tm=256):
    t = hn.shape[0]
    out = jax.ShapeDtypeStruct((PEER_HEADS, PEER_NKEYS, t), F32)
    ospec = pl.BlockSpec((PEER_HEADS, PEER_NKEYS, tm), lambda ti: (0, 0, ti))
    return pl.pallas_call(
        _gate_kernel,
        grid=(t // tm,),
        in_specs=[
            pl.BlockSpec((tm, D_MODEL), lambda ti: (ti, 0)),
            pl.BlockSpec((D_MODEL, 2 * PEER_HEADS * PEER_HALF), lambda ti: (0, 0)),
            pl.BlockSpec((2 * PEER_HEADS, PEER_NKEYS, PEER_HALF), lambda ti: (0, 0, 0)),
        ],
        out_specs=[ospec, ospec, ospec, ospec],
        out_shape=[out, out, out, out],
        scratch_shapes=[pltpu.VMEM((2 * PEER_HEADS, PEER_NKEYS, tm), F32),
                        pltpu.VMEM((2, PEER_TOPK, PEER_HEADS, tm), F32)],
        compiler_params=pltpu.CompilerParams(
            dimension_semantics=("arbitrary",), vmem_limit_bytes=VMEM_LIMIT),
        name="peer_gate",
    )(hn, wq, keys)


def _gelu_tanh(x):
    c = np.float32(np.sqrt(2.0 / np.pi))
    return 0.5 * x * (1.0 + jnp.tanh(c * (x + np.float32(0.044715) * (x * x * x))))


def _peer_kernel(hn_ref, u_ref, vt_ref, n1_ref, e1_ref, rank2_ref, e2_ref, x_ref,
                 y_ref, acc_ref, a0_scr, a1_scr, g0_scr, g1_scr, r2_scr, e2_scr, *, n_i1):
    eb = pl.program_id(1)
    tm = hn_ref.shape[0]
    rows = PEER_NKEYS // BF16_ROWS

    @pl.when(eb == 0)
    def _():
        acc_ref[...] = jnp.zeros_like(acc_ref)
        a1_scr[...] = jnp.zeros_like(a1_scr)
        g1_scr[...] = jnp.zeros_like(g1_scr)
        r2_scr[...] = rank2_ref[...].astype(BF16)
        e2_scr[...] = e2_ref[...].astype(BF16)

    def step(a_prev, g_prev, a_cur, g_cur):
        half = tm // 2

        def scores(hf):
            return lax.dot_general(u_ref[...], hn_ref[hf * half:(hf + 1) * half, :],
                                   (((1,), (1,)), ((), ())), preferred_element_type=F32)

        def activate(c):
            cs = slice(c * LANES, (c + 1) * LANES)
            return (_gelu_tanh(a_prev[:, cs]) * g_prev[:, cs].astype(F32)).astype(BF16)

        def retrieve(p):
            return jnp.dot(vt_ref[0], p, preferred_element_type=F32)

        def build_gate(c):
            cs = slice(c * LANES, (c + 1) * LANES)
            gs = [jnp.zeros((rows, BF16_ROWS, LANES), BF16) for _ in range(n_i1)]
            for h in range(PEER_HEADS):
                r2 = r2_scr[h, :, cs].reshape(rows, BF16_ROWS, LANES)
                e2 = e2_scr[h, :, cs].reshape(rows, BF16_ROWS, LANES)
                for j in range(n_i1):
                    n1row = jnp.broadcast_to(n1_ref[h, 0, j:j + 1, cs], (BF16_ROWS, LANES)).astype(BF16)
                    e1row = jnp.broadcast_to(e1_ref[h, 0, j:j + 1, cs], (BF16_ROWS, LANES)).astype(BF16)
                    gs[j] = gs[j] + jnp.where(r2 < n1row[None], e2, jnp.zeros_like(e2)) * e1row[None]
            for j in range(n_i1):
                g_cur[j * PEER_NKEYS:(j + 1) * PEER_NKEYS, cs] = gs[j].reshape(PEER_NKEYS, LANES)

        ncol = half // LANES
        a_new0 = scores(0)
        p0 = jnp.concatenate([activate(c) for c in range(ncol)], axis=1)
        out0 = retrieve(p0)
        p1 = jnp.concatenate([activate(c) for c in range(ncol, 2 * ncol)], axis=1)
        build_gate(0)
        out1 = retrieve(p1)
        build_gate(1)
        build_gate(2)
        a_new1 = scores(1)
        build_gate(3)
        acc_ref[:, :half] += out0
        acc_ref[:, half:] += out1
        a_cur[:, :half] = a_new0
        a_cur[:, half:] = a_new1

    @pl.when(eb % 2 == 0)
    def _():
        step(a1_scr, g1_scr, a0_scr, g0_scr)

    @pl.when(eb % 2 == 1)
    def _():
        step(a0_scr, g0_scr, a1_scr, g1_scr)

    @pl.when(eb == pl.num_programs(1) - 1)
    def _():
        y_ref[...] = x_ref[...] + acc_ref[...].T


def _peer_dense(hn, u, v, n1t, e1t, rank2t, e2t, x1, *, tm=512, n_i1=4):
    t = hn.shape[0]
    te = n_i1 * PEER_NKEYS
    vt = v.reshape(PEER_EXPERTS // te, te, D_MODEL).transpose(0, 2, 1)
    nblk = PEER_NKEYS // n_i1
    n1r = n1t.reshape(PEER_HEADS, nblk, n_i1, t)
    e1r = e1t.reshape(PEER_HEADS, nblk, n_i1, t)
    grid = (t // tm, nblk + 1)
    this_blk = lambda eb: jnp.minimum(eb, nblk - 1)
    prev_blk = lambda eb: jnp.maximum(eb - 1, 0)
    return pl.pallas_call(
        functools.partial(_peer_kernel, n_i1=n_i1),
        grid=grid,
        in_specs=[
            pl.BlockSpec((tm, D_MODEL), lambda ti, eb: (ti, 0)),
            pl.BlockSpec((te, D_MODEL), lambda ti, eb: (this_blk(eb), 0)),
            pl.BlockSpec((1, D_MODEL, te), lambda ti, eb: (prev_blk(eb), 0, 0)),
            pl.BlockSpec((PEER_HEADS, 1, n_i1, tm), lambda ti, eb: (0, this_blk(eb), 0, ti)),
            pl.BlockSpec((PEER_HEADS, 1, n_i1, tm), lambda ti, eb: (0, this_blk(eb), 0, ti)),
            pl.BlockSpec((PEER_HEADS, PEER_NKEYS, tm), lambda ti, eb: (0, 0, ti)),
            pl.BlockSpec((PEER_HEADS, PEER_NKEYS, tm), lambda ti, eb: (0, 0, ti)),
            pl.BlockSpec((tm, D_MODEL), lambda ti, eb: (ti, 0)),
        ],
        out_specs=pl.BlockSpec((tm, D_MODEL), lambda ti, eb: (ti, 0)),
        out_shape=jax.ShapeDtypeStruct((t, D_MODEL), F32),
        scratch_shapes=[pltpu.VMEM((D_MODEL, tm), F32),
                        pltpu.VMEM((te, tm), F32), pltpu.VMEM((te, tm), F32),
                        pltpu.VMEM((te, tm), BF16), pltpu.VMEM((te, tm), BF16),
                        pltpu.VMEM((PEER_HEADS, PEER_NKEYS, tm), BF16),
                        pltpu.VMEM((PEER_HEADS, PEER_NKEYS, tm), BF16)],
        compiler_params=pltpu.CompilerParams(
            dimension_semantics=("arbitrary", "arbitrary"),
            vmem_limit_bytes=VMEM_LIMIT),
        name="peer_dense",
    )(hn, u, vt, n1r, e1r, rank2t, e2t, x1)


def _qkv(x, attn_norm_g, w_in, q_lat_g, w_uq, kv_lat_g, w_ukv, q_head_g, k_head_g):
    b, s, _ = x.shape
    h = _rmsnorm(x, attn_norm_g)
    p = h @ w_in
    o0, o1, o2 = Q_LORA, Q_LORA + KV_LORA, Q_LORA + KV_LORA + QK_ROPE
    c_q, c_kv, k_r, f = p[..., :o0], p[..., o0:o1], p[..., o1:o2], p[..., o2:]
    q = (_rmsnorm(c_q, q_lat_g) @ w_uq).reshape(b, s, N_HEADS, QK_HEAD)
    kv = (_rmsnorm(c_kv, kv_lat_g) @ w_ukv).reshape(b, s, N_HEADS, QK_NOPE + V_HEAD)
    k_nope, v = kv[..., :QK_NOPE], kv[..., QK_NOPE:]
    k = jnp.concatenate([k_nope, jnp.broadcast_to(k_r[:, :, None, :], (b, s, N_HEADS, QK_ROPE))], axis=-1)
    q = _rmsnorm(q, q_head_g) * (QK_HEAD ** -0.5 * LOG2_E)
    k = _rmsnorm(k, k_head_g)
    pos = jnp.arange(s, dtype=F32)
    q = jnp.concatenate([q[..., :QK_NOPE], _rope(q[..., QK_NOPE:], pos)], axis=-1)
    k = jnp.concatenate([k[..., :QK_NOPE], _rope(k[..., QK_NOPE:], pos)], axis=-1)
    pad = ((0, 0), (0, 0), (0, 0), (0, LANES - QK_HEAD))
    q = jnp.pad(q, pad).reshape(b, s, N_HEADS * LANES).astype(BF16)
    k = jnp.pad(k, pad).reshape(b, s, N_HEADS * LANES).astype(BF16).transpose(0, 2, 1)
    ones = jnp.ones((b, s, N_HEADS, 1), F32)
    v = jnp.concatenate([v, ones], axis=-1)
    v = jnp.pad(v, ((0, 0), (0, 0), (0, 0), (0, LANES - V_HEAD - 1)))
    v = v.reshape(b, s, N_HEADS * LANES).astype(BF16)
    return q, k, v, f


def _fnet(f):
    b, s, _ = f.shape
    z = f.reshape(b, s, FNET_GROUPS, FNET_CH)
    y = jnp.fft.fft2(z, axes=(1, 3), norm='ortho').real
    return y.reshape(b, s, FNET_W)


def _mixer(x, attn_norm_g, w_in, q_lat_g, w_uq, kv_lat_g, w_ukv, q_head_g, k_head_g,
           attn_out_g, fnet_out_g, w_out):
    q, k, v, f = _qkv(x, attn_norm_g, w_in, q_lat_g, w_uq, kv_lat_g, w_ukv, q_head_g, k_head_g)
    a = _flash_attention(q, k, v)
    fm = _fnet(f)
    mixed = jnp.concatenate([_rmsnorm(a, attn_out_g), _rmsnorm(fm, fnet_out_g)], axis=-1)
    return x + mixed @ w_out


def kernel(x_prompt, x_sample, attn_norm_g, w_in, q_lat_g, w_uq, kv_lat_g, w_ukv, q_head_g,
           k_head_g, attn_out_g, fnet_out_g, w_out, ffn_norm_g, peer_w_q, peer_sub_keys,
           peer_u, peer_v):
    l = 0
    mix = functools.partial(
        _mixer, attn_norm_g=attn_norm_g[l], w_in=w_in[l], q_lat_g=q_lat_g[l], w_uq=w_uq[l],
        kv_lat_g=kv_lat_g[l], w_ukv=w_ukv[l], q_head_g=q_head_g[l], k_head_g=k_head_g[l],
        attn_out_g=attn_out_g[l], fnet_out_g=fnet_out_g[l], w_out=w_out[l])
    x1p = mix(x_prompt)
    x1s = mix(x_sample)
    x1 = jnp.concatenate([x1p.reshape(-1, D_MODEL), x1s.reshape(-1, D_MODEL)], axis=0)
    hn = _rmsnorm(x1, ffn_norm_g[l]).astype(BF16)
    keys = peer_sub_keys[l].reshape(2 * PEER_HEADS, PEER_NKEYS, PEER_HALF).astype(BF16)
    rank2t, e2t, n1t, e1t = _peer_gate(hn, peer_w_q[l].astype(BF16), keys)
    u = peer_u[l].astype(BF16)
    y = _peer_dense(hn, u, peer_v[l].astype(BF16), n1t, e1t, rank2t, e2t, x1)
    np_ = x_prompt.shape[0] * x_prompt.shape[1]
    return (y[:np_].reshape(x_prompt.shape), y[np_:].reshape(x_sample.shape))
```

```python
import functools

import jax
import jax.numpy as jnp
import numpy as np
from jax import lax
from jax.experimental import pallas as pl
from jax.experimental.pallas import tpu as pltpu

D_MODEL = 1024
N_HEADS = 8
QK_NOPE = 64
QK_ROPE = 32
QK_HEAD = QK_NOPE + QK_ROPE
V_HEAD = 64
Q_LORA = 384
KV_LORA = 256
ATTN_W = N_HEADS * V_HEAD
ROPE_THETA = 10000.0
FNET_W = D_MODEL - ATTN_W
FNET_GROUPS = 4
FNET_CH = FNET_W // FNET_GROUPS
PEER_HEADS = 8
PEER_NKEYS = 128
PEER_EXPERTS = PEER_NKEYS * PEER_NKEYS
PEER_HALF = 128
PEER_TOPK = 16
EPS = 1e-6
LOG2_E = 1.4426950408889634

LANES = 128
BF16_ROWS = 16
VMEM_LIMIT = 48 * 1024 * 1024

F32 = jnp.float32
BF16 = jnp.bfloat16


def _rmsnorm(x, g):
    xf = x.astype(F32)
    y = xf * lax.rsqrt(jnp.mean(xf * xf, axis=-1, keepdims=True) + EPS)
    return y * g.astype(F32)


def _rope(x, pos):
    half = QK_ROPE // 2
    freqs = 1.0 / (ROPE_THETA ** (jnp.arange(half, dtype=F32) / half))
    ang = pos[:, None] * freqs[None, :]
    cos = jnp.cos(ang)[None, :, None, :]
    sin = jnp.sin(ang)[None, :, None, :]
    x1, x2 = x[..., :half], x[..., half:]
    return jnp.concatenate([x1 * cos - x2 * sin, x2 * cos + x1 * sin], axis=-1)


def _flash_kernel(q_ref, kt_ref, v_ref, o_ref, m_scr, acc_scr, *, tk):
    seq = v_ref.shape[1]
    tq = q_ref.shape[1]
    nkv = seq // tk
    m_scr[...] = jnp.full(m_scr.shape, -jnp.inf, F32)
    acc_scr[...] = jnp.zeros(acc_scr.shape, F32)

    def body(j, carry):
        start = pl.multiple_of(j * tk, tk)
        heads = [slice(hh * LANES, (hh + 1) * LANES) for hh in range(2)]
        scores = [jnp.dot(q_ref[0, :, hs], kt_ref[0, hs, pl.ds(start, tk)],
                          preferred_element_type=F32) for hs in heads]
        for hh, hs in enumerate(heads):
            cols = [scores[hh][:, c * LANES:(c + 1) * LANES] for c in range(tk // LANES)]
            m_old = m_scr[hh]
            m_new = jnp.maximum(m_old, jnp.max(functools.reduce(jnp.maximum, cols),
                                               axis=1, keepdims=True))
            p = jnp.concatenate([jnp.exp2(c - m_new) for c in cols], axis=1).astype(BF16)
            pv = jnp.dot(p, v_ref[0, pl.ds(start, tk), hs], preferred_element_type=F32)
            acc_scr[hh] = jnp.exp2(m_old - m_new) * acc_scr[hh] + pv
            m_scr[hh] = m_new
        return carry

    lax.fori_loop(0, nkv, body, 0)
    outs = []
    for hh in range(2):
        acc = acc_scr[hh]
        outs.append(acc[:, :V_HEAD] / acc[:, V_HEAD:V_HEAD + 1])
    o_ref[0] = jnp.concatenate(outs, axis=1)


def _flash_attention(q, kt, v, *, tq=512, tk=512):
    b, s, _ = q.shape
    grid = (b, N_HEADS // 2, s // tq)
    return pl.pallas_call(
        functools.partial(_flash_kernel, tk=tk),
        grid=grid,
        in_specs=[
            pl.BlockSpec((1, tq, 2 * LANES), lambda bi, hp, qi: (bi, qi, hp)),
            pl.BlockSpec((1, 2 * LANES, s), lambda bi, hp, qi: (bi, hp, 0)),
            pl.BlockSpec((1, s, 2 * LANES), lambda bi, hp, qi: (bi, 0, hp)),
        ],
        out_specs=pl.BlockSpec((1, tq, LANES), lambda bi, hp, qi: (bi, qi, hp)),
        out_shape=jax.ShapeDtypeStruct((b, s, ATTN_W), F32),
        scratch_shapes=[pltpu.VMEM((2, tq, LANES), F32), pltpu.VMEM((2, tq, LANES), F32)],
        compiler_params=pltpu.CompilerParams(
            dimension_semantics=("arbitrary", "arbitrary", "arbitrary"),
            vmem_limit_bytes=VMEM_LIMIT),
        name="flash_attention",
    )(q, kt, v)


_CAND_PAIRS = tuple((a, b) for a in range(PEER_TOPK) for b in range(PEER_TOPK)
                    if (a + 1) * (b + 1) <= PEER_TOPK)
_NEG_INF = float("-inf")


def _tree_max(xs):
    xs = list(xs)
    while len(xs) > 1:
        nxt = [jnp.maximum(xs[i], xs[i + 1]) for i in range(0, len(xs) - 1, 2)]
        if len(xs) % 2:
            nxt.append(xs[-1])
        xs = nxt
    return xs[0]


def _gate_kernel(hn_ref, wq_ref, keys_ref, rank2_ref, e2_ref, n1_ref, e1_ref, s_scr, vals_scr):
    tm = hn_ref.shape[0]
    ncol = tm // LANES
    q = jnp.dot(hn_ref[...], wq_ref[...], preferred_element_type=F32).astype(BF16)
    for hc in range(2 * PEER_HEADS):
        s_scr[hc] = lax.dot_general(keys_ref[hc], q[:, hc * PEER_HALF:(hc + 1) * PEER_HALF],
                                    (((1,), (1,)), ((), ())), preferred_element_type=F32)

    for hc in range(2 * PEER_HEADS):
        h, c = divmod(hc, 2)

        def col_body(col, carry, hc=hc, h=h, c=c):
            cs = pl.ds(pl.multiple_of(col * LANES, LANES), LANES)
            cur = s_scr[hc, :, cs]
            rank = jnp.full((PEER_NKEYS, LANES), float(PEER_TOPK), F32)
            for r in range(PEER_TOPK):
                m = jnp.max(cur, axis=0, keepdims=True)
                hit = cur == m
                if c == 1:
                    rank = jnp.where(hit, float(r), rank)
                cur = jnp.where(hit, _NEG_INF, cur)
                vals_scr[c, r, h:h + 1, cs] = m
            if c == 1:
                rank2_ref[h, :, cs] = rank
            return carry

        lax.fori_loop(0, ncol, col_body, 0)

    def fin_body(col, carry):
        cs = pl.ds(pl.multiple_of(col * LANES, LANES), LANES)
        v1 = [vals_scr[0, a, :, cs] for a in range(PEER_TOPK)]
        v2 = [vals_scr[1, b, :, cs] for b in range(PEER_TOPK)]
        cands = [v1[a] + v2[b] for a, b in _CAND_PAIRS]
        top = cands[0]
        z = jnp.zeros_like(top)
        m = top
        for r in range(PEER_TOPK):
            m = _tree_max(cands)
            z = z + jnp.exp(m - top)
            if r + 1 < PEER_TOPK:
                cands = [jnp.where(cd == m, _NEG_INF, cd) for cd in cands]
        thr = m
        rz = 1.0 / z
        for h in range(PEER_HEADS):
            thr_h = thr[h:h + 1, :]
            s1 = s_scr[2 * h, :, cs]
            n1 = jnp.zeros((PEER_NKEYS, LANES), F32)
            for b in range(PEER_TOPK):
                n1 = n1 + jnp.where((s1 + v2[b][h:h + 1, :]) >= thr_h, 1.0, 0.0)
            n1_ref[h, :, cs] = n1
            e1_ref[h, :, cs] = jnp.exp(s1 - v1[0][h:h + 1, :])
            s2 = s_scr[2 * h + 1, :, cs]
            e2_ref[h, :, cs] = jnp.exp(s2 - v2[0][h:h + 1, :]) * rz[h:h + 1, :]
        return carry

    lax.fori_loop(0, ncol, fin_body, 0)


def _peer_gate(hn, wq, keys, *, tm=256):
    t = hn.shape[0]
    out = jax.ShapeDtypeStruct((PEER_HEADS, PEER_NKEYS, t), F32)
    ospec = pl.BlockSpec((PEER_HEADS, PEER_NKEYS, tm), lambda ti: (0, 0, ti))
    return pl.pallas_call(
        _gate_kernel,
        grid=(t // tm,),
        in_specs=[
            pl.BlockSpec((tm, D_MODEL), lambda ti: (ti, 0)),
            pl.BlockSpec((D_MODEL, 2 * PEER_HEADS * PEER_HALF), lambda ti: (0, 0)),
            pl.BlockSpec((2 * PEER_HEADS, PEER_NKEYS, PEER_HALF), lambda ti: (0, 0, 0)),
        ],
        out_specs=[ospec, ospec, ospec, ospec],
        out_shape=[out, out, out, out],
        scratch_shapes=[pltpu.VMEM((2 * PEER_HEADS, PEER_NKEYS, tm), F32),
                        pltpu.VMEM((2, PEER_TOPK, PEER_HEADS, tm), F32)],
        compiler_params=pltpu.CompilerParams(
            dimension_semantics=("arbitrary",), vmem_limit_bytes=VMEM_LIMIT),
        name="peer_gate",
    )(hn, wq, keys)


def _gelu_tanh(x):
    c = np.float32(np.sqrt(2.0 / np.pi))
    return 0.5 * x * (1.0 + jnp.tanh(c * (x + np.float32(0.044715) * (x * x * x))))


def _peer_kernel(hnt_ref, u_ref, vt_ref, n1_ref, e1_ref, rank2_ref, e2_ref, x_ref,
                 y_ref, acc_ref, a0_scr, a1_scr, g0_scr, g1_scr, r2_scr, e2_scr, *, n_i1):
    eb = pl.program_id(1)
    tm = hnt_ref.shape[1]
    rows = PEER_NKEYS // BF16_ROWS

    @pl.when(eb == 0)
    def _():
        acc_ref[...] = jnp.zeros_like(acc_ref)
        a1_scr[...] = jnp.zeros_like(a1_scr)
        g1_scr[...] = jnp.zeros_like(g1_scr)
        r2_scr[...] = rank2_ref[...].astype(BF16)
        e2_scr[...] = e2_ref[...].astype(BF16)

    def step(a_prev, g_prev, a_cur, g_cur):
        half = tm // 2

        def scores(hf):
            return jnp.dot(u_ref[...], hnt_ref[:, hf * half:(hf + 1) * half],
                           preferred_element_type=F32)

        def activate(c):
            cs = slice(c * LANES, (c + 1) * LANES)
            return (_gelu_tanh(a_prev[:, cs]) * g_prev[:, cs].astype(F32)).astype(BF16)

        def retrieve(p):
            return jnp.dot(vt_ref[0], p, preferred_element_type=F32)

        def build_gate(c):
            cs = slice(c * LANES, (c + 1) * LANES)
            gs = [jnp.zeros((rows, BF16_ROWS, LANES), BF16) for _ in range(n_i1)]
            for h in range(PEER_HEADS):
                r2 = r2_scr[h, :, cs].reshape(rows, BF16_ROWS, LANES)
                e2 = e2_scr[h, :, cs].reshape(rows, BF16_ROWS, LANES)
                for j in range(n_i1):
                    n1row = jnp.broadcast_to(n1_ref[h, 0, j:j + 1, cs], (BF16_ROWS, LANES)).astype(BF16)
                    e1row = jnp.broadcast_to(e1_ref[h, 0, j:j + 1, cs], (BF16_ROWS, LANES)).astype(BF16)
                    gs[j] = gs[j] + jnp.where(r2 < n1row[None], e2, jnp.zeros_like(e2)) * e1row[None]
            for j in range(n_i1):
                g_cur[j * PEER_NKEYS:(j + 1) * PEER_NKEYS, cs] = gs[j].reshape(PEER_NKEYS, LANES)

        ncol = half // LANES
        p0 = jnp.concatenate([activate(c) for c in range(ncol)], axis=1)
        p1 = jnp.concatenate([activate(c) for c in range(ncol, 2 * ncol)], axis=1)
        out0 = retrieve(p0)
        out1 = retrieve(p1)
        a_new0 = scores(0)
        a_new1 = scores(1)
        build_gate(0)
        build_gate(1)
        build_gate(2)
        build_gate(3)
        acc_ref[:, :half] += out0
        acc_ref[:, half:] += out1
        a_cur[:, :half] = a_new0
        a_cur[:, half:] = a_new1

    @pl.when(eb % 2 == 0)
    def _():
        step(a1_scr, g1_scr, a0_scr, g0_scr)

    @pl.when(eb % 2 == 1)
    def _():
        step(a0_scr, g0_scr, a1_scr, g1_scr)

    @pl.when(eb == pl.num_programs(1) - 1)
    def _():
        y_ref[...] = x_ref[...] + acc_ref[...].T


def _peer_dense(hnt, u, v, n1t, e1t, rank2t, e2t, x1, *, tm=512, n_i1=4):
    t = hnt.shape[1]
    te = n_i1 * PEER_NKEYS
    vt = v.reshape(PEER_EXPERTS // te, te, D_MODEL).transpose(0, 2, 1)
    nblk = PEER_NKEYS // n_i1
    n1r = n1t.reshape(PEER_HEADS, nblk, n_i1, t)
    e1r = e1t.reshape(PEER_HEADS, nblk, n_i1, t)
    grid = (t // tm, nblk + 1)
    this_blk = lambda eb: jnp.minimum(eb, nblk - 1)
    prev_blk = lambda eb: jnp.maximum(eb - 1, 0)
    return pl.pallas_call(
        functools.partial(_peer_kernel, n_i1=n_i1),
        grid=grid,
        in_specs=[
            pl.BlockSpec((D_MODEL, tm), lambda ti, eb: (0, ti)),
            pl.BlockSpec((te, D_MODEL), lambda ti, eb: (this_blk(eb), 0)),
            pl.BlockSpec((1, D_MODEL, te), lambda ti, eb: (prev_blk(eb), 0, 0)),
            pl.BlockSpec((PEER_HEADS, 1, n_i1, tm), lambda ti, eb: (0, this_blk(eb), 0, ti)),
            pl.BlockSpec((PEER_HEADS, 1, n_i1, tm), lambda ti, eb: (0, this_blk(eb), 0, ti)),
            pl.BlockSpec((PEER_HEADS, PEER_NKEYS, tm), lambda ti, eb: (0, 0, ti)),
            pl.BlockSpec((PEER_HEADS, PEER_NKEYS, tm), lambda ti, eb: (0, 0, ti)),
            pl.BlockSpec((tm, D_MODEL), lambda ti, eb: (ti, 0)),
        ],
        out_specs=pl.BlockSpec((tm, D_MODEL), lambda ti, eb: (ti, 0)),
        out_shape=jax.ShapeDtypeStruct((t, D_MODEL), F32),
        scratch_shapes=[pltpu.VMEM((D_MODEL, tm), F32),
                        pltpu.VMEM((te, tm), F32), pltpu.VMEM((te, tm), F32),
                        pltpu.VMEM((te, tm), BF16), pltpu.VMEM((te, tm), BF16),
                        pltpu.VMEM((PEER_HEADS, PEER_NKEYS, tm), BF16),
                        pltpu.VMEM((PEER_HEADS, PEER_NKEYS, tm), BF16)],
        compiler_params=pltpu.CompilerParams(
            dimension_semantics=("arbitrary", "arbitrary"),
            vmem_limit_bytes=VMEM_LIMIT),
        name="peer_dense",
    )(hnt, u, vt, n1r, e1r, rank2t, e2t, x1)


def _qkv(x, attn_norm_g, w_in, q_lat_g, w_uq, kv_lat_g, w_ukv, q_head_g, k_head_g):
    b, s, _ = x.shape
    h = _rmsnorm(x, attn_norm_g)
    p = h @ w_in
    o0, o1, o2 = Q_LORA, Q_LORA + KV_LORA, Q_LORA + KV_LORA + QK_ROPE
    c_q, c_kv, k_r, f = p[..., :o0], p[..., o0:o1], p[..., o1:o2], p[..., o2:]
    q = (_rmsnorm(c_q, q_lat_g) @ w_uq).reshape(b, s, N_HEADS, QK_HEAD)
    kv = (_rmsnorm(c_kv, kv_lat_g) @ w_ukv).reshape(b, s, N_HEADS, QK_NOPE + V_HEAD)
    k_nope, v = kv[..., :QK_NOPE], kv[..., QK_NOPE:]
    k = jnp.concatenate([k_nope, jnp.broadcast_to(k_r[:, :, None, :], (b, s, N_HEADS, QK_ROPE))], axis=-1)
    q = _rmsnorm(q, q_head_g) * (QK_HEAD ** -0.5 * LOG2_E)
    k = _rmsnorm(k, k_head_g)
    pos = jnp.arange(s, dtype=F32)
    q = jnp.concatenate([q[..., :QK_NOPE], _rope(q[..., QK_NOPE:], pos)], axis=-1)
    k = jnp.concatenate([k[..., :QK_NOPE], _rope(k[..., QK_NOPE:], pos)], axis=-1)
    pad = ((0, 0), (0, 0), (0, 0), (0, LANES - QK_HEAD))
    q = jnp.pad(q, pad).reshape(b, s, N_HEADS * LANES).astype(BF16)
    k = jnp.pad(k, pad).reshape(b, s, N_HEADS * LANES).astype(BF16).transpose(0, 2, 1)
    ones = jnp.ones((b, s, N_HEADS, 1), F32)
    v = jnp.concatenate([v, ones], axis=-1)
    v = jnp.pad(v, ((0, 0), (0, 0), (0, 0), (0, LANES - V_HEAD - 1)))
    v = v.reshape(b, s, N_HEADS * LANES).astype(BF16)
    return q, k, v, f


def _fnet(f):
    b, s, _ = f.shape
    z = f.reshape(b, s, FNET_GROUPS, FNET_CH)
    y = jnp.fft.fft2(z, axes=(1, 3), norm='ortho').real
    return y.reshape(b, s, FNET_W)


def _mixer(x, attn_norm_g, w_in, q_lat_g, w_uq, kv_lat_g, w_ukv, q_head_g, k_head_g,
           attn_out_g, fnet_out_g, w_out):
    q, k, v, f = _qkv(x, attn_norm_g, w_in, q_lat_g, w_uq, kv_lat_g, w_ukv, q_head_g, k_head_g)
    a = _flash_attention(q, k, v)
    fm = _fnet(f)
    mixed = jnp.concatenate([_rmsnorm(a, attn_out_g), _rmsnorm(fm, fnet_out_g)], axis=-1)
    return x + mixed @ w_out


def kernel(x_prompt, x_sample, attn_norm_g, w_in, q_lat_g, w_uq, kv_lat_g, w_ukv, q_head_g,
           k_head_g, attn_out_g, fnet_out_g, w_out, ffn_norm_g, peer_w_q, peer_sub_keys,
           peer_u, peer_v):
    l = 0
    mix = functools.partial(
        _mixer, attn_norm_g=attn_norm_g[l], w_in=w_in[l], q_lat_g=q_lat_g[l], w_uq=w_uq[l],
        kv_lat_g=kv_lat_g[l], w_ukv=w_ukv[l], q_head_g=q_head_g[l], k_head_g=k_head_g[l],
        attn_out_g=attn_out_g[l], fnet_out_g=fnet_out_g[l], w_out=w_out[l])
    x1p = mix(x_prompt)
    x1s = mix(x_sample)
    x1 = jnp.concatenate([x1p.reshape(-1, D_MODEL), x1s.reshape(-1, D_MODEL)], axis=0)
    hn = _rmsnorm(x1, ffn_norm_g[l]).astype(BF16)
    keys = peer_sub_keys[l].reshape(2 * PEER_HEADS, PEER_NKEYS, PEER_HALF).astype(BF16)
    rank2t, e2t, n1t, e1t = _peer_gate(hn, peer_w_q[l].astype(BF16), keys)
    u = peer_u[l].astype(BF16)
    y = _peer_dense(hn.T, u, peer_v[l].astype(BF16), n1t, e1t, rank2t, e2t, x1)
    np_ = x_prompt.shape[0] * x_prompt.shape[1]
    return (y[:np_].reshape(x_prompt.shape), y[np_:].reshape(x_sample.shape))
```

```python
import functools

import jax
import jax.numpy as jnp
import numpy as np
from jax import lax
from jax.experimental import pallas as pl
from jax.experimental.pallas import tpu as pltpu

D_MODEL = 1024
N_HEADS = 8
QK_NOPE = 64
QK_ROPE = 32
QK_HEAD = QK_NOPE + QK_ROPE
V_HEAD = 64
Q_LORA = 384
KV_LORA = 256
ATTN_W = N_HEADS * V_HEAD
ROPE_THETA = 10000.0
FNET_W = D_MODEL - ATTN_W
FNET_GROUPS = 4
FNET_CH = FNET_W // FNET_GROUPS
PEER_HEADS = 8
PEER_NKEYS = 128
PEER_EXPERTS = PEER_NKEYS * PEER_NKEYS
PEER_HALF = 128
PEER_TOPK = 16
EPS = 1e-6
LOG2_E = 1.4426950408889634

LANES = 128
BF16_ROWS = 16
GATE_GROUP = 4
VMEM_LIMIT = 48 * 1024 * 1024

F32 = jnp.float32
BF16 = jnp.bfloat16


def _rmsnorm(x, g):
    xf = x.astype(F32)
    y = xf * lax.rsqrt(jnp.mean(xf * xf, axis=-1, keepdims=True) + EPS)
    return y * g.astype(F32)


def _rope(x, pos):
    half = QK_ROPE // 2
    freqs = 1.0 / (ROPE_THETA ** (jnp.arange(half, dtype=F32) / half))
    ang = pos[:, None] * freqs[None, :]
    cos = jnp.cos(ang)[None, :, None, :]
    sin = jnp.sin(ang)[None, :, None, :]
    x1, x2 = x[..., :half], x[..., half:]
    return jnp.concatenate([x1 * cos - x2 * sin, x2 * cos + x1 * sin], axis=-1)


def _flash_kernel(q_ref, kt_ref, v_ref, o_ref, m_scr, acc_scr, *, tk):
    seq = v_ref.shape[1]
    tq = q_ref.shape[1]
    nkv = seq // tk
    m_scr[...] = jnp.full(m_scr.shape, -jnp.inf, F32)
    acc_scr[...] = jnp.zeros(acc_scr.shape, F32)

    def body(j, carry):
        start = pl.multiple_of(j * tk, tk)
        heads = [slice(hh * LANES, (hh + 1) * LANES) for hh in range(2)]
        scores = [jnp.dot(q_ref[0, :, hs], kt_ref[0, hs, pl.ds(start, tk)],
                          preferred_element_type=F32) for hs in heads]
        for hh, hs in enumerate(heads):
            cols = [scores[hh][:, c * LANES:(c + 1) * LANES] for c in range(tk // LANES)]
            m_old = m_scr[hh]
            m_new = jnp.maximum(m_old, jnp.max(functools.reduce(jnp.maximum, cols),
                                               axis=1, keepdims=True))
            p = jnp.concatenate([jnp.exp2(c - m_new) for c in cols], axis=1).astype(BF16)
            pv = jnp.dot(p, v_ref[0, pl.ds(start, tk), hs], preferred_element_type=F32)
            acc_scr[hh] = jnp.exp2(m_old - m_new) * acc_scr[hh] + pv
            m_scr[hh] = m_new
        return carry

    lax.fori_loop(0, nkv, body, 0)
    outs = []
    for hh in range(2):
        acc = acc_scr[hh]
        outs.append(acc[:, :V_HEAD] / acc[:, V_HEAD:V_HEAD + 1])
    o_ref[0] = jnp.concatenate(outs, axis=1)


def _flash_attention(q, kt, v, *, tq=512, tk=512):
    b, s, _ = q.shape
    grid = (b, N_HEADS // 2, s // tq)
    return pl.pallas_call(
        functools.partial(_flash_kernel, tk=tk),
        grid=grid,
        in_specs=[
            pl.BlockSpec((1, tq, 2 * LANES), lambda bi, hp, qi: (bi, qi, hp)),
            pl.BlockSpec((1, 2 * LANES, s), lambda bi, hp, qi: (bi, hp, 0)),
            pl.BlockSpec((1, s, 2 * LANES), lambda bi, hp, qi: (bi, 0, hp)),
        ],
        out_specs=pl.BlockSpec((1, tq, LANES), lambda bi, hp, qi: (bi, qi, hp)),
        out_shape=jax.ShapeDtypeStruct((b, s, ATTN_W), F32),
        scratch_shapes=[pltpu.VMEM((2, tq, LANES), F32), pltpu.VMEM((2, tq, LANES), F32)],
        compiler_params=pltpu.CompilerParams(
            dimension_semantics=("arbitrary", "arbitrary", "arbitrary"),
            vmem_limit_bytes=VMEM_LIMIT),
        name="flash_attention",
    )(q, kt, v)


_CAND_PAIRS = tuple((a, b) for a in range(PEER_TOPK) for b in range(PEER_TOPK)
                    if (a + 1) * (b + 1) <= PEER_TOPK)
_NEG_INF = float("-inf")


def _tree_max(xs):
    xs = list(xs)
    while len(xs) > 1:
        nxt = [jnp.maximum(xs[i], xs[i + 1]) for i in range(0, len(xs) - 1, 2)]
        if len(xs) % 2:
            nxt.append(xs[-1])
        xs = nxt
    return xs[0]


def _gate_kernel(hn_ref, wq_ref, keys_ref, rank2_ref, e2_ref, n1_ref, e1_ref, s_scr, vals_scr):
    tm = hn_ref.shape[0]
    ncol = tm // LANES
    q = jnp.dot(hn_ref[...], wq_ref[...], preferred_element_type=F32).astype(BF16)
    for hc in range(2 * PEER_HEADS):
        s_scr[hc] = lax.dot_general(keys_ref[hc], q[:, hc * PEER_HALF:(hc + 1) * PEER_HALF],
                                    (((1,), (1,)), ((), ())), preferred_element_type=F32)

    for hc in range(2 * PEER_HEADS):
        h, c = divmod(hc, 2)

        def col_body(col, carry, hc=hc, h=h, c=c):
            cs = pl.ds(pl.multiple_of(col * LANES, LANES), LANES)
            cur = s_scr[hc, :, cs]
            rank = jnp.full((PEER_NKEYS, LANES), float(PEER_TOPK), F32)
            for r in range(PEER_TOPK):
                m = jnp.max(cur, axis=0, keepdims=True)
                hit = cur == m
                if c == 1:
                    rank = jnp.where(hit, float(r), rank)
                cur = jnp.where(hit, _NEG_INF, cur)
                vals_scr[c, r, h:h + 1, cs] = m
            if c == 1:
                rank2_ref[h, :, cs] = rank
            return carry

        lax.fori_loop(0, ncol, col_body, 0)

    def fin_body(col, carry):
        cs = pl.ds(pl.multiple_of(col * LANES, LANES), LANES)
        v1 = [vals_scr[0, a, :, cs] for a in range(PEER_TOPK)]
        v2 = [vals_scr[1, b, :, cs] for b in range(PEER_TOPK)]
        cands = [v1[a] + v2[b] for a, b in _CAND_PAIRS]
        top = cands[0]
        z = jnp.zeros_like(top)
        m = top
        for r in range(PEER_TOPK):
            m = _tree_max(cands)
            z = z + jnp.exp(m - top)
            if r + 1 < PEER_TOPK:
                cands = [jnp.where(cd == m, _NEG_INF, cd) for cd in cands]
        thr = m
        rz = 1.0 / z
        for h in range(PEER_HEADS):
            thr_h = thr[h:h + 1, :]
            s1 = s_scr[2 * h, :, cs]
            n1 = jnp.zeros((PEER_NKEYS, LANES), F32)
            for b in range(PEER_TOPK):
                n1 = n1 + jnp.where((s1 + v2[b][h:h + 1, :]) >= thr_h, 1.0, 0.0)
            n1_ref[h, :, cs] = n1
            e1_ref[h, :, cs] = jnp.exp(s1 - v1[0][h:h + 1, :])
            s2 = s_scr[2 * h + 1, :, cs]
            e2_ref[h, :, cs] = jnp.exp(s2 - v2[0][h:h + 1, :]) * rz[h:h + 1, :]
        return carry

    lax.fori_loop(0, ncol, fin_body, 0)


def _peer_gate(hn, wq, keys, *, tm=256):
    t = hn.shape[0]
    out = jax.ShapeDtypeStruct((PEER_HEADS, PEER_NKEYS, t), F32)
    ospec = pl.BlockSpec((PEER_HEADS, PEER_NKEYS, tm), lambda ti: (0, 0, ti))
    return pl.pallas_call(
        _gate_kernel,
        grid=(t // tm,),
        in_specs=[
            pl.BlockSpec((tm, D_MODEL), lambda ti: (ti, 0)),
            pl.BlockSpec((D_MODEL, 2 * PEER_HEADS * PEER_HALF), lambda ti: (0, 0)),
            pl.BlockSpec((2 * PEER_HEADS, PEER_NKEYS, PEER_HALF), lambda ti: (0, 0, 0)),
        ],
        out_specs=[ospec, ospec, ospec, ospec],
        out_shape=[out, out, out, out],
        scratch_shapes=[pltpu.VMEM((2 * PEER_HEADS, PEER_NKEYS, tm), F32),
                        pltpu.VMEM((2, PEER_TOPK, PEER_HEADS, tm), F32)],
        compiler_params=pltpu.CompilerParams(
            dimension_semantics=("arbitrary",), vmem_limit_bytes=VMEM_LIMIT),
        name="peer_gate",
    )(hn, wq, keys)


def _gelu_tanh(x):
    c = np.float32(np.sqrt(2.0 / np.pi))
    return 0.5 * x * (1.0 + jnp.tanh(c * (x + np.float32(0.044715) * (x * x * x))))


def _peer_kernel(hn_ref, u_ref, vt_ref, n1_ref, e1_ref, rank2_ref, e2_ref, x_ref,
                 y_ref, acc_ref, a0_scr, a1_scr, g0_scr, g1_scr, r2_scr, e2_scr, *, n_i1):
    eb = pl.program_id(1)
    tm = hn_ref.shape[0]
    rows = PEER_NKEYS // BF16_ROWS

    @pl.when(eb == 0)
    def _():
        acc_ref[...] = jnp.zeros_like(acc_ref)
        a1_scr[...] = jnp.zeros_like(a1_scr)
        g1_scr[...] = jnp.zeros_like(g1_scr)
        r2_scr[...] = rank2_ref[...].astype(BF16)
        e2_scr[...] = e2_ref[...].astype(BF16)

    def step(a_prev, g_prev, a_cur, g_cur):
        half = tm // 2

        def scores(hf):
            return lax.dot_general(u_ref[...], hn_ref[hf * half:(hf + 1) * half, :],
                                   (((1,), (1,)), ((), ())), preferred_element_type=F32)

        def activate(c):
            cs = slice(c * LANES, (c + 1) * LANES)
            return (_gelu_tanh(a_prev[:, cs]) * g_prev[:, cs].astype(F32)).astype(BF16)

        def retrieve(p):
            return jnp.dot(vt_ref[0], p, preferred_element_type=F32)

        def build_gate(c):
            cs = slice(c * LANES, (c + 1) * LANES)
            for j0 in range(0, n_i1, GATE_GROUP):
                group = range(j0, j0 + GATE_GROUP)
                gs = {j: jnp.zeros((rows, BF16_ROWS, LANES), BF16) for j in group}
                for h in range(PEER_HEADS):
                    r2 = r2_scr[h, :, cs].reshape(rows, BF16_ROWS, LANES)
                    e2 = e2_scr[h, :, cs].reshape(rows, BF16_ROWS, LANES)
                    for j in group:
                        n1row = jnp.broadcast_to(n1_ref[h, 0, j:j + 1, cs], (BF16_ROWS, LANES)).astype(BF16)
                        e1row = jnp.broadcast_to(e1_ref[h, 0, j:j + 1, cs], (BF16_ROWS, LANES)).astype(BF16)
                        gs[j] = gs[j] + jnp.where(r2 < n1row[None], e2, jnp.zeros_like(e2)) * e1row[None]
                for j in group:
                    g_cur[j * PEER_NKEYS:(j + 1) * PEER_NKEYS, cs] = gs[j].reshape(PEER_NKEYS, LANES)

        ncol = half // LANES
        a_new0 = scores(0)
        p0 = jnp.concatenate([activate(c) for c in range(ncol)], axis=1)
        out0 = retrieve(p0)
        p1 = jnp.concatenate([activate(c) for c in range(ncol, 2 * ncol)], axis=1)
        build_gate(0)
        out1 = retrieve(p1)
        build_gate(1)
        build_gate(2)
        a_new1 = scores(1)
        build_gate(3)
        acc_ref[:, :half] += out0
        acc_ref[:, half:] += out1
        a_cur[:, :half] = a_new0
        a_cur[:, half:] = a_new1

    @pl.when(eb % 2 == 0)
    def _():
        step(a1_scr, g1_scr, a0_scr, g0_scr)

    @pl.when(eb % 2 == 1)
    def _():
        step(a0_scr, g0_scr, a1_scr, g1_scr)

    @pl.when(eb == pl.num_programs(1) - 1)
    def _():
        y_ref[...] = x_ref[...] + acc_ref[...].T


def _peer_dense(hn, u, v, n1t, e1t, rank2t, e2t, x1, *, tm=512, n_i1=8):
    t = hn.shape[0]
    te = n_i1 * PEER_NKEYS
    vt = v.reshape(PEER_EXPERTS // te, te, D_MODEL).transpose(0, 2, 1)
    nblk = PEER_NKEYS // n_i1
    n1r = n1t.reshape(PEER_HEADS, nblk, n_i1, t)
    e1r = e1t.reshape(PEER_HEADS, nblk, n_i1, t)
    grid = (t // tm, nblk + 1)
    this_blk = lambda eb: jnp.minimum(eb, nblk - 1)
    prev_blk = lambda eb: jnp.maximum(eb - 1, 0)
    return pl.pallas_call(
        functools.partial(_peer_kernel, n_i1=n_i1),
        grid=grid,
        in_specs=[
            pl.BlockSpec((tm, D_MODEL), lambda ti, eb: (ti, 0)),
            pl.BlockSpec((te, D_MODEL), lambda ti, eb: (this_blk(eb), 0)),
            pl.BlockSpec((1, D_MODEL, te), lambda ti, eb: (prev_blk(eb), 0, 0)),
            pl.BlockSpec((PEER_HEADS, 1, n_i1, tm), lambda ti, eb: (0, this_blk(eb), 0, ti)),
            pl.BlockSpec((PEER_HEADS, 1, n_i1, tm), lambda ti, eb: (0, this_blk(eb), 0, ti)),
            pl.BlockSpec((PEER_HEADS, PEER_NKEYS, tm), lambda ti, eb: (0, 0, ti)),
            pl.BlockSpec((PEER_HEADS, PEER_NKEYS, tm), lambda ti, eb: (0, 0, ti)),
            pl.BlockSpec((tm, D_MODEL), lambda ti, eb: (ti, 0)),
        ],
        out_specs=pl.BlockSpec((tm, D_MODEL), lambda ti, eb: (ti, 0)),
        out_shape=jax.ShapeDtypeStruct((t, D_MODEL), F32),
        scratch_shapes=[pltpu.VMEM((D_MODEL, tm), F32),
                        pltpu.VMEM((te, tm), F32), pltpu.VMEM((te, tm), F32),
                        pltpu.VMEM((te, tm), BF16), pltpu.VMEM((te, tm), BF16),
                        pltpu.VMEM((PEER_HEADS, PEER_NKEYS, tm), BF16),
                        pltpu.VMEM((PEER_HEADS, PEER_NKEYS, tm), BF16)],
        compiler_params=pltpu.CompilerParams(
            dimension_semantics=("arbitrary", "arbitrary"),
            vmem_limit_bytes=VMEM_LIMIT),
        name="peer_dense",
    )(hn, u, vt, n1r, e1r, rank2t, e2t, x1)


def _qkv(x, attn_norm_g, w_in, q_lat_g, w_uq, kv_lat_g, w_ukv, q_head_g, k_head_g):
    b, s, _ = x.shape
    h = _rmsnorm(x, attn_norm_g)
    p = h @ w_in
    o0, o1, o2 = Q_LORA, Q_LORA + KV_LORA, Q_LORA + KV_LORA + QK_ROPE
    c_q, c_kv, k_r, f = p[..., :o0], p[..., o0:o1], p[..., o1:o2], p[..., o2:]
    q = (_rmsnorm(c_q, q_lat_g) @ w_uq).reshape(b, s, N_HEADS, QK_HEAD)
    kv = (_rmsnorm(c_kv, kv_lat_g) @ w_ukv).reshape(b, s, N_HEADS, QK_NOPE + V_HEAD)
    k_nope, v = kv[..., :QK_NOPE], kv[..., QK_NOPE:]
    k = jnp.concatenate([k_nope, jnp.broadcast_to(k_r[:, :, None, :], (b, s, N_HEADS, QK_ROPE))], axis=-1)
    q = _rmsnorm(q, q_head_g) * (QK_HEAD ** -0.5 * LOG2_E)
    k = _rmsnorm(k, k_head_g)
    pos = jnp.arange(s, dtype=F32)
    q = jnp.concatenate([q[..., :QK_NOPE], _rope(q[..., QK_NOPE:], pos)], axis=-1)
    k = jnp.concatenate([k[..., :QK_NOPE], _rope(k[..., QK_NOPE:], pos)], axis=-1)
    pad = ((0, 0), (0, 0), (0, 0), (0, LANES - QK_HEAD))
    q = jnp.pad(q, pad).reshape(b, s, N_HEADS * LANES).astype(BF16)
    k = jnp.pad(k, pad).reshape(b, s, N_HEADS * LANES).astype(BF16).transpose(0, 2, 1)
    ones = jnp.ones((b, s, N_HEADS, 1), F32)
    v = jnp.concatenate([v, ones], axis=-1)
    v = jnp.pad(v, ((0, 0), (0, 0), (0, 0), (0, LANES - V_HEAD - 1)))
    v = v.reshape(b, s, N_HEADS * LANES).astype(BF16)
    return q, k, v, f


def _fnet(f):
    b, s, _ = f.shape
    z = f.reshape(b, s, FNET_GROUPS, FNET_CH)
    y = jnp.fft.fft2(z, axes=(1, 3), norm='ortho').real
    return y.reshape(b, s, FNET_W)


def _mixer(x, attn_norm_g, w_in, q_lat_g, w_uq, kv_lat_g, w_ukv, q_head_g, k_head_g,
           attn_out_g, fnet_out_g, w_out):
    q, k, v, f = _qkv(x, attn_norm_g, w_in, q_lat_g, w_uq, kv_lat_g, w_ukv, q_head_g, k_head_g)
    a = _flash_attention(q, k, v)
    fm = _fnet(f)
    mixed = jnp.concatenate([_rmsnorm(a, attn_out_g), _rmsnorm(fm, fnet_out_g)], axis=-1)
    return x + mixed @ w_out


def kernel(x_prompt, x_sample, attn_norm_g, w_in, q_lat_g, w_uq, kv_lat_g, w_ukv, q_head_g,
           k_head_g, attn_out_g, fnet_out_g, w_out, ffn_norm_g, peer_w_q, peer_sub_keys,
           peer_u, peer_v):
    l = 0
    mix = functools.partial(
        _mixer, attn_norm_g=attn_norm_g[l], w_in=w_in[l], q_lat_g=q_lat_g[l], w_uq=w_uq[l],
        kv_lat_g=kv_lat_g[l], w_ukv=w_ukv[l], q_head_g=q_head_g[l], k_head_g=k_head_g[l],
        attn_out_g=attn_out_g[l], fnet_out_g=fnet_out_g[l], w_out=w_out[l])
    x1p = mix(x_prompt)
    x1s = mix(x_sample)
    x1 = jnp.concatenate([x1p.reshape(-1, D_MODEL), x1s.reshape(-1, D_MODEL)], axis=0)
    hn = _rmsnorm(x1, ffn_norm_g[l]).astype(BF16)
    keys = peer_sub_keys[l].reshape(2 * PEER_HEADS, PEER_NKEYS, PEER_HALF).astype(BF16)
    rank2t, e2t, n1t, e1t = _peer_gate(hn, peer_w_q[l].astype(BF16), keys)
    u = peer_u[l].astype(BF16)
    y = _peer_dense(hn, u, peer_v[l].astype(BF16), n1t, e1t, rank2t, e2t, x1)
    np_ = x_prompt.shape[0] * x_prompt.shape[1]
    return (y[:np_].reshape(x_prompt.shape), y[np_:].reshape(x_sample.shape))
```

```python
import functools

import jax
import jax.numpy as jnp
import numpy as np
from jax import lax
from jax.experimental import pallas as pl
from jax.experimental.pallas import tpu as pltpu

D_MODEL = 1024
N_HEADS = 8
QK_NOPE = 64
QK_ROPE = 32
QK_HEAD = QK_NOPE + QK_ROPE
V_HEAD = 64
Q_LORA = 384
KV_LORA = 256
ATTN_W = N_HEADS * V_HEAD
ROPE_THETA = 10000.0
FNET_W = D_MODEL - ATTN_W
FNET_GROUPS = 4
FNET_CH = FNET_W // FNET_GROUPS
PEER_HEADS = 8
PEER_NKEYS = 128
PEER_EXPERTS = PEER_NKEYS * PEER_NKEYS
PEER_HALF = 128
PEER_TOPK = 16
EPS = 1e-6
LOG2_E = 1.4426950408889634

LANES = 128
BF16_ROWS = 16
GATE_GROUP = 4
VMEM_LIMIT = 48 * 1024 * 1024

F32 = jnp.float32
BF16 = jnp.bfloat16


def _flash_kernel(q_ref, kt_ref, v_ref, o_ref, m_scr, acc_scr, *, tk):
    seq = v_ref.shape[1]
    tq = q_ref.shape[1]
    nkv = seq // tk
    m_scr[...] = jnp.full(m_scr.shape, -jnp.inf, F32)
    acc_scr[...] = jnp.zeros(acc_scr.shape, F32)

    def body(j, carry):
        start = pl.multiple_of(j * tk, tk)
        heads = [slice(hh * LANES, (hh + 1) * LANES) for hh in range(2)]
        scores = [jnp.dot(q_ref[0, :, hs], kt_ref[0, hs, pl.ds(start, tk)],
                          preferred_element_type=F32) for hs in heads]
        for hh, hs in enumerate(heads):
            cols = [scores[hh][:, c * LANES:(c + 1) * LANES] for c in range(tk // LANES)]
            m_old = m_scr[hh]
            m_new = jnp.maximum(m_old, jnp.max(functools.reduce(jnp.maximum, cols),
                                               axis=1, keepdims=True))
            p = jnp.concatenate([jnp.exp2(c - m_new) for c in cols], axis=1).astype(BF16)
            pv = jnp.dot(p, v_ref[0, pl.ds(start, tk), hs], preferred_element_type=F32)
            acc_scr[hh] = jnp.exp2(m_old - m_new) * acc_scr[hh] + pv
            m_scr[hh] = m_new
        return carry

    lax.fori_loop(0, nkv, body, 0)
    outs = []
    for hh in range(2):
        acc = acc_scr[hh]
        outs.append(acc[:, :V_HEAD] / acc[:, V_HEAD:V_HEAD + 1])
    o_ref[0] = jnp.concatenate(outs, axis=1)


def _flash_attention(q, kt, v, *, tq=512, tk=512):
    b, s, _ = q.shape
    grid = (b, N_HEADS // 2, s // tq)
    return pl.pallas_call(
        functools.partial(_flash_kernel, tk=tk),
        grid=grid,
        in_specs=[
            pl.BlockSpec((1, tq, 2 * LANES), lambda bi, hp, qi: (bi, qi, hp)),
            pl.BlockSpec((1, 2 * LANES, s), lambda bi, hp, qi: (bi, hp, 0)),
            pl.BlockSpec((1, s, 2 * LANES), lambda bi, hp, qi: (bi, 0, hp)),
        ],
        out_specs=pl.BlockSpec((1, tq, LANES), lambda bi, hp, qi: (bi, qi, hp)),
        out_shape=jax.ShapeDtypeStruct((b, s, ATTN_W), F32),
        scratch_shapes=[pltpu.VMEM((2, tq, LANES), F32), pltpu.VMEM((2, tq, LANES), F32)],
        compiler_params=pltpu.CompilerParams(
            dimension_semantics=("arbitrary", "arbitrary", "arbitrary"),
            vmem_limit_bytes=VMEM_LIMIT),
        name="flash_attention",
    )(q, kt, v)


_CAND_PAIRS = tuple((a, b) for a in range(PEER_TOPK) for b in range(PEER_TOPK)
                    if (a + 1) * (b + 1) <= PEER_TOPK)
_NEG_INF = float("-inf")


def _tree_max(xs):
    xs = list(xs)
    while len(xs) > 1:
        nxt = [jnp.maximum(xs[i], xs[i + 1]) for i in range(0, len(xs) - 1, 2)]
        if len(xs) % 2:
            nxt.append(xs[-1])
        xs = nxt
    return xs[0]


def _gate_kernel(hn_ref, wq_ref, keys_ref, rank2_ref, e2_ref, n1_ref, e1_ref, s_scr, vals_scr):
    tm = hn_ref.shape[0]
    ncol = tm // LANES
    q = jnp.dot(hn_ref[...], wq_ref[...], preferred_element_type=F32).astype(BF16)
    for hc in range(2 * PEER_HEADS):
        s_scr[hc] = lax.dot_general(keys_ref[hc], q[:, hc * PEER_HALF:(hc + 1) * PEER_HALF],
                                    (((1,), (1,)), ((), ())), preferred_element_type=F32)

    for hc in range(2 * PEER_HEADS):
        h, c = divmod(hc, 2)

        def col_body(col, carry, hc=hc, h=h, c=c):
            cs = pl.ds(pl.multiple_of(col * LANES, LANES), LANES)
            cur = s_scr[hc, :, cs]
            rank = jnp.full((PEER_NKEYS, LANES), float(PEER_TOPK), F32)
            for r in range(PEER_TOPK):
                m = jnp.max(cur, axis=0, keepdims=True)
                hit = cur == m
                if c == 1:
                    rank = jnp.where(hit, float(r), rank)
                cur = jnp.where(hit, _NEG_INF, cur)
                vals_scr[c, r, h:h + 1, cs] = m
            if c == 1:
                rank2_ref[h, :, cs] = rank
            return carry

        lax.fori_loop(0, ncol, col_body, 0)

    def fin_body(col, carry):
        cs = pl.ds(pl.multiple_of(col * LANES, LANES), LANES)
        v1 = [vals_scr[0, a, :, cs] for a in range(PEER_TOPK)]
        v2 = [vals_scr[1, b, :, cs] for b in range(PEER_TOPK)]
        cands = [v1[a] + v2[b] for a, b in _CAND_PAIRS]
        top = cands[0]
        z = jnp.zeros_like(top)
        m = top
        for r in range(PEER_TOPK):
            m = _tree_max(cands)
            z = z + jnp.exp(m - top)
            if r + 1 < PEER_TOPK:
                cands = [jnp.where(cd == m, _NEG_INF, cd) for cd in cands]
        thr = m
        rz = 1.0 / z
        for h in range(PEER_HEADS):
            thr_h = thr[h:h + 1, :]
            s1 = s_scr[2 * h, :, cs]
            n1 = jnp.zeros((PEER_NKEYS, LANES), F32)
            for b in range(PEER_TOPK):
                n1 = n1 + jnp.where((s1 + v2[b][h:h + 1, :]) >= thr_h, 1.0, 0.0)
            n1_ref[h, :, cs] = n1
            e1_ref[h, :, cs] = jnp.exp(s1 - v1[0][h:h + 1, :])
            s2 = s_scr[2 * h + 1, :, cs]
            e2_ref[h, :, cs] = jnp.exp(s2 - v2[0][h:h + 1, :]) * rz[h:h + 1, :]
        return carry

    lax.fori_loop(0, ncol, fin_body, 0)


def _peer_gate(hn, wq, keys, *, tm=256):
    t = hn.shape[0]
    out = jax.ShapeDtypeStruct((PEER_HEADS, PEER_NKEYS, t), F32)
    ospec = pl.BlockSpec((PEER_HEADS, PEER_NKEYS, tm), lambda ti: (0, 0, ti))
    return pl.pallas_call(
        _gate_kernel,
        grid=(t // tm,),
        in_specs=[
            pl.BlockSpec((tm, D_MODEL), lambda ti: (ti, 0)),
            pl.BlockSpec((D_MODEL, 2 * PEER_HEADS * PEER_HALF), lambda ti: (0, 0)),
            pl.BlockSpec((2 * PEER_HEADS, PEER_NKEYS, PEER_HALF), lambda ti: (0, 0, 0)),
        ],
        out_specs=[ospec, ospec, ospec, ospec],
        out_shape=[out, out, out, out],
        scratch_shapes=[pltpu.VMEM((2 * PEER_HEADS, PEER_NKEYS, tm), F32),
                        pltpu.VMEM((2, PEER_TOPK, PEER_HEADS, tm), F32)],
        compiler_params=pltpu.CompilerParams(
            dimension_semantics=("arbitrary",), vmem_limit_bytes=VMEM_LIMIT),
        name="peer_gate",
    )(hn, wq, keys)


def _gelu_tanh(x):
    c = np.float32(np.sqrt(2.0 / np.pi))
    return 0.5 * x * (1.0 + jnp.tanh(c * (x + np.float32(0.044715) * (x * x * x))))


def _peer_kernel(hn_ref, u_ref, vt_ref, n1_ref, e1_ref, rank2_ref, e2_ref, x_ref,
                 y_ref, acc_ref, a0_scr, a1_scr, g0_scr, g1_scr, r2_scr, e2_scr, *, n_i1):
    eb = pl.program_id(1)
    tm = hn_ref.shape[0]
    rows = PEER_NKEYS // BF16_ROWS

    @pl.when(eb == 0)
    def _():
        acc_ref[...] = jnp.zeros_like(acc_ref)
        a1_scr[...] = jnp.zeros_like(a1_scr)
        g1_scr[...] = jnp.zeros_like(g1_scr)
        r2_scr[...] = rank2_ref[...].astype(BF16)
        e2_scr[...] = e2_ref[...].astype(BF16)

    def step(a_prev, g_prev, a_cur, g_cur):
        half = tm // 2

        def scores(hf):
            return lax.dot_general(u_ref[...], hn_ref[hf * half:(hf + 1) * half, :],
                                   (((1,), (1,)), ((), ())), preferred_element_type=F32)

        def activate(c):
            cs = slice(c * LANES, (c + 1) * LANES)
            return (_gelu_tanh(a_prev[:, cs]) * g_prev[:, cs].astype(F32)).astype(BF16)

        def retrieve(p):
            return jnp.dot(vt_ref[0], p, preferred_element_type=F32)

        def build_gate(c):
            cs = slice(c * LANES, (c + 1) * LANES)
            for j0 in range(0, n_i1, GATE_GROUP):
                group = range(j0, j0 + GATE_GROUP)
                gs = {j: jnp.zeros((rows, BF16_ROWS, LANES), BF16) for j in group}
                for h in range(PEER_HEADS):
                    r2 = r2_scr[h, :, cs].reshape(rows, BF16_ROWS, LANES)
                    e2 = e2_scr[h, :, cs].reshape(rows, BF16_ROWS, LANES)
                    for j in group:
                        n1row = jnp.broadcast_to(n1_ref[h, 0, j:j + 1, cs], (BF16_ROWS, LANES)).astype(BF16)
                        e1row = jnp.broadcast_to(e1_ref[h, 0, j:j + 1, cs], (BF16_ROWS, LANES)).astype(BF16)
                        gs[j] = gs[j] + jnp.where(r2 < n1row[None], e2, jnp.zeros_like(e2)) * e1row[None]
                for j in group:
                    g_cur[j * PEER_NKEYS:(j + 1) * PEER_NKEYS, cs] = gs[j].reshape(PEER_NKEYS, LANES)

        ncol = half // LANES
        a_new0 = scores(0)
        p0 = jnp.concatenate([activate(c) for c in range(ncol)], axis=1)
        out0 = retrieve(p0)
        p1 = jnp.concatenate([activate(c) for c in range(ncol, 2 * ncol)], axis=1)
        build_gate(0)
        out1 = retrieve(p1)
        build_gate(1)
        build_gate(2)
        a_new1 = scores(1)
        build_gate(3)
        acc_ref[:, :half] += out0
        acc_ref[:, half:] += out1
        a_cur[:, :half] = a_new0
        a_cur[:, half:] = a_new1

    @pl.when(eb % 2 == 0)
    def _():
        step(a1_scr, g1_scr, a0_scr, g0_scr)

    @pl.when(eb % 2 == 1)
    def _():
        step(a0_scr, g0_scr, a1_scr, g1_scr)

    @pl.when(eb == pl.num_programs(1) - 1)
    def _():
        y_ref[...] = x_ref[...] + acc_ref[...].T


def _peer_dense(hn, u, v, n1t, e1t, rank2t, e2t, x1, *, tm=512, n_i1=8):
    t = hn.shape[0]
    te = n_i1 * PEER_NKEYS
    vt = v.reshape(PEER_EXPERTS // te, te, D_MODEL).transpose(0, 2, 1)
    nblk = PEER_NKEYS // n_i1
    n1r = n1t.reshape(PEER_HEADS, nblk, n_i1, t)
    e1r = e1t.reshape(PEER_HEADS, nblk, n_i1, t)
    grid = (t // tm, nblk + 1)
    this_blk = lambda eb: jnp.minimum(eb, nblk - 1)
    prev_blk = lambda eb: jnp.maximum(eb - 1, 0)
    return pl.pallas_call(
        functools.partial(_peer_kernel, n_i1=n_i1),
        grid=grid,
        in_specs=[
            pl.BlockSpec((tm, D_MODEL), lambda ti, eb: (ti, 0)),
            pl.BlockSpec((te, D_MODEL), lambda ti, eb: (this_blk(eb), 0)),
            pl.BlockSpec((1, D_MODEL, te), lambda ti, eb: (prev_blk(eb), 0, 0)),
            pl.BlockSpec((PEER_HEADS, 1, n_i1, tm), lambda ti, eb: (0, this_blk(eb), 0, ti)),
            pl.BlockSpec((PEER_HEADS, 1, n_i1, tm), lambda ti, eb: (0, this_blk(eb), 0, ti)),
            pl.BlockSpec((PEER_HEADS, PEER_NKEYS, tm), lambda ti, eb: (0, 0, ti)),
            pl.BlockSpec((PEER_HEADS, PEER_NKEYS, tm), lambda ti, eb: (0, 0, ti)),
            pl.BlockSpec((tm, D_MODEL), lambda ti, eb: (ti, 0)),
        ],
        out_specs=pl.BlockSpec((tm, D_MODEL), lambda ti, eb: (ti, 0)),
        out_shape=jax.ShapeDtypeStruct((t, D_MODEL), F32),
        scratch_shapes=[pltpu.VMEM((D_MODEL, tm), F32),
                        pltpu.VMEM((te, tm), F32), pltpu.VMEM((te, tm), F32),
                        pltpu.VMEM((te, tm), BF16), pltpu.VMEM((te, tm), BF16),
                        pltpu.VMEM((PEER_HEADS, PEER_NKEYS, tm), BF16),
                        pltpu.VMEM((PEER_HEADS, PEER_NKEYS, tm), BF16)],
        compiler_params=pltpu.CompilerParams(
            dimension_semantics=("arbitrary", "arbitrary"),
            vmem_limit_bytes=VMEM_LIMIT),
        name="peer_dense",
    )(hn, u, vt, n1r, e1r, rank2t, e2t, x1)


def _row_rms(x, width):
    return x * lax.rsqrt(jnp.sum(x * x, axis=-1, keepdims=True) * (1.0 / width) + EPS)


def _proj_kernel(x_ref, g_in_ref, w_in_ref, g_q_ref, w_uq_ref, g_kv_ref, w_uk_ref, w_uv_ref,
                 g_qh_ref, g_kh_ref, cos_ref, sin_lo_ref, sin_hi_ref, v_one_ref,
                 q_ref, kt_ref, v_ref, f_ref):
    o0, o1, o2 = Q_LORA, Q_LORA + KV_LORA, Q_LORA + KV_LORA + LANES
    h = (_row_rms(x_ref[0], D_MODEL) * g_in_ref[...]).astype(BF16)
    p = jnp.dot(h, w_in_ref[...], preferred_element_type=F32)
    c_q = (_row_rms(p[:, :o0], Q_LORA) * g_q_ref[...]).astype(BF16)
    c_kv = (_row_rms(p[:, o0:o1], KV_LORA) * g_kv_ref[...]).astype(BF16)
    k_rope = p[:, o1:o2]
    f_ref[0] = p[:, o2:]
    q = jnp.dot(c_q, w_uq_ref[...], preferred_element_type=F32)
    k = jnp.dot(c_kv, w_uk_ref[...], preferred_element_type=F32)
    v = jnp.dot(c_kv, w_uv_ref[...], preferred_element_type=F32)
    v_ref[0] = (v + v_one_ref[...]).astype(BF16)

    cos, sin_lo, sin_hi = cos_ref[...], sin_lo_ref[...], sin_hi_ref[...]
    half = QK_ROPE // 2

    def head_norm_rope(slot, gain):
        y = _row_rms(slot, QK_HEAD) * gain
        return (y * cos + pltpu.roll(y, LANES - half, 1) * sin_lo
                + pltpu.roll(y, half, 1) * sin_hi)

    q_slots, k_slots = [], []
    for hd in range(N_HEADS):
        hs = slice(hd * LANES, (hd + 1) * LANES)
        q_slots.append(head_norm_rope(q[:, hs], g_qh_ref[...]))
        k_slots.append(head_norm_rope(k[:, hs] + k_rope, g_kh_ref[...]))
    q_ref[0] = jnp.concatenate(q_slots, axis=1).astype(BF16)
    kt_ref[0] = jnp.concatenate(k_slots, axis=1).T.astype(BF16)


def _slot_cols(w, width, n_in):
    w = w.reshape(n_in, N_HEADS, width)
    return jnp.pad(w, ((0, 0), (0, 0), (0, LANES - width))).reshape(n_in, N_HEADS * LANES)


def _project(x, attn_norm_g, w_in, q_lat_g, w_uq, kv_lat_g, w_ukv, q_head_g, k_head_g, *, tm=512):
    b, s, _ = x.shape
    o0, o1, o2 = Q_LORA, Q_LORA + KV_LORA, Q_LORA + KV_LORA + QK_ROPE
    k_rope_cols = jnp.pad(w_in[:, o1:o2], ((0, 0), (QK_NOPE, LANES - QK_HEAD)))
    w_in_p = jnp.concatenate([w_in[:, :o1], k_rope_cols, w_in[:, o2:]], axis=1).astype(BF16)
    w_uq_p = _slot_cols(w_uq, QK_HEAD, Q_LORA).astype(BF16)
    w_ukv_h = w_ukv.reshape(KV_LORA, N_HEADS, QK_NOPE + V_HEAD)
    w_uk_p = _slot_cols(w_ukv_h[:, :, :QK_NOPE].reshape(KV_LORA, -1), QK_NOPE, KV_LORA).astype(BF16)
    w_uv_p = _slot_cols(w_ukv_h[:, :, QK_NOPE:].reshape(KV_LORA, -1), V_HEAD, KV_LORA).astype(BF16)
    pad_gain = lambda g: jnp.pad(g, (0, LANES - QK_HEAD)).reshape(1, LANES)
    g_qh = pad_gain(q_head_g) * (QK_HEAD ** -0.5 * LOG2_E)
    g_kh = pad_gain(k_head_g)
    half = QK_ROPE // 2
    freqs = 1.0 / (ROPE_THETA ** (jnp.arange(half, dtype=F32) / half))
    ang = jnp.arange(s, dtype=F32)[:, None] * freqs[None, :]
    cos, sin = jnp.cos(ang), jnp.sin(ang)
    zeros = lambda n: jnp.zeros((s, n), F32)
    cos_t = jnp.concatenate([jnp.ones((s, QK_NOPE), F32), cos, cos, zeros(LANES - QK_HEAD)], axis=1)
    sin_lo = jnp.concatenate([zeros(QK_NOPE), -sin, zeros(LANES - QK_NOPE - half)], axis=1)
    sin_hi = jnp.concatenate([zeros(QK_NOPE + half), sin, zeros(LANES - QK_HEAD)], axis=1)
    v_one = jnp.tile((jnp.arange(LANES) == V_HEAD).astype(F32), N_HEADS).reshape(1, -1)
    row = lambda g: g.reshape(1, -1)

    const = lambda shape: pl.BlockSpec(shape, lambda bi, si: (0,) * len(shape))
    tab = pl.BlockSpec((tm, LANES), lambda bi, si: (si, 0))
    wide = N_HEADS * LANES
    return pl.pallas_call(
        _proj_kernel,
        grid=(b, s // tm),
        in_specs=[
            pl.BlockSpec((1, tm, D_MODEL), lambda bi, si: (bi, si, 0)),
            const((1, D_MODEL)), const(w_in_p.shape), const((1, Q_LORA)), const(w_uq_p.shape),
            const((1, KV_LORA)), const(w_uk_p.shape), const(w_uv_p.shape),
            const((1, LANES)), const((1, LANES)), tab, tab, tab, const((1, wide)),
        ],
        out_specs=[
            pl.BlockSpec((1, tm, wide), lambda bi, si: (bi, si, 0)),
            pl.BlockSpec((1, wide, tm), lambda bi, si: (bi, 0, si)),
            pl.BlockSpec((1, tm, wide), lambda bi, si: (bi, si, 0)),
            pl.BlockSpec((1, tm, FNET_W), lambda bi, si: (bi, si, 0)),
        ],
        out_shape=[
            jax.ShapeDtypeStruct((b, s, wide), BF16),
            jax.ShapeDtypeStruct((b, wide, s), BF16),
            jax.ShapeDtypeStruct((b, s, wide), BF16),
            jax.ShapeDtypeStruct((b, s, FNET_W), F32),
        ],
        compiler_params=pltpu.CompilerParams(
            dimension_semantics=("arbitrary", "arbitrary"), vmem_limit_bytes=VMEM_LIMIT),
        name="input_projection",
    )(x, row(attn_norm_g), w_in_p, row(q_lat_g), w_uq_p, row(kv_lat_g), w_uk_p, w_uv_p,
      g_qh, g_kh, cos_t, sin_lo, sin_hi, v_one)


def _mix_kernel(x_ref, a_ref, fm_ref, g_a_ref, g_f_ref, w_a_ref, w_f_ref, g_ffn_ref,
                x1_ref, hn_ref):
    a = (_row_rms(a_ref[...], ATTN_W) * g_a_ref[...]).astype(BF16)
    fm = (_row_rms(fm_ref[...], FNET_W) * g_f_ref[...]).astype(BF16)
    x1 = (x_ref[...] + jnp.dot(a, w_a_ref[...], preferred_element_type=F32)
          + jnp.dot(fm, w_f_ref[...], preferred_element_type=F32))
    x1_ref[...] = x1
    hn_ref[...] = (_row_rms(x1, D_MODEL) * g_ffn_ref[...]).astype(BF16)


def _mix(x, a, fm, attn_out_g, fnet_out_g, w_out, ffn_norm_g, *, tm=512):
    t = x.shape[0]
    row = lambda g: g.reshape(1, -1)
    tok = lambda w: pl.BlockSpec((tm, w), lambda ti: (ti, 0))
    const = lambda shape: pl.BlockSpec(shape, lambda ti: (0, 0))
    return pl.pallas_call(
        _mix_kernel,
        grid=(t // tm,),
        in_specs=[tok(D_MODEL), tok(ATTN_W), tok(FNET_W), const((1, ATTN_W)), const((1, FNET_W)),
                  const((ATTN_W, D_MODEL)), const((FNET_W, D_MODEL)), const((1, D_MODEL))],
        out_specs=[tok(D_MODEL), tok(D_MODEL)],
        out_shape=[jax.ShapeDtypeStruct((t, D_MODEL), F32), jax.ShapeDtypeStruct((t, D_MODEL), BF16)],
        compiler_params=pltpu.CompilerParams(
            dimension_semantics=("arbitrary",), vmem_limit_bytes=VMEM_LIMIT),
        name="output_mix",
    )(x, a, fm, row(attn_out_g), row(fnet_out_g), w_out[:ATTN_W].astype(BF16),
      w_out[ATTN_W:].astype(BF16), row(ffn_norm_g))


def _fnet(f):
    b, s, _ = f.shape
    z = f.reshape(b, s, FNET_GROUPS, FNET_CH)
    y = jnp.fft.fft2(z, axes=(1, 3), norm='ortho').real
    return y.reshape(b, s, FNET_W)


def kernel(x_prompt, x_sample, attn_norm_g, w_in, q_lat_g, w_uq, kv_lat_g, w_ukv, q_head_g,
           k_head_g, attn_out_g, fnet_out_g, w_out, ffn_norm_g, peer_w_q, peer_sub_keys,
           peer_u, peer_v):
    l = 0
    a_parts, fm_parts = [], []
    for x in (x_prompt, x_sample):
        q, kt, v, f = _project(x, attn_norm_g[l], w_in[l], q_lat_g[l], w_uq[l], kv_lat_g[l],
                               w_ukv[l], q_head_g[l], k_head_g[l])
        a_parts.append(_flash_attention(q, kt, v).reshape(-1, ATTN_W))
        fm_parts.append(_fnet(f).reshape(-1, FNET_W))
    x0 = jnp.concatenate([x_prompt.reshape(-1, D_MODEL), x_sample.reshape(-1, D_MODEL)], axis=0)
    x1, hn = _mix(x0, jnp.concatenate(a_parts, axis=0), jnp.concatenate(fm_parts, axis=0),
                  attn_out_g[l], fnet_out_g[l], w_out[l], ffn_norm_g[l])
    keys = peer_sub_keys[l].reshape(2 * PEER_HEADS, PEER_NKEYS, PEER_HALF).astype(BF16)
    rank2t, e2t, n1t, e1t = _peer_gate(hn, peer_w_q[l].astype(BF16), keys)
    y = _peer_dense(hn, peer_u[l].astype(BF16), peer_v[l].astype(BF16), n1t, e1t, rank2t, e2t, x1)
    np_ = x_prompt.shape[0] * x_prompt.shape[1]
    return (y[:np_].reshape(x_prompt.shape), y[np_:].reshape(x_sample.shape))
```

```python
import functools

import jax
import jax.numpy as jnp
import numpy as np
from jax import lax
from jax.experimental import pallas as pl
from jax.experimental.pallas import tpu as pltpu

D_MODEL = 1024
N_HEADS = 8
QK_NOPE = 64
QK_ROPE = 32
QK_HEAD = QK_NOPE + QK_ROPE
V_HEAD = 64
Q_LORA = 384
KV_LORA = 256
ATTN_W = N_HEADS * V_HEAD
ROPE_THETA = 10000.0
FNET_W = D_MODEL - ATTN_W
FNET_GROUPS = 4
FNET_CH = FNET_W // FNET_GROUPS
PEER_HEADS = 8
PEER_NKEYS = 128
PEER_EXPERTS = PEER_NKEYS * PEER_NKEYS
PEER_HALF = 128
PEER_TOPK = 16
EPS = 1e-6
LOG2_E = 1.4426950408889634

LANES = 128
BF16_ROWS = 16
GATE_GROUP = 4
VMEM_LIMIT = 48 * 1024 * 1024

F32 = jnp.float32
BF16 = jnp.bfloat16


def _flash_kernel(q_ref, kt_ref, v_ref, o_ref, m_scr, acc_scr, *, tk):
    seq = v_ref.shape[1]
    tq = q_ref.shape[1]
    nkv = seq // tk
    m_scr[...] = jnp.full(m_scr.shape, -jnp.inf, F32)
    acc_scr[...] = jnp.zeros(acc_scr.shape, F32)

    def body(j, carry):
        start = pl.multiple_of(j * tk, tk)
        heads = [slice(hh * LANES, (hh + 1) * LANES) for hh in range(2)]
        scores = [jnp.dot(q_ref[0, :, hs], kt_ref[0, hs, pl.ds(start, tk)],
                          preferred_element_type=F32) for hs in heads]
        for hh, hs in enumerate(heads):
            cols = [scores[hh][:, c * LANES:(c + 1) * LANES] for c in range(tk // LANES)]
            m_old = m_scr[hh]
            m_new = jnp.maximum(m_old, jnp.max(functools.reduce(jnp.maximum, cols),
                                               axis=1, keepdims=True))
            p = jnp.concatenate([jnp.exp2(c - m_new) for c in cols], axis=1).astype(BF16)
            pv = jnp.dot(p, v_ref[0, pl.ds(start, tk), hs], preferred_element_type=F32)
            acc_scr[hh] = jnp.exp2(m_old - m_new) * acc_scr[hh] + pv
            m_scr[hh] = m_new
        return carry

    lax.fori_loop(0, nkv, body, 0)
    outs = []
    for hh in range(2):
        acc = acc_scr[hh]
        outs.append(acc[:, :V_HEAD] / acc[:, V_HEAD:V_HEAD + 1])
    o_ref[0] = jnp.concatenate(outs, axis=1)


def _flash_attention(q, kt, v, *, tq=512, tk=512):
    b, s, _ = q.shape
    grid = (b, N_HEADS // 2, s // tq)
    return pl.pallas_call(
        functools.partial(_flash_kernel, tk=tk),
        grid=grid,
        in_specs=[
            pl.BlockSpec((1, tq, 2 * LANES), lambda bi, hp, qi: (bi, qi, hp)),
            pl.BlockSpec((1, 2 * LANES, s), lambda bi, hp, qi: (bi, hp, 0)),
            pl.BlockSpec((1, s, 2 * LANES), lambda bi, hp, qi: (bi, 0, hp)),
        ],
        out_specs=pl.BlockSpec((1, tq, LANES), lambda bi, hp, qi: (bi, qi, hp)),
        out_shape=jax.ShapeDtypeStruct((b, s, ATTN_W), F32),
        scratch_shapes=[pltpu.VMEM((2, tq, LANES), F32), pltpu.VMEM((2, tq, LANES), F32)],
        compiler_params=pltpu.CompilerParams(
            dimension_semantics=("arbitrary", "arbitrary", "arbitrary"),
            vmem_limit_bytes=VMEM_LIMIT),
        name="flash_attention",
    )(q, kt, v)


_CAND_PAIRS = tuple((a, b) for a in range(PEER_TOPK) for b in range(PEER_TOPK)
                    if (a + 1) * (b + 1) <= PEER_TOPK)
_NEG_INF = float("-inf")


def _tree_max(xs):
    xs = list(xs)
    while len(xs) > 1:
        nxt = [jnp.maximum(xs[i], xs[i + 1]) for i in range(0, len(xs) - 1, 2)]
        if len(xs) % 2:
            nxt.append(xs[-1])
        xs = nxt
    return xs[0]


def _gate_kernel(hn_ref, wq_ref, keys_ref, rank2_ref, e2_ref, n1_ref, e1_ref, s_scr, vals_scr):
    tm = hn_ref.shape[0]
    ncol = tm // LANES
    q = jnp.dot(hn_ref[...], wq_ref[...], preferred_element_type=F32).astype(BF16)
    for hc in range(2 * PEER_HEADS):
        s_scr[hc] = lax.dot_general(keys_ref[hc], q[:, hc * PEER_HALF:(hc + 1) * PEER_HALF],
                                    (((1,), (1,)), ((), ())), preferred_element_type=F32)

    for hc in range(2 * PEER_HEADS):
        h, c = divmod(hc, 2)

        def col_body(col, carry, hc=hc, h=h, c=c):
            cs = pl.ds(pl.multiple_of(col * LANES, LANES), LANES)
            cur = s_scr[hc, :, cs]
            rank = jnp.full((PEER_NKEYS, LANES), float(PEER_TOPK), F32)
            for r in range(PEER_TOPK):
                m = jnp.max(cur, axis=0, keepdims=True)
                hit = cur == m
                if c == 1:
                    rank = jnp.where(hit, float(r), rank)
                cur = jnp.where(hit, _NEG_INF, cur)
                vals_scr[c, r, h:h + 1, cs] = m
            if c == 1:
                rank2_ref[h, :, cs] = rank
            return carry

        lax.fori_loop(0, ncol, col_body, 0)

    def fin_body(col, carry):
        cs = pl.ds(pl.multiple_of(col * LANES, LANES), LANES)
        v1 = [vals_scr[0, a, :, cs] for a in range(PEER_TOPK)]
        v2 = [vals_scr[1, b, :, cs] for b in range(PEER_TOPK)]
        cands = [v1[a] + v2[b] for a, b in _CAND_PAIRS]
        top = cands[0]
        z = jnp.zeros_like(top)
        m = top
        for r in range(PEER_TOPK):
            m = _tree_max(cands)
            z = z + jnp.exp(m - top)
            if r + 1 < PEER_TOPK:
                cands = [jnp.where(cd == m, _NEG_INF, cd) for cd in cands]
        thr = m
        rz = 1.0 / z
        for h in range(PEER_HEADS):
            thr_h = thr[h:h + 1, :]
            s1 = s_scr[2 * h, :, cs]
            n1 = jnp.zeros((PEER_NKEYS, LANES), F32)
            for b in range(PEER_TOPK):
                n1 = n1 + jnp.where((s1 + v2[b][h:h + 1, :]) >= thr_h, 1.0, 0.0)
            n1_ref[h, :, cs] = n1
            e1_ref[h, :, cs] = jnp.exp(s1 - v1[0][h:h + 1, :])
            s2 = s_scr[2 * h + 1, :, cs]
            e2_ref[h, :, cs] = jnp.exp(s2 - v2[0][h:h + 1, :]) * rz[h:h + 1, :]
        return carry

    lax.fori_loop(0, ncol, fin_body, 0)


def _peer_gate(hn, wq, keys, *, tm=256):
    t = hn.shape[0]
    out = jax.ShapeDtypeStruct((PEER_HEADS, PEER_NKEYS, t), F32)
    ospec = pl.BlockSpec((PEER_HEADS, PEER_NKEYS, tm), lambda ti: (0, 0, ti))
    return pl.pallas_call(
        _gate_kernel,
        grid=(t // tm,),
        in_specs=[
            pl.BlockSpec((tm, D_MODEL), lambda ti: (ti, 0)),
            pl.BlockSpec((D_MODEL, 2 * PEER_HEADS * PEER_HALF), lambda ti: (0, 0)),
            pl.BlockSpec((2 * PEER_HEADS, PEER_NKEYS, PEER_HALF), lambda ti: (0, 0, 0)),
        ],
        out_specs=[ospec, ospec, ospec, ospec],
        out_shape=[out, out, out, out],
        scratch_shapes=[pltpu.VMEM((2 * PEER_HEADS, PEER_NKEYS, tm), F32),
                        pltpu.VMEM((2, PEER_TOPK, PEER_HEADS, tm), F32)],
        compiler_params=pltpu.CompilerParams(
            dimension_semantics=("arbitrary",), vmem_limit_bytes=VMEM_LIMIT),
        name="peer_gate",
    )(hn, wq, keys)


def _gelu_tanh(x):
    c = np.float32(np.sqrt(2.0 / np.pi))
    return 0.5 * x * (1.0 + jnp.tanh(c * (x + np.float32(0.044715) * (x * x * x))))


def _peer_kernel(hn_ref, u_ref, vt_ref, n1_ref, e1_ref, rank2_ref, e2_ref, x_ref,
                 y_ref, acc_ref, a0_scr, a1_scr, g0_scr, g1_scr, r2_scr, e2_scr, *, n_i1):
    eb = pl.program_id(1)
    tm = hn_ref.shape[0]
    rows = PEER_NKEYS // BF16_ROWS

    @pl.when(eb == 0)
    def _():
        acc_ref[...] = jnp.zeros_like(acc_ref)
        a1_scr[...] = jnp.zeros_like(a1_scr)
        g1_scr[...] = jnp.zeros_like(g1_scr)
        r2_scr[...] = rank2_ref[...].astype(BF16)
        e2_scr[...] = e2_ref[...].astype(BF16)

    def step(a_prev, g_prev, a_cur, g_cur):
        half = tm // 2

        def scores(hf):
            return lax.dot_general(u_ref[...], hn_ref[hf * half:(hf + 1) * half, :],
                                   (((1,), (1,)), ((), ())), preferred_element_type=F32)

        def activate(c):
            cs = slice(c * LANES, (c + 1) * LANES)
            return (_gelu_tanh(a_prev[:, cs]) * g_prev[:, cs].astype(F32)).astype(BF16)

        def retrieve(p):
            return jnp.dot(vt_ref[0], p, preferred_element_type=F32)

        def build_gate(c):
            cs = slice(c * LANES, (c + 1) * LANES)
            for j0 in range(0, n_i1, GATE_GROUP):
                group = range(j0, j0 + GATE_GROUP)
                gs = {j: jnp.zeros((rows, BF16_ROWS, LANES), BF16) for j in group}
                for h in range(PEER_HEADS):
                    r2 = r2_scr[h, :, cs].reshape(rows, BF16_ROWS, LANES)
                    e2 = e2_scr[h, :, cs].reshape(rows, BF16_ROWS, LANES)
                    for j in group:
                        n1row = jnp.broadcast_to(n1_ref[h, 0, j:j + 1, cs], (BF16_ROWS, LANES)).astype(BF16)
                        e1row = jnp.broadcast_to(e1_ref[h, 0, j:j + 1, cs], (BF16_ROWS, LANES)).astype(BF16)
                        gs[j] = gs[j] + jnp.where(r2 < n1row[None], e2, jnp.zeros_like(e2)) * e1row[None]
                for j in group:
                    g_cur[j * PEER_NKEYS:(j + 1) * PEER_NKEYS, cs] = gs[j].reshape(PEER_NKEYS, LANES)

        ncol = half // LANES
        a_new0 = scores(0)
        p0 = jnp.concatenate([activate(c) for c in range(ncol)], axis=1)
        out0 = retrieve(p0)
        p1 = jnp.concatenate([activate(c) for c in range(ncol, 2 * ncol)], axis=1)
        build_gate(0)
        out1 = retrieve(p1)
        build_gate(1)
        build_gate(2)
        a_new1 = scores(1)
        build_gate(3)
        acc_ref[:, :half] += out0
        acc_ref[:, half:] += out1
        a_cur[:, :half] = a_new0
        a_cur[:, half:] = a_new1

    @pl.when(eb % 2 == 0)
    def _():
        step(a1_scr, g1_scr, a0_scr, g0_scr)

    @pl.when(eb % 2 == 1)
    def _():
        step(a0_scr, g0_scr, a1_scr, g1_scr)

    @pl.when(eb == pl.num_programs(1) - 1)
    def _():
        y_ref[...] = x_ref[...] + acc_ref[...].T


def _peer_dense(hn, u, v, n1t, e1t, rank2t, e2t, x1, *, tm=512, n_i1=8):
    t = hn.shape[0]
    te = n_i1 * PEER_NKEYS
    vt = v.reshape(PEER_EXPERTS // te, te, D_MODEL).transpose(0, 2, 1)
    nblk = PEER_NKEYS // n_i1
    n1r = n1t.reshape(PEER_HEADS, nblk, n_i1, t)
    e1r = e1t.reshape(PEER_HEADS, nblk, n_i1, t)
    grid = (t // tm, nblk + 1)
    this_blk = lambda eb: jnp.minimum(eb, nblk - 1)
    prev_blk = lambda eb: jnp.maximum(eb - 1, 0)
    return pl.pallas_call(
        functools.partial(_peer_kernel, n_i1=n_i1),
        grid=grid,
        in_specs=[
            pl.BlockSpec((tm, D_MODEL), lambda ti, eb: (ti, 0)),
            pl.BlockSpec((te, D_MODEL), lambda ti, eb: (this_blk(eb), 0)),
            pl.BlockSpec((1, D_MODEL, te), lambda ti, eb: (prev_blk(eb), 0, 0)),
            pl.BlockSpec((PEER_HEADS, 1, n_i1, tm), lambda ti, eb: (0, this_blk(eb), 0, ti)),
            pl.BlockSpec((PEER_HEADS, 1, n_i1, tm), lambda ti, eb: (0, this_blk(eb), 0, ti)),
            pl.BlockSpec((PEER_HEADS, PEER_NKEYS, tm), lambda ti, eb: (0, 0, ti)),
            pl.BlockSpec((PEER_HEADS, PEER_NKEYS, tm), lambda ti, eb: (0, 0, ti)),
            pl.BlockSpec((tm, D_MODEL), lambda ti, eb: (ti, 0)),
        ],
        out_specs=pl.BlockSpec((tm, D_MODEL), lambda ti, eb: (ti, 0)),
        out_shape=jax.ShapeDtypeStruct((t, D_MODEL), F32),
        scratch_shapes=[pltpu.VMEM((D_MODEL, tm), F32),
                        pltpu.VMEM((te, tm), F32), pltpu.VMEM((te, tm), F32),
                        pltpu.VMEM((te, tm), BF16), pltpu.VMEM((te, tm), BF16),
                        pltpu.VMEM((PEER_HEADS, PEER_NKEYS, tm), BF16),
                        pltpu.VMEM((PEER_HEADS, PEER_NKEYS, tm), BF16)],
        compiler_params=pltpu.CompilerParams(
            dimension_semantics=("arbitrary", "arbitrary"),
            vmem_limit_bytes=VMEM_LIMIT),
        name="peer_dense",
    )(hn, u, vt, n1r, e1r, rank2t, e2t, x1)


def _row_rms(x, width):
    return x * lax.rsqrt(jnp.sum(x * x, axis=-1, keepdims=True) * (1.0 / width) + EPS)


def _proj_kernel(x_ref, g_in_ref, w_in_ref, g_q_ref, w_uq_ref, g_kv_ref, w_uk_ref, w_uv_ref,
                 g_qh_ref, g_kh_ref, cos_ref, sin_lo_ref, sin_hi_ref, v_one_ref, dft_c_ref,
                 q_ref, kt_ref, v_ref, f_ref):
    o0, o1, o2 = Q_LORA, Q_LORA + KV_LORA, Q_LORA + KV_LORA + LANES
    h = (_row_rms(x_ref[0], D_MODEL) * g_in_ref[...]).astype(BF16)
    p = jnp.dot(h, w_in_ref[...], preferred_element_type=F32)
    c_q = (_row_rms(p[:, :o0], Q_LORA) * g_q_ref[...]).astype(BF16)
    c_kv = (_row_rms(p[:, o0:o1], KV_LORA) * g_kv_ref[...]).astype(BF16)
    k_rope = p[:, o1:o2]
    f_ref[0] = jnp.dot(p[:, o2:].astype(BF16), dft_c_ref[...],
                       preferred_element_type=F32).astype(BF16)
    q = jnp.dot(c_q, w_uq_ref[...], preferred_element_type=F32)
    k = jnp.dot(c_kv, w_uk_ref[...], preferred_element_type=F32)
    v = jnp.dot(c_kv, w_uv_ref[...], preferred_element_type=F32)
    v_ref[0] = (v + v_one_ref[...]).astype(BF16)

    cos, sin_lo, sin_hi = cos_ref[...], sin_lo_ref[...], sin_hi_ref[...]
    half = QK_ROPE // 2

    def head_norm_rope(slot, gain):
        y = _row_rms(slot, QK_HEAD) * gain
        return (y * cos + pltpu.roll(y, LANES - half, 1) * sin_lo
                + pltpu.roll(y, half, 1) * sin_hi)

    q_slots, k_slots = [], []
    for hd in range(N_HEADS):
        hs = slice(hd * LANES, (hd + 1) * LANES)
        q_slots.append(head_norm_rope(q[:, hs], g_qh_ref[...]))
        k_slots.append(head_norm_rope(k[:, hs] + k_rope, g_kh_ref[...]))
    q_ref[0] = jnp.concatenate(q_slots, axis=1).astype(BF16)
    kt_ref[0] = jnp.concatenate(k_slots, axis=1).T.astype(BF16)


def _slot_cols(w, width, n_in):
    w = w.reshape(n_in, N_HEADS, width)
    return jnp.pad(w, ((0, 0), (0, 0), (0, LANES - width))).reshape(n_in, N_HEADS * LANES)


def _project(x, attn_norm_g, w_in, q_lat_g, w_uq, kv_lat_g, w_ukv, q_head_g, k_head_g, *, tm=512):
    b, s, _ = x.shape
    o0, o1, o2 = Q_LORA, Q_LORA + KV_LORA, Q_LORA + KV_LORA + QK_ROPE
    k_rope_cols = jnp.pad(w_in[:, o1:o2], ((0, 0), (QK_NOPE, LANES - QK_HEAD)))
    w_in_p = jnp.concatenate([w_in[:, :o1], k_rope_cols, w_in[:, o2:]], axis=1).astype(BF16)
    w_uq_p = _slot_cols(w_uq, QK_HEAD, Q_LORA).astype(BF16)
    w_ukv_h = w_ukv.reshape(KV_LORA, N_HEADS, QK_NOPE + V_HEAD)
    w_uk_p = _slot_cols(w_ukv_h[:, :, :QK_NOPE].reshape(KV_LORA, -1), QK_NOPE, KV_LORA).astype(BF16)
    w_uv_p = _slot_cols(w_ukv_h[:, :, QK_NOPE:].reshape(KV_LORA, -1), V_HEAD, KV_LORA).astype(BF16)
    pad_gain = lambda g: jnp.pad(g, (0, LANES - QK_HEAD)).reshape(1, LANES)
    g_qh = pad_gain(q_head_g) * (QK_HEAD ** -0.5 * LOG2_E)
    g_kh = pad_gain(k_head_g)
    half = QK_ROPE // 2
    freqs = 1.0 / (ROPE_THETA ** (jnp.arange(half, dtype=F32) / half))
    ang = jnp.arange(s, dtype=F32)[:, None] * freqs[None, :]
    cos, sin = jnp.cos(ang), jnp.sin(ang)
    zeros = lambda n: jnp.zeros((s, n), F32)
    cos_t = jnp.concatenate([jnp.ones((s, QK_NOPE), F32), cos, cos, zeros(LANES - QK_HEAD)], axis=1)
    sin_lo = jnp.concatenate([zeros(QK_NOPE), -sin, zeros(LANES - QK_NOPE - half)], axis=1)
    sin_hi = jnp.concatenate([zeros(QK_NOPE + half), sin, zeros(LANES - QK_HEAD)], axis=1)
    v_one = jnp.tile((jnp.arange(LANES) == V_HEAD).astype(F32), N_HEADS).reshape(1, -1)
    row = lambda g: g.reshape(1, -1)

    const = lambda shape: pl.BlockSpec(shape, lambda bi, si: (0,) * len(shape))
    tab = pl.BlockSpec((tm, LANES), lambda bi, si: (si, 0))
    wide = N_HEADS * LANES
    return pl.pallas_call(
        _proj_kernel,
        grid=(b, s // tm),
        in_specs=[
            pl.BlockSpec((1, tm, D_MODEL), lambda bi, si: (bi, si, 0)),
            const((1, D_MODEL)), const(w_in_p.shape), const((1, Q_LORA)), const(w_uq_p.shape),
            const((1, KV_LORA)), const(w_uk_p.shape), const(w_uv_p.shape),
            const((1, LANES)), const((1, LANES)), tab, tab, tab, const((1, wide)),
            const((FNET_W, 2 * FNET_W)),
        ],
        out_specs=[
            pl.BlockSpec((1, tm, wide), lambda bi, si: (bi, si, 0)),
            pl.BlockSpec((1, wide, tm), lambda bi, si: (bi, 0, si)),
            pl.BlockSpec((1, tm, wide), lambda bi, si: (bi, si, 0)),
            pl.BlockSpec((1, tm, 2 * FNET_W), lambda bi, si: (bi, si, 0)),
        ],
        out_shape=[
            jax.ShapeDtypeStruct((b, s, wide), BF16),
            jax.ShapeDtypeStruct((b, wide, s), BF16),
            jax.ShapeDtypeStruct((b, s, wide), BF16),
            jax.ShapeDtypeStruct((b, s, 2 * FNET_W), BF16),
        ],
        compiler_params=pltpu.CompilerParams(
            dimension_semantics=("arbitrary", "arbitrary"), vmem_limit_bytes=VMEM_LIMIT),
        name="input_projection",
    )(x, row(attn_norm_g), w_in_p, row(q_lat_g), w_uq_p, row(kv_lat_g), w_uk_p, w_uv_p,
      g_qh, g_kh, cos_t, sin_lo, sin_hi, v_one, _channel_dft(s))


def _mix_kernel(x_ref, a_ref, fm_ref, g_a_ref, g_f_ref, w_a_ref, w_f_ref, g_ffn_ref,
                x1_ref, hn_ref):
    a = (_row_rms(a_ref[...], ATTN_W) * g_a_ref[...]).astype(BF16)
    fm = (_row_rms(fm_ref[...], FNET_W) * g_f_ref[...]).astype(BF16)
    x1 = (x_ref[...] + jnp.dot(a, w_a_ref[...], preferred_element_type=F32)
          + jnp.dot(fm, w_f_ref[...], preferred_element_type=F32))
    x1_ref[...] = x1
    hn_ref[...] = (_row_rms(x1, D_MODEL) * g_ffn_ref[...]).astype(BF16)


def _mix(x, a, fm, attn_out_g, fnet_out_g, w_out, ffn_norm_g, *, tm=512):
    t = x.shape[0]
    row = lambda g: g.reshape(1, -1)
    tok = lambda w: pl.BlockSpec((tm, w), lambda ti: (ti, 0))
    const = lambda shape: pl.BlockSpec(shape, lambda ti: (0, 0))
    return pl.pallas_call(
        _mix_kernel,
        grid=(t // tm,),
        in_specs=[tok(D_MODEL), tok(ATTN_W), tok(FNET_W), const((1, ATTN_W)), const((1, FNET_W)),
                  const((ATTN_W, D_MODEL)), const((FNET_W, D_MODEL)), const((1, D_MODEL))],
        out_specs=[tok(D_MODEL), tok(D_MODEL)],
        out_shape=[jax.ShapeDtypeStruct((t, D_MODEL), F32), jax.ShapeDtypeStruct((t, D_MODEL), BF16)],
        compiler_params=pltpu.CompilerParams(
            dimension_semantics=("arbitrary",), vmem_limit_bytes=VMEM_LIMIT),
        name="output_mix",
    )(x, a, fm, row(attn_out_g), row(fnet_out_g), w_out[:ATTN_W].astype(BF16),
      w_out[ATTN_W:].astype(BF16), row(ffn_norm_g))


FFT_S1 = 64


def _dft(n):
    ang = -2.0 * np.pi * np.outer(np.arange(n), np.arange(n)) / n
    return np.cos(ang), np.sin(ang)


def _channel_dft(s):
    c, si = _dft(FNET_CH)
    eye = np.eye(FNET_GROUPS)
    scale = (FNET_CH * s) ** -0.5
    return jnp.asarray(np.concatenate([np.kron(eye, c), np.kron(eye, si)], axis=1) * scale, BF16)


def _fft_stage1_kernel(x_ref, dr_ref, di_ref, twr_ref, twi_ref, o_ref):
    x = x_ref[0]
    yr = jnp.dot(dr_ref[...], x, preferred_element_type=F32)
    yi = jnp.dot(di_ref[...], x, preferred_element_type=F32)
    w = FNET_W
    for j in range(x.shape[1] // (2 * w)):
        re, im = slice(2 * j * w, (2 * j + 1) * w), slice((2 * j + 1) * w, (2 * j + 2) * w)
        ar = yr[:, re] - yi[:, im]
        ai = yr[:, im] + yi[:, re]
        tw = slice(j * LANES, (j + 1) * LANES)
        twr = jnp.concatenate([twr_ref[:, tw]] * (w // LANES), axis=1)
        twi = jnp.concatenate([twi_ref[:, tw]] * (w // LANES), axis=1)
        o_ref[0, :, re] = (ar * twr - ai * twi).astype(BF16)
        o_ref[0, :, im] = (ar * twi + ai * twr).astype(BF16)


def _fft_stage2_kernel(a_ref, m_ref, o_ref):
    w = FNET_W
    for j in range(a_ref.shape[1]):
        xa = a_ref[0, j]
        stacked = jnp.concatenate([xa[:, :w], xa[:, w:]], axis=0)
        o_ref[0, j] = jnp.dot(m_ref[...], stacked, preferred_element_type=F32)


def _fnet(fx, *, tn2=8, kb=8):
    b, s, wide = fx.shape
    s1, s2 = FFT_S1, s // FFT_S1
    d1r, d1i = _dft(s1)
    d2r, d2i = _dft(s2)
    ang = -2.0 * np.pi / s * jnp.outer(jnp.arange(s1, dtype=F32), jnp.arange(s2, dtype=F32))
    twr = jnp.repeat(jnp.cos(ang), LANES, axis=1)
    twi = jnp.repeat(jnp.sin(ang), LANES, axis=1)
    cols = tn2 * wide
    x2 = fx.reshape(b, s1, s2 * wide)
    a = pl.pallas_call(
        _fft_stage1_kernel,
        grid=(b, s2 // tn2),
        in_specs=[
            pl.BlockSpec((1, s1, cols), lambda bi, ci: (bi, 0, ci)),
            pl.BlockSpec((s1, s1), lambda bi, ci: (0, 0)),
            pl.BlockSpec((s1, s1), lambda bi, ci: (0, 0)),
            pl.BlockSpec((s1, tn2 * LANES), lambda bi, ci: (0, ci)),
            pl.BlockSpec((s1, tn2 * LANES), lambda bi, ci: (0, ci)),
        ],
        out_specs=pl.BlockSpec((1, s1, cols), lambda bi, ci: (bi, 0, ci)),
        out_shape=jax.ShapeDtypeStruct((b, s1, s2 * wide), BF16),
        compiler_params=pltpu.CompilerParams(
            dimension_semantics=("arbitrary", "arbitrary"), vmem_limit_bytes=VMEM_LIMIT),
        name="fft_stage1",
    )(x2, jnp.asarray(d1r, BF16), jnp.asarray(d1i, BF16), twr, twi)
    a4 = a.reshape(b, s1, s2, wide)
    m2 = jnp.asarray(np.concatenate([d2r, -d2i], axis=1), BF16)
    y = pl.pallas_call(
        _fft_stage2_kernel,
        grid=(b, s1 // kb),
        in_specs=[
            pl.BlockSpec((1, kb, s2, wide), lambda bi, ki: (bi, ki, 0, 0)),
            pl.BlockSpec((s2, 2 * s2), lambda bi, ki: (0, 0)),
        ],
        out_specs=pl.BlockSpec((1, kb, s2, FNET_W), lambda bi, ki: (bi, ki, 0, 0)),
        out_shape=jax.ShapeDtypeStruct((b, s1, s2, FNET_W), F32),
        compiler_params=pltpu.CompilerParams(
            dimension_semantics=("arbitrary", "arbitrary"), vmem_limit_bytes=VMEM_LIMIT),
        name="fft_stage2",
    )(a4, m2)
    return y.transpose(0, 2, 1, 3).reshape(b, s, FNET_W)


def kernel(x_prompt, x_sample, attn_norm_g, w_in, q_lat_g, w_uq, kv_lat_g, w_ukv, q_head_g,
           k_head_g, attn_out_g, fnet_out_g, w_out, ffn_norm_g, peer_w_q, peer_sub_keys,
           peer_u, peer_v):
    l = 0
    a_parts, fm_parts = [], []
    for x in (x_prompt, x_sample):
        q, kt, v, f = _project(x, attn_norm_g[l], w_in[l], q_lat_g[l], w_uq[l], kv_lat_g[l],
                               w_ukv[l], q_head_g[l], k_head_g[l])
        a_parts.append(_flash_attention(q, kt, v).reshape(-1, ATTN_W))
        fm_parts.append(_fnet(f).reshape(-1, FNET_W))
    x0 = jnp.concatenate([x_prompt.reshape(-1, D_MODEL), x_sample.reshape(-1, D_MODEL)], axis=0)
    x1, hn = _mix(x0, jnp.concatenate(a_parts, axis=0), jnp.concatenate(fm_parts, axis=0),
                  attn_out_g[l], fnet_out_g[l], w_out[l], ffn_norm_g[l])
    keys = peer_sub_keys[l].reshape(2 * PEER_HEADS, PEER_NKEYS, PEER_HALF).astype(BF16)
    rank2t, e2t, n1t, e1t = _peer_gate(hn, peer_w_q[l].astype(BF16), keys)
    y = _peer_dense(hn, peer_u[l].astype(BF16), peer_v[l].astype(BF16), n1t, e1t, rank2t, e2t, x1)
    np_ = x_prompt.shape[0] * x_prompt.shape[1]
    return (y[:np_].reshape(x_prompt.shape), y[np_:].reshape(x_sample.shape))
```

```python
import functools

import jax
import jax.numpy as jnp
import numpy as np
from jax import lax
from jax.experimental import pallas as pl
from jax.experimental.pallas import tpu as pltpu

D_MODEL = 1024
N_HEADS = 8
QK_NOPE = 64
QK_ROPE = 32
QK_HEAD = QK_NOPE + QK_ROPE
V_HEAD = 64
Q_LORA = 384
KV_LORA = 256
ATTN_W = N_HEADS * V_HEAD
ROPE_THETA = 10000.0
FNET_W = D_MODEL - ATTN_W
FNET_GROUPS = 4
FNET_CH = FNET_W // FNET_GROUPS
PEER_HEADS = 8
PEER_NKEYS = 128
PEER_EXPERTS = PEER_NKEYS * PEER_NKEYS
PEER_HALF = 128
PEER_TOPK = 16
EPS = 1e-6
LOG2_E = 1.4426950408889634

LANES = 128
BF16_ROWS = 16
GATE_GROUP = 4
VMEM_LIMIT = 48 * 1024 * 1024

F32 = jnp.float32
BF16 = jnp.bfloat16


def _flash_kernel(q_ref, kt_ref, v_ref, o_ref, m_scr, acc_scr, *, tk):
    seq = v_ref.shape[1]
    tq = q_ref.shape[1]
    nkv = seq // tk
    m_scr[...] = jnp.full(m_scr.shape, -jnp.inf, F32)
    acc_scr[...] = jnp.zeros(acc_scr.shape, F32)

    def body(j, carry):
        start = pl.multiple_of(j * tk, tk)
        heads = [slice(hh * LANES, (hh + 1) * LANES) for hh in range(2)]
        scores = [jnp.dot(q_ref[0, :, hs], kt_ref[0, hs, pl.ds(start, tk)],
                          preferred_element_type=F32) for hs in heads]
        for hh, hs in enumerate(heads):
            cols = [scores[hh][:, c * LANES:(c + 1) * LANES] for c in range(tk // LANES)]
            m_old = m_scr[hh]
            m_new = jnp.maximum(m_old, jnp.max(functools.reduce(jnp.maximum, cols),
                                               axis=1, keepdims=True))
            p = jnp.concatenate([jnp.exp2(c - m_new) for c in cols], axis=1).astype(BF16)
            pv = jnp.dot(p, v_ref[0, pl.ds(start, tk), hs], preferred_element_type=F32)
            acc_scr[hh] = jnp.exp2(m_old - m_new) * acc_scr[hh] + pv
            m_scr[hh] = m_new
        return carry

    lax.fori_loop(0, nkv, body, 0)
    outs = []
    for hh in range(2):
        acc = acc_scr[hh]
        outs.append(acc[:, :V_HEAD] / acc[:, V_HEAD:V_HEAD + 1])
    o_ref[0] = jnp.concatenate(outs, axis=1)


def _flash_attention(q, kt, v, *, tq=512, tk=512):
    b, s, _ = q.shape
    grid = (b, N_HEADS // 2, s // tq)
    return pl.pallas_call(
        functools.partial(_flash_kernel, tk=tk),
        grid=grid,
        in_specs=[
            pl.BlockSpec((1, tq, 2 * LANES), lambda bi, hp, qi: (bi, qi, hp)),
            pl.BlockSpec((1, 2 * LANES, s), lambda bi, hp, qi: (bi, hp, 0)),
            pl.BlockSpec((1, s, 2 * LANES), lambda bi, hp, qi: (bi, 0, hp)),
        ],
        out_specs=pl.BlockSpec((1, tq, LANES), lambda bi, hp, qi: (bi, qi, hp)),
        out_shape=jax.ShapeDtypeStruct((b, s, ATTN_W), F32),
        scratch_shapes=[pltpu.VMEM((2, tq, LANES), F32), pltpu.VMEM((2, tq, LANES), F32)],
        compiler_params=pltpu.CompilerParams(
            dimension_semantics=("arbitrary", "arbitrary", "arbitrary"),
            vmem_limit_bytes=VMEM_LIMIT),
        name="flash_attention",
    )(q, kt, v)


_CAND_PAIRS = tuple((a, b) for a in range(PEER_TOPK) for b in range(PEER_TOPK)
                    if (a + 1) * (b + 1) <= PEER_TOPK)
_NEG_INF = float("-inf")


def _tree_max(xs):
    xs = list(xs)
    while len(xs) > 1:
        nxt = [jnp.maximum(xs[i], xs[i + 1]) for i in range(0, len(xs) - 1, 2)]
        if len(xs) % 2:
            nxt.append(xs[-1])
        xs = nxt
    return xs[0]


def _gate_kernel(hn_ref, wq_ref, keys_ref, rank2_ref, e2_ref, n1_ref, e1_ref, s_scr, vals_scr):
    tm = hn_ref.shape[0]
    ncol = tm // LANES
    q = jnp.dot(hn_ref[...], wq_ref[...], preferred_element_type=F32).astype(BF16)
    for hc in range(2 * PEER_HEADS):
        s_scr[hc] = lax.dot_general(keys_ref[hc], q[:, hc * PEER_HALF:(hc + 1) * PEER_HALF],
                                    (((1,), (1,)), ((), ())), preferred_element_type=F32)

    for hc in range(2 * PEER_HEADS):
        h, c = divmod(hc, 2)

        def col_body(col, carry, hc=hc, h=h, c=c):
            cs = pl.ds(pl.multiple_of(col * LANES, LANES), LANES)
            cur = s_scr[hc, :, cs]
            rank = jnp.full((PEER_NKEYS, LANES), float(PEER_TOPK), F32)
            for r in range(PEER_TOPK):
                m = jnp.max(cur, axis=0, keepdims=True)
                hit = cur == m
                if c == 1:
                    rank = jnp.where(hit, float(r), rank)
                cur = jnp.where(hit, _NEG_INF, cur)
                vals_scr[c, r, h:h + 1, cs] = m
            if c == 1:
                rank2_ref[h, :, cs] = rank
            return carry

        lax.fori_loop(0, ncol, col_body, 0)

    def fin_body(col, carry):
        cs = pl.ds(pl.multiple_of(col * LANES, LANES), LANES)
        v1 = [vals_scr[0, a, :, cs] for a in range(PEER_TOPK)]
        v2 = [vals_scr[1, b, :, cs] for b in range(PEER_TOPK)]
        cands = [v1[a] + v2[b] for a, b in _CAND_PAIRS]
        top = cands[0]
        z = jnp.zeros_like(top)
        m = top
        for r in range(PEER_TOPK):
            m = _tree_max(cands)
            z = z + jnp.exp(m - top)
            if r + 1 < PEER_TOPK:
                cands = [jnp.where(cd == m, _NEG_INF, cd) for cd in cands]
        thr = m
        rz = 1.0 / z
        for h in range(PEER_HEADS):
            thr_h = thr[h:h + 1, :]
            s1 = s_scr[2 * h, :, cs]
            n1 = jnp.zeros((PEER_NKEYS, LANES), F32)
            for b in range(PEER_TOPK):
                n1 = n1 + jnp.where((s1 + v2[b][h:h + 1, :]) >= thr_h, 1.0, 0.0)
            n1_ref[h, :, cs] = n1
            e1_ref[h, :, cs] = jnp.exp(s1 - v1[0][h:h + 1, :])
            s2 = s_scr[2 * h + 1, :, cs]
            e2_ref[h, :, cs] = jnp.exp(s2 - v2[0][h:h + 1, :]) * rz[h:h + 1, :]
        return carry

    lax.fori_loop(0, ncol, fin_body, 0)


def _peer_gate(hn, wq, keys, *, tm=256):
    t = hn.shape[0]
    out = jax.ShapeDtypeStruct((PEER_HEADS, PEER_NKEYS, t), F32)
    ospec = pl.BlockSpec((PEER_HEADS, PEER_NKEYS, tm), lambda ti: (0, 0, ti))
    return pl.pallas_call(
        _gate_kernel,
        grid=(t // tm,),
        in_specs=[
            pl.BlockSpec((tm, D_MODEL), lambda ti: (ti, 0)),
            pl.BlockSpec((D_MODEL, 2 * PEER_HEADS * PEER_HALF), lambda ti: (0, 0)),
            pl.BlockSpec((2 * PEER_HEADS, PEER_NKEYS, PEER_HALF), lambda ti: (0, 0, 0)),
        ],
        out_specs=[ospec, ospec, ospec, ospec],
        out_shape=[out, out, out, out],
        scratch_shapes=[pltpu.VMEM((2 * PEER_HEADS, PEER_NKEYS, tm), F32),
                        pltpu.VMEM((2, PEER_TOPK, PEER_HEADS, tm), F32)],
        compiler_params=pltpu.CompilerParams(
            dimension_semantics=("arbitrary",), vmem_limit_bytes=VMEM_LIMIT),
        name="peer_gate",
    )(hn, wq, keys)


def _gelu_tanh(x):
    k1 = float(np.sqrt(2.0 / np.pi))
    k2 = k1 * 0.044715
    hx = 0.5 * x
    return hx + hx * jnp.tanh(x * (k1 + k2 * (x * x)))


def _peer_kernel(hn_ref, u_ref, vt_ref, n1_ref, e1_ref, rank2_ref, e2_ref, x_ref,
                 y_ref, acc_ref, a0_scr, a1_scr, g0_scr, g1_scr, r2_scr, e2_scr, *, n_i1):
    eb = pl.program_id(1)
    tm = hn_ref.shape[0]
    rows = PEER_NKEYS // BF16_ROWS

    @pl.when(eb == 0)
    def _():
        acc_ref[...] = jnp.zeros_like(acc_ref)
        a1_scr[...] = jnp.zeros_like(a1_scr)
        g1_scr[...] = jnp.zeros_like(g1_scr)
        r2_scr[...] = rank2_ref[...].astype(BF16)
        e2_scr[...] = e2_ref[...].astype(BF16)

    def step(a_prev, g_prev, a_cur, g_cur):
        half = tm // 2

        def scores(hf):
            return lax.dot_general(u_ref[...], hn_ref[hf * half:(hf + 1) * half, :],
                                   (((1,), (1,)), ((), ())), preferred_element_type=F32)

        def activate(c):
            cs = slice(c * LANES, (c + 1) * LANES)
            return _gelu_tanh(a_prev[:, cs]) * g_prev[:, cs]

        def retrieve(p):
            return jnp.dot(vt_ref[0], p, preferred_element_type=F32)

        def build_gate(c):
            cs = slice(c * LANES, (c + 1) * LANES)
            for j0 in range(0, n_i1, GATE_GROUP):
                group = range(j0, j0 + GATE_GROUP)
                gs = {j: jnp.zeros((rows, BF16_ROWS, LANES), BF16) for j in group}
                for h in range(PEER_HEADS):
                    r2 = r2_scr[h, :, cs].reshape(rows, BF16_ROWS, LANES)
                    e2 = e2_scr[h, :, cs].reshape(rows, BF16_ROWS, LANES)
                    for j in group:
                        n1row = jnp.broadcast_to(n1_ref[h, 0, j:j + 1, cs], (BF16_ROWS, LANES)).astype(BF16)
                        e1row = jnp.broadcast_to(e1_ref[h, 0, j:j + 1, cs], (BF16_ROWS, LANES)).astype(BF16)
                        gs[j] = gs[j] + jnp.where(r2 < n1row[None], e2, jnp.zeros_like(e2)) * e1row[None]
                for j in group:
                    g_cur[j * PEER_NKEYS:(j + 1) * PEER_NKEYS, cs] = gs[j].reshape(PEER_NKEYS, LANES)

        ncol = half // LANES
        a_new0 = scores(0)
        p0 = jnp.concatenate([activate(c) for c in range(ncol)], axis=1)
        out0 = retrieve(p0)
        p1 = jnp.concatenate([activate(c) for c in range(ncol, 2 * ncol)], axis=1)
        build_gate(0)
        out1 = retrieve(p1)
        build_gate(1)
        build_gate(2)
        a_new1 = scores(1)
        build_gate(3)
        acc_ref[:, :half] += out0
        acc_ref[:, half:] += out1
        a_cur[:, :half] = a_new0.astype(BF16)
        a_cur[:, half:] = a_new1.astype(BF16)

    @pl.when(eb % 2 == 0)
    def _():
        step(a1_scr, g1_scr, a0_scr, g0_scr)

    @pl.when(eb % 2 == 1)
    def _():
        step(a0_scr, g0_scr, a1_scr, g1_scr)

    @pl.when(eb == pl.num_programs(1) - 1)
    def _():
        y_ref[...] = x_ref[...] + acc_ref[...].T


def _peer_dense(hn, u, v, n1t, e1t, rank2t, e2t, x1, *, tm=512, n_i1=8):
    t = hn.shape[0]
    te = n_i1 * PEER_NKEYS
    vt = v.reshape(PEER_EXPERTS // te, te, D_MODEL).transpose(0, 2, 1)
    nblk = PEER_NKEYS // n_i1
    n1r = n1t.reshape(PEER_HEADS, nblk, n_i1, t)
    e1r = e1t.reshape(PEER_HEADS, nblk, n_i1, t)
    grid = (t // tm, nblk + 1)
    this_blk = lambda eb: jnp.minimum(eb, nblk - 1)
    prev_blk = lambda eb: jnp.maximum(eb - 1, 0)
    return pl.pallas_call(
        functools.partial(_peer_kernel, n_i1=n_i1),
        grid=grid,
        in_specs=[
            pl.BlockSpec((tm, D_MODEL), lambda ti, eb: (ti, 0)),
            pl.BlockSpec((te, D_MODEL), lambda ti, eb: (this_blk(eb), 0)),
            pl.BlockSpec((1, D_MODEL, te), lambda ti, eb: (prev_blk(eb), 0, 0)),
            pl.BlockSpec((PEER_HEADS, 1, n_i1, tm), lambda ti, eb: (0, this_blk(eb), 0, ti)),
            pl.BlockSpec((PEER_HEADS, 1, n_i1, tm), lambda ti, eb: (0, this_blk(eb), 0, ti)),
            pl.BlockSpec((PEER_HEADS, PEER_NKEYS, tm), lambda ti, eb: (0, 0, ti)),
            pl.BlockSpec((PEER_HEADS, PEER_NKEYS, tm), lambda ti, eb: (0, 0, ti)),
            pl.BlockSpec((tm, D_MODEL), lambda ti, eb: (ti, 0)),
        ],
        out_specs=pl.BlockSpec((tm, D_MODEL), lambda ti, eb: (ti, 0)),
        out_shape=jax.ShapeDtypeStruct((t, D_MODEL), F32),
        scratch_shapes=[pltpu.VMEM((D_MODEL, tm), F32),
                        pltpu.VMEM((te, tm), BF16), pltpu.VMEM((te, tm), BF16),
                        pltpu.VMEM((te, tm), BF16), pltpu.VMEM((te, tm), BF16),
                        pltpu.VMEM((PEER_HEADS, PEER_NKEYS, tm), BF16),
                        pltpu.VMEM((PEER_HEADS, PEER_NKEYS, tm), BF16)],
        compiler_params=pltpu.CompilerParams(
            dimension_semantics=("arbitrary", "arbitrary"),
            vmem_limit_bytes=VMEM_LIMIT),
        name="peer_dense",
    )(hn, u, vt, n1r, e1r, rank2t, e2t, x1)


def _row_rms(x, width):
    return x * lax.rsqrt(jnp.sum(x * x, axis=-1, keepdims=True) * (1.0 / width) + EPS)


def _proj_kernel(x_ref, g_in_ref, w_in_ref, g_q_ref, w_uq_ref, g_kv_ref, w_uk_ref, w_uv_ref,
                 g_qh_ref, g_kh_ref, cos_ref, sin_lo_ref, sin_hi_ref, v_one_ref, dft_c_ref,
                 q_ref, kt_ref, v_ref, f_ref):
    o0, o1, o2 = Q_LORA, Q_LORA + KV_LORA, Q_LORA + KV_LORA + LANES
    h = (_row_rms(x_ref[0], D_MODEL) * g_in_ref[...]).astype(BF16)
    p = jnp.dot(h, w_in_ref[...], preferred_element_type=F32)
    c_q = (_row_rms(p[:, :o0], Q_LORA) * g_q_ref[...]).astype(BF16)
    c_kv = (_row_rms(p[:, o0:o1], KV_LORA) * g_kv_ref[...]).astype(BF16)
    k_rope = p[:, o1:o2]
    f_ref[0] = jnp.dot(p[:, o2:].astype(BF16), dft_c_ref[...],
                       preferred_element_type=F32).astype(BF16)
    q = jnp.dot(c_q, w_uq_ref[...], preferred_element_type=F32)
    k = jnp.dot(c_kv, w_uk_ref[...], preferred_element_type=F32)
    v = jnp.dot(c_kv, w_uv_ref[...], preferred_element_type=F32)
    v_ref[0] = (v + v_one_ref[...]).astype(BF16)

    cos, sin_lo, sin_hi = cos_ref[...], sin_lo_ref[...], sin_hi_ref[...]
    half = QK_ROPE // 2

    def head_norm_rope(slot, gain):
        y = _row_rms(slot, QK_HEAD) * gain
        return (y * cos + pltpu.roll(y, LANES - half, 1) * sin_lo
                + pltpu.roll(y, half, 1) * sin_hi)

    q_slots, k_slots = [], []
    for hd in range(N_HEADS):
        hs = slice(hd * LANES, (hd + 1) * LANES)
        q_slots.append(head_norm_rope(q[:, hs], g_qh_ref[...]))
        k_slots.append(head_norm_rope(k[:, hs] + k_rope, g_kh_ref[...]))
    q_ref[0] = jnp.concatenate(q_slots, axis=1).astype(BF16)
    kt_ref[0] = jnp.concatenate(k_slots, axis=1).T.astype(BF16)


def _slot_cols(w, width, n_in):
    w = w.reshape(n_in, N_HEADS, width)
    return jnp.pad(w, ((0, 0), (0, 0), (0, LANES - width))).reshape(n_in, N_HEADS * LANES)


def _project(x, attn_norm_g, w_in, q_lat_g, w_uq, kv_lat_g, w_ukv, q_head_g, k_head_g, *, tm=512):
    b, s, _ = x.shape
    o0, o1, o2 = Q_LORA, Q_LORA + KV_LORA, Q_LORA + KV_LORA + QK_ROPE
    k_rope_cols = jnp.pad(w_in[:, o1:o2], ((0, 0), (QK_NOPE, LANES - QK_HEAD)))
    w_in_p = jnp.concatenate([w_in[:, :o1], k_rope_cols, w_in[:, o2:]], axis=1).astype(BF16)
    w_uq_p = _slot_cols(w_uq, QK_HEAD, Q_LORA).astype(BF16)
    w_ukv_h = w_ukv.reshape(KV_LORA, N_HEADS, QK_NOPE + V_HEAD)
    w_uk_p = _slot_cols(w_ukv_h[:, :, :QK_NOPE].reshape(KV_LORA, -1), QK_NOPE, KV_LORA).astype(BF16)
    w_uv_p = _slot_cols(w_ukv_h[:, :, QK_NOPE:].reshape(KV_LORA, -1), V_HEAD, KV_LORA).astype(BF16)
    pad_gain = lambda g: jnp.pad(g, (0, LANES - QK_HEAD)).reshape(1, LANES)
    g_qh = pad_gain(q_head_g) * (QK_HEAD ** -0.5 * LOG2_E)
    g_kh = pad_gain(k_head_g)
    half = QK_ROPE // 2
    freqs = 1.0 / (ROPE_THETA ** (jnp.arange(half, dtype=F32) / half))
    ang = jnp.arange(s, dtype=F32)[:, None] * freqs[None, :]
    cos, sin = jnp.cos(ang), jnp.sin(ang)
    zeros = lambda n: jnp.zeros((s, n), F32)
    cos_t = jnp.concatenate([jnp.ones((s, QK_NOPE), F32), cos, cos, zeros(LANES - QK_HEAD)], axis=1)
    sin_lo = jnp.concatenate([zeros(QK_NOPE), -sin, zeros(LANES - QK_NOPE - half)], axis=1)
    sin_hi = jnp.concatenate([zeros(QK_NOPE + half), sin, zeros(LANES - QK_HEAD)], axis=1)
    v_one = jnp.tile((jnp.arange(LANES) == V_HEAD).astype(F32), N_HEADS).reshape(1, -1)
    row = lambda g: g.reshape(1, -1)

    const = lambda shape: pl.BlockSpec(shape, lambda bi, si: (0,) * len(shape))
    tab = pl.BlockSpec((tm, LANES), lambda bi, si: (si, 0))
    wide = N_HEADS * LANES
    return pl.pallas_call(
        _proj_kernel,
        grid=(b, s // tm),
        in_specs=[
            pl.BlockSpec((1, tm, D_MODEL), lambda bi, si: (bi, si, 0)),
            const((1, D_MODEL)), const(w_in_p.shape), const((1, Q_LORA)), const(w_uq_p.shape),
            const((1, KV_LORA)), const(w_uk_p.shape), const(w_uv_p.shape),
            const((1, LANES)), const((1, LANES)), tab, tab, tab, const((1, wide)),
            const((FNET_W, 2 * FNET_W)),
        ],
        out_specs=[
            pl.BlockSpec((1, tm, wide), lambda bi, si: (bi, si, 0)),
            pl.BlockSpec((1, wide, tm), lambda bi, si: (bi, 0, si)),
            pl.BlockSpec((1, tm, wide), lambda bi, si: (bi, si, 0)),
            pl.BlockSpec((1, tm, 2 * FNET_W), lambda bi, si: (bi, si, 0)),
        ],
        out_shape=[
            jax.ShapeDtypeStruct((b, s, wide), BF16),
            jax.ShapeDtypeStruct((b, wide, s), BF16),
            jax.ShapeDtypeStruct((b, s, wide), BF16),
            jax.ShapeDtypeStruct((b, s, 2 * FNET_W), BF16),
        ],
        compiler_params=pltpu.CompilerParams(
            dimension_semantics=("arbitrary", "arbitrary"), vmem_limit_bytes=VMEM_LIMIT),
        name="input_projection",
    )(x, row(attn_norm_g), w_in_p, row(q_lat_g), w_uq_p, row(kv_lat_g), w_uk_p, w_uv_p,
      g_qh, g_kh, cos_t, sin_lo, sin_hi, v_one, _channel_dft(s))


def _mix_kernel(x_ref, a_ref, fm_ref, g_a_ref, g_f_ref, w_a_ref, w_f_ref, g_ffn_ref,
                x1_ref, hn_ref):
    a = (_row_rms(a_ref[...], ATTN_W) * g_a_ref[...]).astype(BF16)
    fm = (_row_rms(fm_ref[...], FNET_W) * g_f_ref[...]).astype(BF16)
    x1 = (x_ref[...] + jnp.dot(a, w_a_ref[...], preferred_element_type=F32)
          + jnp.dot(fm, w_f_ref[...], preferred_element_type=F32))
    x1_ref[...] = x1
    hn_ref[...] = (_row_rms(x1, D_MODEL) * g_ffn_ref[...]).astype(BF16)


def _mix(x, a, fm, attn_out_g, fnet_out_g, w_out, ffn_norm_g, *, tm=512):
    t = x.shape[0]
    row = lambda g: g.reshape(1, -1)
    tok = lambda w: pl.BlockSpec((tm, w), lambda ti: (ti, 0))
    const = lambda shape: pl.BlockSpec(shape, lambda ti: (0, 0))
    return pl.pallas_call(
        _mix_kernel,
        grid=(t // tm,),
        in_specs=[tok(D_MODEL), tok(ATTN_W), tok(FNET_W), const((1, ATTN_W)), const((1, FNET_W)),
                  const((ATTN_W, D_MODEL)), const((FNET_W, D_MODEL)), const((1, D_MODEL))],
        out_specs=[tok(D_MODEL), tok(D_MODEL)],
        out_shape=[jax.ShapeDtypeStruct((t, D_MODEL), F32), jax.ShapeDtypeStruct((t, D_MODEL), BF16)],
        compiler_params=pltpu.CompilerParams(
            dimension_semantics=("arbitrary",), vmem_limit_bytes=VMEM_LIMIT),
        name="output_mix",
    )(x, a, fm, row(attn_out_g), row(fnet_out_g), w_out[:ATTN_W].astype(BF16),
      w_out[ATTN_W:].astype(BF16), row(ffn_norm_g))


FFT_S1 = 64


def _dft(n):
    ang = -2.0 * np.pi * np.outer(np.arange(n), np.arange(n)) / n
    return np.cos(ang), np.sin(ang)


def _channel_dft(s):
    c, si = _dft(FNET_CH)
    eye = np.eye(FNET_GROUPS)
    scale = (FNET_CH * s) ** -0.5
    return jnp.asarray(np.concatenate([np.kron(eye, c), np.kron(eye, si)], axis=1) * scale, BF16)


def _fft_stage1_kernel(x_ref, dr_ref, di_ref, twr_ref, twi_ref, o_ref):
    x = x_ref[0]
    yr = jnp.dot(dr_ref[...], x, preferred_element_type=F32)
    yi = jnp.dot(di_ref[...], x, preferred_element_type=F32)
    w = FNET_W
    for j in range(x.shape[1] // (2 * w)):
        re, im = slice(2 * j * w, (2 * j + 1) * w), slice((2 * j + 1) * w, (2 * j + 2) * w)
        ar = yr[:, re] - yi[:, im]
        ai = yr[:, im] + yi[:, re]
        tw = slice(j * LANES, (j + 1) * LANES)
        twr = jnp.concatenate([twr_ref[:, tw]] * (w // LANES), axis=1)
        twi = jnp.concatenate([twi_ref[:, tw]] * (w // LANES), axis=1)
        o_ref[0, :, re] = (ar * twr - ai * twi).astype(BF16)
        o_ref[0, :, im] = (ar * twi + ai * twr).astype(BF16)


def _fft_stage2_kernel(a_ref, m_ref, o_ref):
    w = FNET_W
    for j in range(a_ref.shape[1]):
        xa = a_ref[0, j]
        stacked = jnp.concatenate([xa[:, :w], xa[:, w:]], axis=0)
        o_ref[0, j] = jnp.dot(m_ref[...], stacked, preferred_element_type=F32)


def _fnet(fx, *, tn2=8, kb=8):
    b, s, wide = fx.shape
    s1, s2 = FFT_S1, s // FFT_S1
    d1r, d1i = _dft(s1)
    d2r, d2i = _dft(s2)
    ang = -2.0 * np.pi / s * jnp.outer(jnp.arange(s1, dtype=F32), jnp.arange(s2, dtype=F32))
    twr = jnp.repeat(jnp.cos(ang), LANES, axis=1)
    twi = jnp.repeat(jnp.sin(ang), LANES, axis=1)
    cols = tn2 * wide
    x2 = fx.reshape(b, s1, s2 * wide)
    a = pl.pallas_call(
        _fft_stage1_kernel,
        grid=(b, s2 // tn2),
        in_specs=[
            pl.BlockSpec((1, s1, cols), lambda bi, ci: (bi, 0, ci)),
            pl.BlockSpec((s1, s1), lambda bi, ci: (0, 0)),
            pl.BlockSpec((s1, s1), lambda bi, ci: (0, 0)),
            pl.BlockSpec((s1, tn2 * LANES), lambda bi, ci: (0, ci)),
            pl.BlockSpec((s1, tn2 * LANES), lambda bi, ci: (0, ci)),
        ],
        out_specs=pl.BlockSpec((1, s1, cols), lambda bi, ci: (bi, 0, ci)),
        out_shape=jax.ShapeDtypeStruct((b, s1, s2 * wide), BF16),
        compiler_params=pltpu.CompilerParams(
            dimension_semantics=("arbitrary", "arbitrary"), vmem_limit_bytes=VMEM_LIMIT),
        name="fft_stage1",
    )(x2, jnp.asarray(d1r, BF16), jnp.asarray(d1i, BF16), twr, twi)
    a4 = a.reshape(b, s1, s2, wide)
    m2 = jnp.asarray(np.concatenate([d2r, -d2i], axis=1), BF16)
    y = pl.pallas_call(
        _fft_stage2_kernel,
        grid=(b, s1 // kb),
        in_specs=[
            pl.BlockSpec((1, kb, s2, wide), lambda bi, ki: (bi, ki, 0, 0)),
            pl.BlockSpec((s2, 2 * s2), lambda bi, ki: (0, 0)),
        ],
        out_specs=pl.BlockSpec((1, kb, s2, FNET_W), lambda bi, ki: (bi, ki, 0, 0)),
        out_shape=jax.ShapeDtypeStruct((b, s1, s2, FNET_W), F32),
        compiler_params=pltpu.CompilerParams(
            dimension_semantics=("arbitrary", "arbitrary"), vmem_limit_bytes=VMEM_LIMIT),
        name="fft_stage2",
    )(a4, m2)
    return y.transpose(0, 2, 1, 3).reshape(b, s, FNET_W)


def kernel(x_prompt, x_sample, attn_norm_g, w_in, q_lat_g, w_uq, kv_lat_g, w_ukv, q_head_g,
           k_head_g, attn_out_g, fnet_out_g, w_out, ffn_norm_g, peer_w_q, peer_sub_keys,
           peer_u, peer_v):
    l = 0
    keys = peer_sub_keys[l].reshape(2 * PEER_HEADS, PEER_NKEYS, PEER_HALF).astype(BF16)
    w_pq, u, v_tab = peer_w_q[l].astype(BF16), peer_u[l].astype(BF16), peer_v[l].astype(BF16)
    outs = []
    for x in (x_prompt, x_sample):
        q, kt, v, f = _project(x, attn_norm_g[l], w_in[l], q_lat_g[l], w_uq[l], kv_lat_g[l],
                               w_ukv[l], q_head_g[l], k_head_g[l])
        a = _flash_attention(q, kt, v).reshape(-1, ATTN_W)
        fm = _fnet(f).reshape(-1, FNET_W)
        x1, hn = _mix(x.reshape(-1, D_MODEL), a, fm, attn_out_g[l], fnet_out_g[l], w_out[l],
                      ffn_norm_g[l])
        rank2t, e2t, n1t, e1t = _peer_gate(hn, w_pq, keys)
        outs.append(_peer_dense(hn, u, v_tab, n1t, e1t, rank2t, e2t, x1).reshape(x.shape))
    return tuple(outs)
```

```python
import functools

import jax
import jax.numpy as jnp
import numpy as np
from jax import lax
from jax.experimental import pallas as pl
from jax.experimental.pallas import tpu as pltpu

D_MODEL = 1024
N_HEADS = 8
QK_NOPE = 64
QK_ROPE = 32
QK_HEAD = QK_NOPE + QK_ROPE
V_HEAD = 64
Q_LORA = 384
KV_LORA = 256
ATTN_W = N_HEADS * V_HEAD
ROPE_THETA = 10000.0
FNET_W = D_MODEL - ATTN_W
FNET_GROUPS = 4
FNET_CH = FNET_W // FNET_GROUPS
PEER_HEADS = 8
PEER_NKEYS = 128
PEER_EXPERTS = PEER_NKEYS * PEER_NKEYS
PEER_HALF = 128
PEER_TOPK = 16
EPS = 1e-6
LOG2_E = 1.4426950408889634

LANES = 128
BF16_ROWS = 16
GATE_GROUP = 4
VMEM_LIMIT = 48 * 1024 * 1024

F32 = jnp.float32
BF16 = jnp.bfloat16


def _flash_kernel(q_ref, kt_ref, v_ref, o_ref, m_scr, acc_scr, *, tk):
    seq = v_ref.shape[1]
    tq = q_ref.shape[1]
    nkv = seq // tk
    m_scr[...] = jnp.full(m_scr.shape, -jnp.inf, F32)
    acc_scr[...] = jnp.zeros(acc_scr.shape, F32)

    def body(j, carry):
        start = pl.multiple_of(j * tk, tk)
        heads = [slice(hh * LANES, (hh + 1) * LANES) for hh in range(2)]
        scores = [jnp.dot(q_ref[0, :, hs], kt_ref[0, hs, pl.ds(start, tk)],
                          preferred_element_type=F32) for hs in heads]
        for hh, hs in enumerate(heads):
            cols = [scores[hh][:, c * LANES:(c + 1) * LANES] for c in range(tk // LANES)]
            m_old = m_scr[hh]
            m_new = jnp.maximum(m_old, jnp.max(functools.reduce(jnp.maximum, cols),
                                               axis=1, keepdims=True))
            p = jnp.concatenate([jnp.exp2(c - m_new) for c in cols], axis=1).astype(BF16)
            pv = jnp.dot(p, v_ref[0, pl.ds(start, tk), hs], preferred_element_type=F32)
            acc_scr[hh] = jnp.exp2(m_old - m_new) * acc_scr[hh] + pv
            m_scr[hh] = m_new
        return carry

    lax.fori_loop(0, nkv, body, 0)
    outs = []
    for hh in range(2):
        acc = acc_scr[hh]
        outs.append(acc[:, :V_HEAD] / acc[:, V_HEAD:V_HEAD + 1])
    o_ref[0] = jnp.concatenate(outs, axis=1)


def _flash_attention(q, kt, v, *, tq=512, tk=512):
    b, s, _ = q.shape
    grid = (b, N_HEADS // 2, s // tq)
    return pl.pallas_call(
        functools.partial(_flash_kernel, tk=tk),
        grid=grid,
        in_specs=[
            pl.BlockSpec((1, tq, 2 * LANES), lambda bi, hp, qi: (bi, qi, hp)),
            pl.BlockSpec((1, 2 * LANES, s), lambda bi, hp, qi: (bi, hp, 0)),
            pl.BlockSpec((1, s, 2 * LANES), lambda bi, hp, qi: (bi, 0, hp)),
        ],
        out_specs=pl.BlockSpec((1, tq, LANES), lambda bi, hp, qi: (bi, qi, hp)),
        out_shape=jax.ShapeDtypeStruct((b, s, ATTN_W), F32),
        scratch_shapes=[pltpu.VMEM((2, tq, LANES), F32), pltpu.VMEM((2, tq, LANES), F32)],
        compiler_params=pltpu.CompilerParams(
            dimension_semantics=("arbitrary", "arbitrary", "arbitrary"),
            vmem_limit_bytes=VMEM_LIMIT),
        name="flash_attention",
    )(q, kt, v)


_CAND_PAIRS = tuple((a, b) for a in range(PEER_TOPK) for b in range(PEER_TOPK)
                    if (a + 1) * (b + 1) <= PEER_TOPK)
_NEG_INF = float("-inf")


def _tree_max(xs):
    xs = list(xs)
    while len(xs) > 1:
        nxt = [jnp.maximum(xs[i], xs[i + 1]) for i in range(0, len(xs) - 1, 2)]
        if len(xs) % 2:
            nxt.append(xs[-1])
        xs = nxt
    return xs[0]


def _gate_kernel(hn_ref, wq_ref, keys_ref, rank2_ref, e2_ref, n1_ref, e1_ref, s_scr, vals_scr):
    tm = hn_ref.shape[0]
    ncol = tm // LANES
    q = jnp.dot(hn_ref[...], wq_ref[...], preferred_element_type=F32).astype(BF16)
    for hc in range(2 * PEER_HEADS):
        s_scr[hc] = lax.dot_general(keys_ref[hc], q[:, hc * PEER_HALF:(hc + 1) * PEER_HALF],
                                    (((1,), (1,)), ((), ())), preferred_element_type=F32)

    for hc in range(2 * PEER_HEADS):
        h, c = divmod(hc, 2)

        def col_body(col, carry, hc=hc, h=h, c=c):
            cs = pl.ds(pl.multiple_of(col * LANES, LANES), LANES)
            cur = s_scr[hc, :, cs]
            rank = jnp.full((PEER_NKEYS, LANES), float(PEER_TOPK), F32)
            for r in range(PEER_TOPK):
                m = jnp.max(cur, axis=0, keepdims=True)
                hit = cur == m
                if c == 1:
                    rank = jnp.where(hit, float(r), rank)
                cur = jnp.where(hit, _NEG_INF, cur)
                vals_scr[c, r, h:h + 1, cs] = m
            if c == 1:
                rank2_ref[h, :, cs] = rank
            return carry

        lax.fori_loop(0, ncol, col_body, 0)

    def fin_body(col, carry):
        cs = pl.ds(pl.multiple_of(col * LANES, LANES), LANES)
        v1 = [vals_scr[0, a, :, cs] for a in range(PEER_TOPK)]
        v2 = [vals_scr[1, b, :, cs] for b in range(PEER_TOPK)]
        cands = [v1[a] + v2[b] for a, b in _CAND_PAIRS]
        top = cands[0]
        z = jnp.zeros_like(top)
        m = top
        for r in range(PEER_TOPK):
            m = _tree_max(cands)
            z = z + jnp.exp(m - top)
            if r + 1 < PEER_TOPK:
                cands = [jnp.where(cd == m, _NEG_INF, cd) for cd in cands]
        thr = m
        rz = 1.0 / z
        for h in range(PEER_HEADS):
            thr_h = thr[h:h + 1, :]
            s1 = s_scr[2 * h, :, cs]
            n1 = jnp.zeros((PEER_NKEYS, LANES), F32)
            for b in range(PEER_TOPK):
                n1 = n1 + jnp.where((s1 + v2[b][h:h + 1, :]) >= thr_h, 1.0, 0.0)
            n1_ref[h, :, cs] = n1
            e1_ref[h, :, cs] = jnp.exp(s1 - v1[0][h:h + 1, :])
            s2 = s_scr[2 * h + 1, :, cs]
            e2_ref[h, :, cs] = jnp.exp(s2 - v2[0][h:h + 1, :]) * rz[h:h + 1, :]
        return carry

    lax.fori_loop(0, ncol, fin_body, 0)


def _peer_gate(hn, wq, keys, *, tm=256):
    t = hn.shape[0]
    out = jax.ShapeDtypeStruct((PEER_HEADS, PEER_NKEYS, t), F32)
    ospec = pl.BlockSpec((PEER_HEADS, PEER_NKEYS, tm), lambda ti: (0, 0, ti))
    return pl.pallas_call(
        _gate_kernel,
        grid=(t // tm,),
        in_specs=[
            pl.BlockSpec((tm, D_MODEL), lambda ti: (ti, 0)),
            pl.BlockSpec((D_MODEL, 2 * PEER_HEADS * PEER_HALF), lambda ti: (0, 0)),
            pl.BlockSpec((2 * PEER_HEADS, PEER_NKEYS, PEER_HALF), lambda ti: (0, 0, 0)),
        ],
        out_specs=[ospec, ospec, ospec, ospec],
        out_shape=[out, out, out, out],
        scratch_shapes=[pltpu.VMEM((2 * PEER_HEADS, PEER_NKEYS, tm), F32),
                        pltpu.VMEM((2, PEER_TOPK, PEER_HEADS, tm), F32)],
        compiler_params=pltpu.CompilerParams(
            dimension_semantics=("arbitrary",), vmem_limit_bytes=VMEM_LIMIT),
        name="peer_gate",
    )(hn, wq, keys)


def _gelu_tanh(x):
    k1 = float(np.sqrt(2.0 / np.pi))
    k2 = k1 * 0.044715
    hx = 0.5 * x
    return hx + hx * jnp.tanh(x * (k1 + k2 * (x * x)))


def _peer_kernel(hn_ref, u_ref, vt_ref, n1_ref, e1_ref, rank2_ref, e2_ref, x_ref,
                 y_ref, acc_ref, a0_scr, a1_scr, g0_scr, g1_scr, r2_scr, e2_scr, *, n_i1):
    eb = pl.program_id(1)
    tm = hn_ref.shape[0]
    rows = PEER_NKEYS // BF16_ROWS

    @pl.when(eb == 0)
    def _():
        acc_ref[...] = jnp.zeros_like(acc_ref)
        a1_scr[...] = jnp.zeros_like(a1_scr)
        g1_scr[...] = jnp.zeros_like(g1_scr)
        r2_scr[...] = rank2_ref[...].astype(BF16)
        e2_scr[...] = e2_ref[...].astype(BF16)

    def step(a_prev, g_prev, a_cur, g_cur):
        def activate(c):
            cs = slice(c * LANES, (c + 1) * LANES)
            return _gelu_tanh(a_prev[:, cs]) * g_prev[:, cs]

        def build_gate(c):
            cs = slice(c * LANES, (c + 1) * LANES)
            for j0 in range(0, n_i1, GATE_GROUP):
                group = range(j0, j0 + GATE_GROUP)
                gs = {j: jnp.zeros((rows, BF16_ROWS, LANES), BF16) for j in group}
                for h in range(PEER_HEADS):
                    r2 = r2_scr[h, :, cs].reshape(rows, BF16_ROWS, LANES)
                    e2 = e2_scr[h, :, cs].reshape(rows, BF16_ROWS, LANES)
                    for j in group:
                        n1row = jnp.broadcast_to(n1_ref[h, 0, j:j + 1, cs], (BF16_ROWS, LANES)).astype(BF16)
                        e1row = jnp.broadcast_to(e1_ref[h, 0, j:j + 1, cs], (BF16_ROWS, LANES)).astype(BF16)
                        gs[j] = gs[j] + jnp.where(r2 < n1row[None], e2, jnp.zeros_like(e2)) * e1row[None]
                for j in group:
                    g_cur[j * PEER_NKEYS:(j + 1) * PEER_NKEYS, cs] = gs[j].reshape(PEER_NKEYS, LANES)

        ncol = tm // LANES
        a_new = lax.dot_general(u_ref[...], hn_ref[...], (((1,), (1,)), ((), ())),
                                preferred_element_type=F32)
        p = jnp.concatenate([activate(c) for c in range(ncol)], axis=1)
        out = jnp.dot(vt_ref[0], p, preferred_element_type=F32)
        for c in range(ncol):
            build_gate(c)
        acc_ref[...] += out
        a_cur[...] = a_new.astype(BF16)

    @pl.when(eb % 2 == 0)
    def _():
        step(a1_scr, g1_scr, a0_scr, g0_scr)

    @pl.when(eb % 2 == 1)
    def _():
        step(a0_scr, g0_scr, a1_scr, g1_scr)

    @pl.when(eb == pl.num_programs(1) - 1)
    def _():
        y_ref[...] = x_ref[...] + acc_ref[...].T


def _peer_dense(hn, u, v, n1t, e1t, rank2t, e2t, x1, *, tm=512, n_i1=8):
    t = hn.shape[0]
    te = n_i1 * PEER_NKEYS
    vt = v.reshape(PEER_EXPERTS // te, te, D_MODEL).transpose(0, 2, 1)
    nblk = PEER_NKEYS // n_i1
    n1r = n1t.reshape(PEER_HEADS, nblk, n_i1, t)
    e1r = e1t.reshape(PEER_HEADS, nblk, n_i1, t)
    grid = (t // tm, nblk + 1)
    this_blk = lambda eb: jnp.minimum(eb, nblk - 1)
    prev_blk = lambda eb: jnp.maximum(eb - 1, 0)
    return pl.pallas_call(
        functools.partial(_peer_kernel, n_i1=n_i1),
        grid=grid,
        in_specs=[
            pl.BlockSpec((tm, D_MODEL), lambda ti, eb: (ti, 0)),
            pl.BlockSpec((te, D_MODEL), lambda ti, eb: (this_blk(eb), 0)),
            pl.BlockSpec((1, D_MODEL, te), lambda ti, eb: (prev_blk(eb), 0, 0)),
            pl.BlockSpec((PEER_HEADS, 1, n_i1, tm), lambda ti, eb: (0, this_blk(eb), 0, ti)),
            pl.BlockSpec((PEER_HEADS, 1, n_i1, tm), lambda ti, eb: (0, this_blk(eb), 0, ti)),
            pl.BlockSpec((PEER_HEADS, PEER_NKEYS, tm), lambda ti, eb: (0, 0, ti)),
            pl.BlockSpec((PEER_HEADS, PEER_NKEYS, tm), lambda ti, eb: (0, 0, ti)),
            pl.BlockSpec((tm, D_MODEL), lambda ti, eb: (ti, 0)),
        ],
        out_specs=pl.BlockSpec((tm, D_MODEL), lambda ti, eb: (ti, 0)),
        out_shape=jax.ShapeDtypeStruct((t, D_MODEL), F32),
        scratch_shapes=[pltpu.VMEM((D_MODEL, tm), F32),
                        pltpu.VMEM((te, tm), BF16), pltpu.VMEM((te, tm), BF16),
                        pltpu.VMEM((te, tm), BF16), pltpu.VMEM((te, tm), BF16),
                        pltpu.VMEM((PEER_HEADS, PEER_NKEYS, tm), BF16),
                        pltpu.VMEM((PEER_HEADS, PEER_NKEYS, tm), BF16)],
        compiler_params=pltpu.CompilerParams(
            dimension_semantics=("arbitrary", "arbitrary"),
            vmem_limit_bytes=VMEM_LIMIT),
        name="peer_dense",
    )(hn, u, vt, n1r, e1r, rank2t, e2t, x1)


def _row_rms(x, width):
    return x * lax.rsqrt(jnp.sum(x * x, axis=-1, keepdims=True) * (1.0 / width) + EPS)


def _proj_kernel(x_ref, g_in_ref, w_in_ref, g_q_ref, w_uq_ref, g_kv_ref, w_uk_ref, w_uv_ref,
                 g_qh_ref, g_kh_ref, cos_ref, sin_lo_ref, sin_hi_ref, v_one_ref, dft_c_ref,
                 q_ref, kt_ref, v_ref, f_ref):
    o0, o1, o2 = Q_LORA, Q_LORA + KV_LORA, Q_LORA + KV_LORA + LANES
    h = (_row_rms(x_ref[0], D_MODEL) * g_in_ref[...]).astype(BF16)
    p = jnp.dot(h, w_in_ref[...], preferred_element_type=F32)
    c_q = (_row_rms(p[:, :o0], Q_LORA) * g_q_ref[...]).astype(BF16)
    c_kv = (_row_rms(p[:, o0:o1], KV_LORA) * g_kv_ref[...]).astype(BF16)
    k_rope = p[:, o1:o2]
    f_ref[0] = jnp.dot(p[:, o2:].astype(BF16), dft_c_ref[...],
                       preferred_element_type=F32).astype(BF16)
    q = jnp.dot(c_q, w_uq_ref[...], preferred_element_type=F32)
    k = jnp.dot(c_kv, w_uk_ref[...], preferred_element_type=F32)
    v = jnp.dot(c_kv, w_uv_ref[...], preferred_element_type=F32)
    v_ref[0] = (v + v_one_ref[...]).astype(BF16)

    cos, sin_lo, sin_hi = cos_ref[...], sin_lo_ref[...], sin_hi_ref[...]
    half = QK_ROPE // 2

    def head_norm_rope(slot, gain):
        y = _row_rms(slot, QK_HEAD) * gain
        return (y * cos + pltpu.roll(y, LANES - half, 1) * sin_lo
                + pltpu.roll(y, half, 1) * sin_hi)

    q_slots, k_slots = [], []
    for hd in range(N_HEADS):
        hs = slice(hd * LANES, (hd + 1) * LANES)
        q_slots.append(head_norm_rope(q[:, hs], g_qh_ref[...]))
        k_slots.append(head_norm_rope(k[:, hs] + k_rope, g_kh_ref[...]))
    q_ref[0] = jnp.concatenate(q_slots, axis=1).astype(BF16)
    kt_ref[0] = jnp.concatenate(k_slots, axis=1).T.astype(BF16)


def _slot_cols(w, width, n_in):
    w = w.reshape(n_in, N_HEADS, width)
    return jnp.pad(w, ((0, 0), (0, 0), (0, LANES - width))).reshape(n_in, N_HEADS * LANES)


def _project(x, attn_norm_g, w_in, q_lat_g, w_uq, kv_lat_g, w_ukv, q_head_g, k_head_g, *, tm=512):
    b, s, _ = x.shape
    o0, o1, o2 = Q_LORA, Q_LORA + KV_LORA, Q_LORA + KV_LORA + QK_ROPE
    k_rope_cols = jnp.pad(w_in[:, o1:o2], ((0, 0), (QK_NOPE, LANES - QK_HEAD)))
    w_in_p = jnp.concatenate([w_in[:, :o1], k_rope_cols, w_in[:, o2:]], axis=1).astype(BF16)
    w_uq_p = _slot_cols(w_uq, QK_HEAD, Q_LORA).astype(BF16)
    w_ukv_h = w_ukv.reshape(KV_LORA, N_HEADS, QK_NOPE + V_HEAD)
    w_uk_p = _slot_cols(w_ukv_h[:, :, :QK_NOPE].reshape(KV_LORA, -1), QK_NOPE, KV_LORA).astype(BF16)
    w_uv_p = _slot_cols(w_ukv_h[:, :, QK_NOPE:].reshape(KV_LORA, -1), V_HEAD, KV_LORA).astype(BF16)
    pad_gain = lambda g: jnp.pad(g, (0, LANES - QK_HEAD)).reshape(1, LANES)
    g_qh = pad_gain(q_head_g) * (QK_HEAD ** -0.5 * LOG2_E)
    g_kh = pad_gain(k_head_g)
    half = QK_ROPE // 2
    freqs = 1.0 / (ROPE_THETA ** (jnp.arange(half, dtype=F32) / half))
    ang = jnp.arange(s, dtype=F32)[:, None] * freqs[None, :]
    cos, sin = jnp.cos(ang), jnp.sin(ang)
    zeros = lambda n: jnp.zeros((s, n), F32)
    cos_t = jnp.concatenate([jnp.ones((s, QK_NOPE), F32), cos, cos, zeros(LANES - QK_HEAD)], axis=1)
    sin_lo = jnp.concatenate([zeros(QK_NOPE), -sin, zeros(LANES - QK_NOPE - half)], axis=1)
    sin_hi = jnp.concatenate([zeros(QK_NOPE + half), sin, zeros(LANES - QK_HEAD)], axis=1)
    v_one = jnp.tile((jnp.arange(LANES) == V_HEAD).astype(F32), N_HEADS).reshape(1, -1)
    row = lambda g: g.reshape(1, -1)

    const = lambda shape: pl.BlockSpec(shape, lambda bi, si: (0,) * len(shape))
    tab = pl.BlockSpec((tm, LANES), lambda bi, si: (si, 0))
    wide = N_HEADS * LANES
    return pl.pallas_call(
        _proj_kernel,
        grid=(b, s // tm),
        in_specs=[
            pl.BlockSpec((1, tm, D_MODEL), lambda bi, si: (bi, si, 0)),
            const((1, D_MODEL)), const(w_in_p.shape), const((1, Q_LORA)), const(w_uq_p.shape),
            const((1, KV_LORA)), const(w_uk_p.shape), const(w_uv_p.shape),
            const((1, LANES)), const((1, LANES)), tab, tab, tab, const((1, wide)),
            const((FNET_W, 2 * FNET_W)),
        ],
        out_specs=[
            pl.BlockSpec((1, tm, wide), lambda bi, si: (bi, si, 0)),
            pl.BlockSpec((1, wide, tm), lambda bi, si: (bi, 0, si)),
            pl.BlockSpec((1, tm, wide), lambda bi, si: (bi, si, 0)),
            pl.BlockSpec((1, tm, 2 * FNET_W), lambda bi, si: (bi, si, 0)),
        ],
        out_shape=[
            jax.ShapeDtypeStruct((b, s, wide), BF16),
            jax.ShapeDtypeStruct((b, wide, s), BF16),
            jax.ShapeDtypeStruct((b, s, wide), BF16),
            jax.ShapeDtypeStruct((b, s, 2 * FNET_W), BF16),
        ],
        compiler_params=pltpu.CompilerParams(
            dimension_semantics=("arbitrary", "arbitrary"), vmem_limit_bytes=VMEM_LIMIT),
        name="input_projection",
    )(x, row(attn_norm_g), w_in_p, row(q_lat_g), w_uq_p, row(kv_lat_g), w_uk_p, w_uv_p,
      g_qh, g_kh, cos_t, sin_lo, sin_hi, v_one, _channel_dft(s))


def _mix_kernel(x_ref, a_ref, fm_ref, g_a_ref, g_f_ref, w_a_ref, w_f_ref, g_ffn_ref,
                x1_ref, hn_ref):
    a = (_row_rms(a_ref[...], ATTN_W) * g_a_ref[...]).astype(BF16)
    fm = (_row_rms(fm_ref[...], FNET_W) * g_f_ref[...]).astype(BF16)
    x1 = (x_ref[...] + jnp.dot(a, w_a_ref[...], preferred_element_type=F32)
          + jnp.dot(fm, w_f_ref[...], preferred_element_type=F32))
    x1_ref[...] = x1
    hn_ref[...] = (_row_rms(x1, D_MODEL) * g_ffn_ref[...]).astype(BF16)


def _mix(x, a, fm, attn_out_g, fnet_out_g, w_out, ffn_norm_g, *, tm=512):
    t = x.shape[0]
    row = lambda g: g.reshape(1, -1)
    tok = lambda w: pl.BlockSpec((tm, w), lambda ti: (ti, 0))
    const = lambda shape: pl.BlockSpec(shape, lambda ti: (0, 0))
    return pl.pallas_call(
        _mix_kernel,
        grid=(t // tm,),
        in_specs=[tok(D_MODEL), tok(ATTN_W), tok(FNET_W), const((1, ATTN_W)), const((1, FNET_W)),
                  const((ATTN_W, D_MODEL)), const((FNET_W, D_MODEL)), const((1, D_MODEL))],
        out_specs=[tok(D_MODEL), tok(D_MODEL)],
        out_shape=[jax.ShapeDtypeStruct((t, D_MODEL), F32), jax.ShapeDtypeStruct((t, D_MODEL), BF16)],
        compiler_params=pltpu.CompilerParams(
            dimension_semantics=("arbitrary",), vmem_limit_bytes=VMEM_LIMIT),
        name="output_mix",
    )(x, a, fm, row(attn_out_g), row(fnet_out_g), w_out[:ATTN_W].astype(BF16),
      w_out[ATTN_W:].astype(BF16), row(ffn_norm_g))


FFT_S1 = 64


def _dft(n):
    ang = -2.0 * np.pi * np.outer(np.arange(n), np.arange(n)) / n
    return np.cos(ang), np.sin(ang)


def _channel_dft(s):
    c, si = _dft(FNET_CH)
    eye = np.eye(FNET_GROUPS)
    scale = (FNET_CH * s) ** -0.5
    return jnp.asarray(np.concatenate([np.kron(eye, c), np.kron(eye, si)], axis=1) * scale, BF16)


def _fft_stage1_kernel(x_ref, dr_ref, di_ref, twr_ref, twi_ref, o_ref):
    x = x_ref[0]
    yr = jnp.dot(dr_ref[...], x, preferred_element_type=F32)
    yi = jnp.dot(di_ref[...], x, preferred_element_type=F32)
    w = FNET_W
    for j in range(x.shape[1] // (2 * w)):
        re, im = slice(2 * j * w, (2 * j + 1) * w), slice((2 * j + 1) * w, (2 * j + 2) * w)
        ar = yr[:, re] - yi[:, im]
        ai = yr[:, im] + yi[:, re]
        tw = slice(j * LANES, (j + 1) * LANES)
        twr = jnp.concatenate([twr_ref[:, tw]] * (w // LANES), axis=1)
        twi = jnp.concatenate([twi_ref[:, tw]] * (w // LANES), axis=1)
        o_ref[0, :, re] = (ar * twr - ai * twi).astype(BF16)
        o_ref[0, :, im] = (ar * twi + ai * twr).astype(BF16)


def _fft_stage2_kernel(a_ref, m_ref, o_ref):
    w = FNET_W
    for j in range(a_ref.shape[1]):
        xa = a_ref[0, j]
        stacked = jnp.concatenate([xa[:, :w], xa[:, w:]], axis=0)
        o_ref[0, j] = jnp.dot(m_ref[...], stacked, preferred_element_type=F32)


def _fnet(fx, *, tn2=8, kb=8):
    b, s, wide = fx.shape
    s1, s2 = FFT_S1, s // FFT_S1
    d1r, d1i = _dft(s1)
    d2r, d2i = _dft(s2)
    ang = -2.0 * np.pi / s * jnp.outer(jnp.arange(s1, dtype=F32), jnp.arange(s2, dtype=F32))
    twr = jnp.repeat(jnp.cos(ang), LANES, axis=1)
    twi = jnp.repeat(jnp.sin(ang), LANES, axis=1)
    cols = tn2 * wide
    x2 = fx.reshape(b, s1, s2 * wide)
    a = pl.pallas_call(
        _fft_stage1_kernel,
        grid=(b, s2 // tn2),
        in_specs=[
            pl.BlockSpec((1, s1, cols), lambda bi, ci: (bi, 0, ci)),
            pl.BlockSpec((s1, s1), lambda bi, ci: (0, 0)),
            pl.BlockSpec((s1, s1), lambda bi, ci: (0, 0)),
            pl.BlockSpec((s1, tn2 * LANES), lambda bi, ci: (0, ci)),
            pl.BlockSpec((s1, tn2 * LANES), lambda bi, ci: (0, ci)),
        ],
        out_specs=pl.BlockSpec((1, s1, cols), lambda bi, ci: (bi, 0, ci)),
        out_shape=jax.ShapeDtypeStruct((b, s1, s2 * wide), BF16),
        compiler_params=pltpu.CompilerParams(
            dimension_semantics=("arbitrary", "arbitrary"), vmem_limit_bytes=VMEM_LIMIT),
        name="fft_stage1",
    )(x2, jnp.asarray(d1r, BF16), jnp.asarray(d1i, BF16), twr, twi)
    a4 = a.reshape(b, s1, s2, wide)
    m2 = jnp.asarray(np.concatenate([d2r, -d2i], axis=1), BF16)
    y = pl.pallas_call(
        _fft_stage2_kernel,
        grid=(b, s1 // kb),
        in_specs=[
            pl.BlockSpec((1, kb, s2, wide), lambda bi, ki: (bi, ki, 0, 0)),
            pl.BlockSpec((s2, 2 * s2), lambda bi, ki: (0, 0)),
        ],
        out_specs=pl.BlockSpec((1, kb, s2, FNET_W), lambda bi, ki: (bi, ki, 0, 0)),
        out_shape=jax.ShapeDtypeStruct((b, s1, s2, FNET_W), F32),
        compiler_params=pltpu.CompilerParams(
            dimension_semantics=("arbitrary", "arbitrary"), vmem_limit_bytes=VMEM_LIMIT),
        name="fft_stage2",
    )(a4, m2)
    return y.transpose(0, 2, 1, 3).reshape(b, s, FNET_W)


def kernel(x_prompt, x_sample, attn_norm_g, w_in, q_lat_g, w_uq, kv_lat_g, w_ukv, q_head_g,
           k_head_g, attn_out_g, fnet_out_g, w_out, ffn_norm_g, peer_w_q, peer_sub_keys,
           peer_u, peer_v):
    l = 0
    keys = peer_sub_keys[l].reshape(2 * PEER_HEADS, PEER_NKEYS, PEER_HALF).astype(BF16)
    w_pq, u, v_tab = peer_w_q[l].astype(BF16), peer_u[l].astype(BF16), peer_v[l].astype(BF16)
    outs = []
    for x in (x_prompt, x_sample):
        q, kt, v, f = _project(x, attn_norm_g[l], w_in[l], q_lat_g[l], w_uq[l], kv_lat_g[l],
                               w_ukv[l], q_head_g[l], k_head_g[l])
        a = _flash_attention(q, kt, v).reshape(-1, ATTN_W)
        fm = _fnet(f).reshape(-1, FNET_W)
        x1, hn = _mix(x.reshape(-1, D_MODEL), a, fm, attn_out_g[l], fnet_out_g[l], w_out[l],
                      ffn_norm_g[l])
        rank2t, e2t, n1t, e1t = _peer_gate(hn, w_pq, keys)
        outs.append(_peer_dense(hn, u, v_tab, n1t, e1t, rank2t, e2t, x1).reshape(x.shape))
    return tuple(outs)
```

```python
import functools

import jax
import jax.numpy as jnp
import numpy as np
from jax import lax
from jax.experimental import pallas as pl
from jax.experimental.pallas import tpu as pltpu

D_MODEL = 1024
N_HEADS = 8
QK_NOPE = 64
QK_ROPE = 32
QK_HEAD = QK_NOPE + QK_ROPE
V_HEAD = 64
Q_LORA = 384
KV_LORA = 256
ATTN_W = N_HEADS * V_HEAD
ROPE_THETA = 10000.0
FNET_W = D_MODEL - ATTN_W
FNET_GROUPS = 4
FNET_CH = FNET_W // FNET_GROUPS
PEER_HEADS = 8
PEER_NKEYS = 128
PEER_EXPERTS = PEER_NKEYS * PEER_NKEYS
PEER_HALF = 128
PEER_TOPK = 16
EPS = 1e-6
LOG2_E = 1.4426950408889634

LANES = 128
BF16_ROWS = 16
GATE_GROUP = 4
VMEM_LIMIT = 48 * 1024 * 1024

F32 = jnp.float32
BF16 = jnp.bfloat16


def _flash_kernel(q_ref, kt_ref, v_ref, o_ref, m_scr, acc_scr, *, tk):
    seq = v_ref.shape[1]
    tq = q_ref.shape[1]
    nkv = seq // tk
    m_scr[...] = jnp.full(m_scr.shape, -jnp.inf, F32)
    acc_scr[...] = jnp.zeros(acc_scr.shape, F32)

    def body(j, carry):
        start = pl.multiple_of(j * tk, tk)
        heads = [slice(hh * LANES, (hh + 1) * LANES) for hh in range(2)]
        scores = [jnp.dot(q_ref[0, :, hs], kt_ref[0, hs, pl.ds(start, tk)],
                          preferred_element_type=F32) for hs in heads]
        for hh, hs in enumerate(heads):
            cols = [scores[hh][:, c * LANES:(c + 1) * LANES] for c in range(tk // LANES)]
            m_old = m_scr[hh]
            m_new = jnp.maximum(m_old, jnp.max(functools.reduce(jnp.maximum, cols),
                                               axis=1, keepdims=True))
            p = jnp.concatenate([jnp.exp2(c - m_new) for c in cols], axis=1).astype(BF16)
            pv = jnp.dot(p, v_ref[0, pl.ds(start, tk), hs], preferred_element_type=F32)
            acc_scr[hh] = jnp.exp2(m_old - m_new) * acc_scr[hh] + pv
            m_scr[hh] = m_new
        return carry

    lax.fori_loop(0, nkv, body, 0)
    outs = []
    for hh in range(2):
        acc = acc_scr[hh]
        outs.append(acc[:, :V_HEAD] / acc[:, V_HEAD:V_HEAD + 1])
    o_ref[0] = jnp.concatenate(outs, axis=1)


def _flash_attention(q, kt, v, *, tq=512, tk=512):
    b, s, _ = q.shape
    grid = (b, N_HEADS // 2, s // tq)
    return pl.pallas_call(
        functools.partial(_flash_kernel, tk=tk),
        grid=grid,
        in_specs=[
            pl.BlockSpec((1, tq, 2 * LANES), lambda bi, hp, qi: (bi, qi, hp)),
            pl.BlockSpec((1, 2 * LANES, s), lambda bi, hp, qi: (bi, hp, 0)),
            pl.BlockSpec((1, s, 2 * LANES), lambda bi, hp, qi: (bi, 0, hp)),
        ],
        out_specs=pl.BlockSpec((1, tq, LANES), lambda bi, hp, qi: (bi, qi, hp)),
        out_shape=jax.ShapeDtypeStruct((b, s, ATTN_W), F32),
        scratch_shapes=[pltpu.VMEM((2, tq, LANES), F32), pltpu.VMEM((2, tq, LANES), F32)],
        compiler_params=pltpu.CompilerParams(
            dimension_semantics=("arbitrary", "arbitrary", "arbitrary"),
            vmem_limit_bytes=VMEM_LIMIT),
        name="flash_attention",
    )(q, kt, v)


_CAND_PAIRS = tuple((a, b) for a in range(PEER_TOPK) for b in range(PEER_TOPK)
                    if (a + 1) * (b + 1) <= PEER_TOPK)
_NEG_INF = float("-inf")


def _tree_max(xs):
    xs = list(xs)
    while len(xs) > 1:
        nxt = [jnp.maximum(xs[i], xs[i + 1]) for i in range(0, len(xs) - 1, 2)]
        if len(xs) % 2:
            nxt.append(xs[-1])
        xs = nxt
    return xs[0]


def _gate_kernel(hn_ref, wq_ref, keys_ref, rank2_ref, e2_ref, n1_ref, e1_ref, s_scr, vals_scr):
    tm = hn_ref.shape[0]
    ncol = tm // LANES
    q = jnp.dot(hn_ref[...], wq_ref[...], preferred_element_type=F32).astype(BF16)
    for hc in range(2 * PEER_HEADS):
        s_scr[hc] = lax.dot_general(keys_ref[hc], q[:, hc * PEER_HALF:(hc + 1) * PEER_HALF],
                                    (((1,), (1,)), ((), ())), preferred_element_type=F32)

    for hc in range(2 * PEER_HEADS):
        h, c = divmod(hc, 2)

        def col_body(col, carry, hc=hc, h=h, c=c):
            cs = pl.ds(pl.multiple_of(col * LANES, LANES), LANES)
            cur = s_scr[hc, :, cs]
            rank = jnp.full((PEER_NKEYS, LANES), float(PEER_TOPK), F32)
            for r in range(PEER_TOPK):
                m = jnp.max(cur, axis=0, keepdims=True)
                hit = cur == m
                if c == 1:
                    rank = jnp.where(hit, float(r), rank)
                cur = jnp.where(hit, _NEG_INF, cur)
                vals_scr[c, r, h:h + 1, cs] = m
            if c == 1:
                rank2_ref[h, :, cs] = rank
            return carry

        lax.fori_loop(0, ncol, col_body, 0)

    def fin_body(col, carry):
        cs = pl.ds(pl.multiple_of(col * LANES, LANES), LANES)
        v1 = [vals_scr[0, a, :, cs] for a in range(PEER_TOPK)]
        v2 = [vals_scr[1, b, :, cs] for b in range(PEER_TOPK)]
        cands = [v1[a] + v2[b] for a, b in _CAND_PAIRS]
        top = cands[0]
        z = jnp.zeros_like(top)
        m = top
        for r in range(PEER_TOPK):
            m = _tree_max(cands)
            z = z + jnp.exp(m - top)
            if r + 1 < PEER_TOPK:
                cands = [jnp.where(cd == m, _NEG_INF, cd) for cd in cands]
        thr = m
        rz = 1.0 / z
        for h in range(PEER_HEADS):
            thr_h = thr[h:h + 1, :]
            s1 = s_scr[2 * h, :, cs]
            n1 = jnp.zeros((PEER_NKEYS, LANES), F32)
            for b in range(PEER_TOPK):
                n1 = n1 + jnp.where((s1 + v2[b][h:h + 1, :]) >= thr_h, 1.0, 0.0)
            n1_ref[h, :, cs] = n1
            e1_ref[h, :, cs] = jnp.exp(s1 - v1[0][h:h + 1, :])
            s2 = s_scr[2 * h + 1, :, cs]
            e2_ref[h, :, cs] = jnp.exp(s2 - v2[0][h:h + 1, :]) * rz[h:h + 1, :]
        return carry

    lax.fori_loop(0, ncol, fin_body, 0)


def _peer_gate(hn, wq, keys, *, tm=256):
    t = hn.shape[0]
    out = jax.ShapeDtypeStruct((PEER_HEADS, PEER_NKEYS, t), F32)
    ospec = pl.BlockSpec((PEER_HEADS, PEER_NKEYS, tm), lambda ti: (0, 0, ti))
    return pl.pallas_call(
        _gate_kernel,
        grid=(t // tm,),
        in_specs=[
            pl.BlockSpec((tm, D_MODEL), lambda ti: (ti, 0)),
            pl.BlockSpec((D_MODEL, 2 * PEER_HEADS * PEER_HALF), lambda ti: (0, 0)),
            pl.BlockSpec((2 * PEER_HEADS, PEER_NKEYS, PEER_HALF), lambda ti: (0, 0, 0)),
        ],
        out_specs=[ospec, ospec, ospec, ospec],
        out_shape=[out, out, out, out],
        scratch_shapes=[pltpu.VMEM((2 * PEER_HEADS, PEER_NKEYS, tm), F32),
                        pltpu.VMEM((2, PEER_TOPK, PEER_HEADS, tm), F32)],
        compiler_params=pltpu.CompilerParams(
            dimension_semantics=("arbitrary",), vmem_limit_bytes=VMEM_LIMIT),
        name="peer_gate",
    )(hn, wq, keys)


def _gelu_tanh(x):
    k1 = float(np.sqrt(2.0 / np.pi))
    k2 = k1 * 0.044715
    hx = 0.5 * x
    return hx + hx * jnp.tanh(x * (k1 + k2 * (x * x)))


def _peer_kernel(hn_ref, u_ref, vt_ref, n1_ref, e1_ref, rank2_ref, e2_ref, x_ref,
                 y_ref, acc_ref, a0_scr, a1_scr, g0_scr, g1_scr, r2_scr, e2_scr, *, n_i1):
    eb = pl.program_id(1)
    tm = hn_ref.shape[0]
    rows = PEER_NKEYS // BF16_ROWS

    @pl.when(eb == 0)
    def _():
        acc_ref[...] = jnp.zeros_like(acc_ref)
        a1_scr[...] = jnp.zeros_like(a1_scr)
        g1_scr[...] = jnp.zeros_like(g1_scr)
        r2_scr[...] = rank2_ref[...].astype(BF16)
        e2_scr[...] = e2_ref[...].astype(BF16)

    def step(a_prev, g_prev, a_cur, g_cur):
        def activate(c):
            cs = slice(c * LANES, (c + 1) * LANES)
            return _gelu_tanh(a_prev[:, cs]) * g_prev[:, cs]

        def build_gate(c):
            cs = slice(c * LANES, (c + 1) * LANES)
            for j0 in range(0, n_i1, GATE_GROUP):
                group = range(j0, j0 + GATE_GROUP)
                gs = {j: jnp.zeros((rows, BF16_ROWS, LANES), BF16) for j in group}
                for h in range(PEER_HEADS):
                    r2 = r2_scr[h, :, cs].reshape(rows, BF16_ROWS, LANES)
                    e2 = e2_scr[h, :, cs].reshape(rows, BF16_ROWS, LANES)
                    for j in group:
                        n1row = jnp.broadcast_to(n1_ref[h, 0, j:j + 1, cs], (BF16_ROWS, LANES)).astype(BF16)
                        e1row = jnp.broadcast_to(e1_ref[h, 0, j:j + 1, cs], (BF16_ROWS, LANES)).astype(BF16)
                        gs[j] = gs[j] + jnp.where(r2 < n1row[None], e2, jnp.zeros_like(e2)) * e1row[None]
                for j in group:
                    g_cur[j * PEER_NKEYS:(j + 1) * PEER_NKEYS, cs] = gs[j].reshape(PEER_NKEYS, LANES)

        ncol = tm // LANES
        a_new = lax.dot_general(u_ref[...], hn_ref[...], (((1,), (1,)), ((), ())),
                                preferred_element_type=F32)
        p = jnp.concatenate([activate(c) for c in range(ncol)], axis=1)
        out = jnp.dot(vt_ref[0], p, preferred_element_type=F32)
        for c in range(ncol):
            build_gate(c)
        acc_ref[...] += out
        a_cur[...] = a_new.astype(BF16)

    @pl.when(eb % 2 == 0)
    def _():
        step(a1_scr, g1_scr, a0_scr, g0_scr)

    @pl.when(eb % 2 == 1)
    def _():
        step(a0_scr, g0_scr, a1_scr, g1_scr)

    @pl.when(eb == pl.num_programs(1) - 1)
    def _():
        y_ref[...] = x_ref[...] + acc_ref[...].T


def _peer_dense(hn, u, v, n1t, e1t, rank2t, e2t, x1, *, tm=512, n_i1=8):
    t = hn.shape[0]
    te = n_i1 * PEER_NKEYS
    vt = v.reshape(PEER_EXPERTS // te, te, D_MODEL).transpose(0, 2, 1)
    nblk = PEER_NKEYS // n_i1
    n1r = n1t.reshape(PEER_HEADS, nblk, n_i1, t)
    e1r = e1t.reshape(PEER_HEADS, nblk, n_i1, t)
    grid = (t // tm, nblk + 1)
    this_blk = lambda eb: jnp.minimum(eb, nblk - 1)
    prev_blk = lambda eb: jnp.maximum(eb - 1, 0)
    return pl.pallas_call(
        functools.partial(_peer_kernel, n_i1=n_i1),
        grid=grid,
        in_specs=[
            pl.BlockSpec((tm, D_MODEL), lambda ti, eb: (ti, 0)),
            pl.BlockSpec((te, D_MODEL), lambda ti, eb: (this_blk(eb), 0)),
            pl.BlockSpec((1, D_MODEL, te), lambda ti, eb: (prev_blk(eb), 0, 0)),
            pl.BlockSpec((PEER_HEADS, 1, n_i1, tm), lambda ti, eb: (0, this_blk(eb), 0, ti)),
            pl.BlockSpec((PEER_HEADS, 1, n_i1, tm), lambda ti, eb: (0, this_blk(eb), 0, ti)),
            pl.BlockSpec((PEER_HEADS, PEER_NKEYS, tm), lambda ti, eb: (0, 0, ti)),
            pl.BlockSpec((PEER_HEADS, PEER_NKEYS, tm), lambda ti, eb: (0, 0, ti)),
            pl.BlockSpec((tm, D_MODEL), lambda ti, eb: (ti, 0)),
        ],
        out_specs=pl.BlockSpec((tm, D_MODEL), lambda ti, eb: (ti, 0)),
        out_shape=jax.ShapeDtypeStruct((t, D_MODEL), F32),
        scratch_shapes=[pltpu.VMEM((D_MODEL, tm), F32),
                        pltpu.VMEM((te, tm), BF16), pltpu.VMEM((te, tm), BF16),
                        pltpu.VMEM((te, tm), BF16), pltpu.VMEM((te, tm), BF16),
                        pltpu.VMEM((PEER_HEADS, PEER_NKEYS, tm), BF16),
                        pltpu.VMEM((PEER_HEADS, PEER_NKEYS, tm), BF16)],
        compiler_params=pltpu.CompilerParams(
            dimension_semantics=("arbitrary", "arbitrary"),
            vmem_limit_bytes=VMEM_LIMIT),
        name="peer_dense",
    )(hn, u, vt, n1r, e1r, rank2t, e2t, x1)


def _row_rms(x, width):
    return x * lax.rsqrt(jnp.sum(x * x, axis=-1, keepdims=True) * (1.0 / width) + EPS)


def _proj_kernel(x_ref, g_in_ref, w_in_ref, g_q_ref, w_uq_ref, g_kv_ref, w_uk_ref, w_uv_ref,
                 g_qh_ref, g_kh_ref, cos_ref, sin_ref, ones_ref, perm_ref, v_one_ref, dft_c_ref,
                 q_ref, kt_ref, v_ref, f_ref):
    o0, o1, o2 = Q_LORA, Q_LORA + KV_LORA, Q_LORA + KV_LORA + LANES
    h = (_row_rms(x_ref[0], D_MODEL) * g_in_ref[...]).astype(BF16)
    p = jnp.dot(h, w_in_ref[...], preferred_element_type=F32)
    c_q = (_row_rms(p[:, :o0], Q_LORA) * g_q_ref[...]).astype(BF16)
    c_kv = (_row_rms(p[:, o0:o1], KV_LORA) * g_kv_ref[...]).astype(BF16)
    k_rope = p[:, o1:o2]
    f_ref[0] = jnp.dot(p[:, o2:].astype(BF16), dft_c_ref[...],
                       preferred_element_type=F32).astype(BF16)
    q = jnp.dot(c_q, w_uq_ref[...], preferred_element_type=F32)
    k = jnp.dot(c_kv, w_uk_ref[...], preferred_element_type=F32)
    v = jnp.dot(c_kv, w_uv_ref[...], preferred_element_type=F32)
    v_ref[0] = (v + v_one_ref[...]).astype(BF16)

    cos, sin = cos_ref[...], sin_ref[...]

    def select_sum(val, mat):
        hi = val.astype(BF16)
        lo = (val - hi.astype(F32)).astype(BF16)
        return (jnp.dot(hi, mat, preferred_element_type=F32)
                + jnp.dot(lo, mat, preferred_element_type=F32))

    def head_norm_rope(slot, gain):
        ss = select_sum(slot * slot, ones_ref[...])
        y = slot * lax.rsqrt(ss * (1.0 / QK_HEAD) + EPS) * gain
        return y * cos + select_sum(y, perm_ref[...]) * sin

    q_slots, k_slots = [], []
    for hd in range(N_HEADS):
        hs = slice(hd * LANES, (hd + 1) * LANES)
        q_slots.append(head_norm_rope(q[:, hs], g_qh_ref[...]))
        k_slots.append(head_norm_rope(k[:, hs] + k_rope, g_kh_ref[...]))
    q_ref[0] = jnp.concatenate(q_slots, axis=1).astype(BF16)
    kt_ref[0] = jnp.concatenate(k_slots, axis=1).T.astype(BF16)


def _slot_cols(w, width, n_in):
    w = w.reshape(n_in, N_HEADS, width)
    return jnp.pad(w, ((0, 0), (0, 0), (0, LANES - width))).reshape(n_in, N_HEADS * LANES)


def _project(x, attn_norm_g, w_in, q_lat_g, w_uq, kv_lat_g, w_ukv, q_head_g, k_head_g, *, tm=512):
    b, s, _ = x.shape
    o0, o1, o2 = Q_LORA, Q_LORA + KV_LORA, Q_LORA + KV_LORA + QK_ROPE
    k_rope_cols = jnp.pad(w_in[:, o1:o2], ((0, 0), (QK_NOPE, LANES - QK_HEAD)))
    w_in_p = jnp.concatenate([w_in[:, :o1], k_rope_cols, w_in[:, o2:]], axis=1).astype(BF16)
    w_uq_p = _slot_cols(w_uq, QK_HEAD, Q_LORA).astype(BF16)
    w_ukv_h = w_ukv.reshape(KV_LORA, N_HEADS, QK_NOPE + V_HEAD)
    w_uk_p = _slot_cols(w_ukv_h[:, :, :QK_NOPE].reshape(KV_LORA, -1), QK_NOPE, KV_LORA).astype(BF16)
    w_uv_p = _slot_cols(w_ukv_h[:, :, QK_NOPE:].reshape(KV_LORA, -1), V_HEAD, KV_LORA).astype(BF16)
    pad_gain = lambda g: jnp.pad(g, (0, LANES - QK_HEAD)).reshape(1, LANES)
    g_qh = pad_gain(q_head_g) * (QK_HEAD ** -0.5 * LOG2_E)
    g_kh = pad_gain(k_head_g)
    half = QK_ROPE // 2
    freqs = 1.0 / (ROPE_THETA ** (jnp.arange(half, dtype=F32) / half))
    ang = jnp.arange(s, dtype=F32)[:, None] * freqs[None, :]
    cos, sin = jnp.cos(ang), jnp.sin(ang)
    zeros = lambda n: jnp.zeros((s, n), F32)
    cos_t = jnp.concatenate([jnp.ones((s, QK_NOPE), F32), cos, cos, zeros(LANES - QK_HEAD)], axis=1)
    sin_t = jnp.concatenate([zeros(QK_NOPE), -sin, sin, zeros(LANES - QK_HEAD)], axis=1)
    lane = np.arange(LANES)
    partner = np.where((lane >= QK_NOPE) & (lane < QK_NOPE + half), lane + half,
                       np.where((lane >= QK_NOPE + half) & (lane < QK_HEAD), lane - half, -1))
    perm = jnp.asarray(lane[:, None] == partner[None, :], BF16)
    ones = jnp.ones((LANES, LANES), BF16)
    v_one = jnp.tile((jnp.arange(LANES) == V_HEAD).astype(F32), N_HEADS).reshape(1, -1)
    row = lambda g: g.reshape(1, -1)

    const = lambda shape: pl.BlockSpec(shape, lambda bi, si: (0,) * len(shape))
    tab = pl.BlockSpec((tm, LANES), lambda bi, si: (si, 0))
    wide = N_HEADS * LANES
    return pl.pallas_call(
        _proj_kernel,
        grid=(b, s // tm),
        in_specs=[
            pl.BlockSpec((1, tm, D_MODEL), lambda bi, si: (bi, si, 0)),
            const((1, D_MODEL)), const(w_in_p.shape), const((1, Q_LORA)), const(w_uq_p.shape),
            const((1, KV_LORA)), const(w_uk_p.shape), const(w_uv_p.shape),
            const((1, LANES)), const((1, LANES)), tab, tab, const((LANES, LANES)),
            const((LANES, LANES)), const((1, wide)),
            const((FNET_W, 2 * FNET_W)),
        ],
        out_specs=[
            pl.BlockSpec((1, tm, wide), lambda bi, si: (bi, si, 0)),
            pl.BlockSpec((1, wide, tm), lambda bi, si: (bi, 0, si)),
            pl.BlockSpec((1, tm, wide), lambda bi, si: (bi, si, 0)),
            pl.BlockSpec((1, tm, 2 * FNET_W), lambda bi, si: (bi, si, 0)),
        ],
        out_shape=[
            jax.ShapeDtypeStruct((b, s, wide), BF16),
            jax.ShapeDtypeStruct((b, wide, s), BF16),
            jax.ShapeDtypeStruct((b, s, wide), BF16),
            jax.ShapeDtypeStruct((b, s, 2 * FNET_W), BF16),
        ],
        compiler_params=pltpu.CompilerParams(
            dimension_semantics=("arbitrary", "arbitrary"), vmem_limit_bytes=VMEM_LIMIT),
        name="input_projection",
    )(x, row(attn_norm_g), w_in_p, row(q_lat_g), w_uq_p, row(kv_lat_g), w_uk_p, w_uv_p,
      g_qh, g_kh, cos_t, sin_t, ones, perm, v_one, _channel_dft(s))


def _mix_kernel(x_ref, a_ref, fm_ref, g_a_ref, g_f_ref, w_a_ref, w_f_ref, g_ffn_ref,
                x1_ref, hn_ref):
    a = (_row_rms(a_ref[...], ATTN_W) * g_a_ref[...]).astype(BF16)
    fm = (_row_rms(fm_ref[...], FNET_W) * g_f_ref[...]).astype(BF16)
    x1 = (x_ref[...] + jnp.dot(a, w_a_ref[...], preferred_element_type=F32)
          + jnp.dot(fm, w_f_ref[...], preferred_element_type=F32))
    x1_ref[...] = x1
    hn_ref[...] = (_row_rms(x1, D_MODEL) * g_ffn_ref[...]).astype(BF16)


def _mix(x, a, fm, attn_out_g, fnet_out_g, w_out, ffn_norm_g, *, tm=512):
    t = x.shape[0]
    row = lambda g: g.reshape(1, -1)
    tok = lambda w: pl.BlockSpec((tm, w), lambda ti: (ti, 0))
    const = lambda shape: pl.BlockSpec(shape, lambda ti: (0, 0))
    return pl.pallas_call(
        _mix_kernel,
        grid=(t // tm,),
        in_specs=[tok(D_MODEL), tok(ATTN_W), tok(FNET_W), const((1, ATTN_W)), const((1, FNET_W)),
                  const((ATTN_W, D_MODEL)), const((FNET_W, D_MODEL)), const((1, D_MODEL))],
        out_specs=[tok(D_MODEL), tok(D_MODEL)],
        out_shape=[jax.ShapeDtypeStruct((t, D_MODEL), F32), jax.ShapeDtypeStruct((t, D_MODEL), BF16)],
        compiler_params=pltpu.CompilerParams(
            dimension_semantics=("arbitrary",), vmem_limit_bytes=VMEM_LIMIT),
        name="output_mix",
    )(x, a, fm, row(attn_out_g), row(fnet_out_g), w_out[:ATTN_W].astype(BF16),
      w_out[ATTN_W:].astype(BF16), row(ffn_norm_g))


FFT_S1 = 64


def _dft(n):
    ang = -2.0 * np.pi * np.outer(np.arange(n), np.arange(n)) / n
    return np.cos(ang), np.sin(ang)


def _channel_dft(s):
    c, si = _dft(FNET_CH)
    eye = np.eye(FNET_GROUPS)
    scale = (FNET_CH * s) ** -0.5
    return jnp.asarray(np.concatenate([np.kron(eye, c), np.kron(eye, si)], axis=1) * scale, BF16)


def _fft_stage1_kernel(x_ref, dr_ref, di_ref, twr_ref, twi_ref, o_ref):
    x = x_ref[0]
    yr = jnp.dot(dr_ref[...], x, preferred_element_type=F32)
    yi = jnp.dot(di_ref[...], x, preferred_element_type=F32)
    w = FNET_W
    for j in range(x.shape[1] // (2 * w)):
        re, im = slice(2 * j * w, (2 * j + 1) * w), slice((2 * j + 1) * w, (2 * j + 2) * w)
        ar = yr[:, re] - yi[:, im]
        ai = yr[:, im] + yi[:, re]
        tw = slice(j * LANES, (j + 1) * LANES)
        twr = jnp.concatenate([twr_ref[:, tw]] * (w // LANES), axis=1)
        twi = jnp.concatenate([twi_ref[:, tw]] * (w // LANES), axis=1)
        o_ref[0, :, re] = (ar * twr - ai * twi).astype(BF16)
        o_ref[0, :, im] = (ar * twi + ai * twr).astype(BF16)


def _fft_stage2_kernel(a_ref, m_ref, o_ref):
    w = FNET_W
    for j in range(a_ref.shape[1]):
        xa = a_ref[0, j]
        stacked = jnp.concatenate([xa[:, :w], xa[:, w:]], axis=0)
        o_ref[0, j] = jnp.dot(m_ref[...], stacked, preferred_element_type=F32)


def _fnet(fx, *, tn2=8, kb=8):
    b, s, wide = fx.shape
    s1, s2 = FFT_S1, s // FFT_S1
    d1r, d1i = _dft(s1)
    d2r, d2i = _dft(s2)
    ang = -2.0 * np.pi / s * jnp.outer(jnp.arange(s1, dtype=F32), jnp.arange(s2, dtype=F32))
    twr = jnp.repeat(jnp.cos(ang), LANES, axis=1)
    twi = jnp.repeat(jnp.sin(ang), LANES, axis=1)
    cols = tn2 * wide
    x2 = fx.reshape(b, s1, s2 * wide)
    a = pl.pallas_call(
        _fft_stage1_kernel,
        grid=(b, s2 // tn2),
        in_specs=[
            pl.BlockSpec((1, s1, cols), lambda bi, ci: (bi, 0, ci)),
            pl.BlockSpec((s1, s1), lambda bi, ci: (0, 0)),
            pl.BlockSpec((s1, s1), lambda bi, ci: (0, 0)),
            pl.BlockSpec((s1, tn2 * LANES), lambda bi, ci: (0, ci)),
            pl.BlockSpec((s1, tn2 * LANES), lambda bi, ci: (0, ci)),
        ],
        out_specs=pl.BlockSpec((1, s1, cols), lambda bi, ci: (bi, 0, ci)),
        out_shape=jax.ShapeDtypeStruct((b, s1, s2 * wide), BF16),
        compiler_params=pltpu.CompilerParams(
            dimension_semantics=("arbitrary", "arbitrary"), vmem_limit_bytes=VMEM_LIMIT),
        name="fft_stage1",
    )(x2, jnp.asarray(d1r, BF16), jnp.asarray(d1i, BF16), twr, twi)
    a4 = a.reshape(b, s1, s2, wide)
    m2 = jnp.asarray(np.concatenate([d2r, -d2i], axis=1), BF16)
    y = pl.pallas_call(
        _fft_stage2_kernel,
        grid=(b, s1 // kb),
        in_specs=[
            pl.BlockSpec((1, kb, s2, wide), lambda bi, ki: (bi, ki, 0, 0)),
            pl.BlockSpec((s2, 2 * s2), lambda bi, ki: (0, 0)),
        ],
        out_specs=pl.BlockSpec((1, kb, s2, FNET_W), lambda bi, ki: (bi, ki, 0, 0)),
        out_shape=jax.ShapeDtypeStruct((b, s1, s2, FNET_W), F32),
        compiler_params=pltpu.CompilerParams(
            dimension_semantics=("arbitrary", "arbitrary"), vmem_limit_bytes=VMEM_LIMIT),
        name="fft_stage2",
    )(a4, m2)
    return y.transpose(0, 2, 1, 3).reshape(b, s, FNET_W)


def kernel(x_prompt, x_sample, attn_norm_g, w_in, q_lat_g, w_uq, kv_lat_g, w_ukv, q_head_g,
           k_head_g, attn_out_g, fnet_out_g, w_out, ffn_norm_g, peer_w_q, peer_sub_keys,
           peer_u, peer_v):
    l = 0
    keys = peer_sub_keys[l].reshape(2 * PEER_HEADS, PEER_NKEYS, PEER_HALF).astype(BF16)
    w_pq, u, v_tab = peer_w_q[l].astype(BF16), peer_u[l].astype(BF16), peer_v[l].astype(BF16)
    outs = []
    for x in (x_prompt, x_sample):
        q, kt, v, f = _project(x, attn_norm_g[l], w_in[l], q_lat_g[l], w_uq[l], kv_lat_g[l],
                               w_ukv[l], q_head_g[l], k_head_g[l])
        a = _flash_attention(q, kt, v).reshape(-1, ATTN_W)
        fm = _fnet(f).reshape(-1, FNET_W)
        x1, hn = _mix(x.reshape(-1, D_MODEL), a, fm, attn_out_g[l], fnet_out_g[l], w_out[l],
                      ffn_norm_g[l])
        rank2t, e2t, n1t, e1t = _peer_gate(hn, w_pq, keys)
        outs.append(_peer_dense(hn, u, v_tab, n1t, e1t, rank2t, e2t, x1).reshape(x.shape))
    return tuple(outs)
```

```python
import functools

import jax
import jax.numpy as jnp
import numpy as np
from jax import lax
from jax.experimental import pallas as pl
from jax.experimental.pallas import tpu as pltpu

D_MODEL = 1024
N_HEADS = 8
QK_NOPE = 64
QK_ROPE = 32
QK_HEAD = QK_NOPE + QK_ROPE
V_HEAD = 64
Q_LORA = 384
KV_LORA = 256
ATTN_W = N_HEADS * V_HEAD
ROPE_THETA = 10000.0
FNET_W = D_MODEL - ATTN_W
FNET_GROUPS = 4
FNET_CH = FNET_W // FNET_GROUPS
PEER_HEADS = 8
PEER_NKEYS = 128
PEER_EXPERTS = PEER_NKEYS * PEER_NKEYS
PEER_HALF = 128
PEER_TOPK = 16
EPS = 1e-6
LOG2_E = 1.4426950408889634

LANES = 128
BF16_ROWS = 16
GATE_GROUP = 4
VMEM_LIMIT = 48 * 1024 * 1024

F32 = jnp.float32
BF16 = jnp.bfloat16


def _flash_kernel(q_ref, kt_ref, v_ref, o_ref, m_scr, acc_scr, *, tk):
    seq = v_ref.shape[1]
    tq = q_ref.shape[1]
    nkv = seq // tk
    m_scr[...] = jnp.full(m_scr.shape, -jnp.inf, F32)
    acc_scr[...] = jnp.zeros(acc_scr.shape, F32)

    def body(j, carry):
        start = pl.multiple_of(j * tk, tk)
        heads = [slice(hh * LANES, (hh + 1) * LANES) for hh in range(2)]
        scores = [jnp.dot(q_ref[0, :, hs], kt_ref[0, hs, pl.ds(start, tk)],
                          preferred_element_type=F32) for hs in heads]
        for hh, hs in enumerate(heads):
            cols = [scores[hh][:, c * LANES:(c + 1) * LANES] for c in range(tk // LANES)]
            m_old = m_scr[hh]
            m_new = jnp.maximum(m_old, jnp.max(functools.reduce(jnp.maximum, cols),
                                               axis=1, keepdims=True))
            p = jnp.concatenate([jnp.exp2(c - m_new) for c in cols], axis=1).astype(BF16)
            pv = jnp.dot(p, v_ref[0, pl.ds(start, tk), hs], preferred_element_type=F32)
            acc_scr[hh] = jnp.exp2(m_old - m_new) * acc_scr[hh] + pv
            m_scr[hh] = m_new
        return carry

    lax.fori_loop(0, nkv, body, 0)
    outs = []
    for hh in range(2):
        acc = acc_scr[hh]
        outs.append(acc[:, :V_HEAD] / acc[:, V_HEAD:V_HEAD + 1])
    o_ref[0] = jnp.concatenate(outs, axis=1)


def _flash_attention(q, kt, v, *, tq=512, tk=512):
    b, s, _ = q.shape
    grid = (b, N_HEADS // 2, s // tq)
    return pl.pallas_call(
        functools.partial(_flash_kernel, tk=tk),
        grid=grid,
        in_specs=[
            pl.BlockSpec((1, tq, 2 * LANES), lambda bi, hp, qi: (bi, qi, hp)),
            pl.BlockSpec((1, 2 * LANES, s), lambda bi, hp, qi: (bi, hp, 0)),
            pl.BlockSpec((1, s, 2 * LANES), lambda bi, hp, qi: (bi, 0, hp)),
        ],
        out_specs=pl.BlockSpec((1, tq, LANES), lambda bi, hp, qi: (bi, qi, hp)),
        out_shape=jax.ShapeDtypeStruct((b, s, ATTN_W), F32),
        scratch_shapes=[pltpu.VMEM((2, tq, LANES), F32), pltpu.VMEM((2, tq, LANES), F32)],
        compiler_params=pltpu.CompilerParams(
            dimension_semantics=("arbitrary", "arbitrary", "arbitrary"),
            vmem_limit_bytes=VMEM_LIMIT),
        name="flash_attention",
    )(q, kt, v)


_CAND_PAIRS = tuple((a, b) for a in range(PEER_TOPK) for b in range(PEER_TOPK)
                    if (a + 1) * (b + 1) <= PEER_TOPK)
_NEG_INF = float("-inf")


def _tree_max(xs):
    xs = list(xs)
    while len(xs) > 1:
        nxt = [jnp.maximum(xs[i], xs[i + 1]) for i in range(0, len(xs) - 1, 2)]
        if len(xs) % 2:
            nxt.append(xs[-1])
        xs = nxt
    return xs[0]


def _tree_min(xs):
    xs = list(xs)
    while len(xs) > 1:
        nxt = [jnp.minimum(xs[i], xs[i + 1]) for i in range(0, len(xs) - 1, 2)]
        if len(xs) % 2:
            nxt.append(xs[-1])
        xs = nxt
    return xs[0]


def _sort_network(n):
    pairs = []
    p = 1
    while p < n:
        k = p
        while k >= 1:
            for j in range(k % p, n - k, 2 * k):
                for i in range(min(k, n - j - k)):
                    if (i + j) // (2 * p) == (i + j + k) // (2 * p):
                        pairs.append((i + j, i + j + k))
            k //= 2
        p *= 2
    return tuple(pairs)


_SUBLANES = 8
_SORT_ROWS = _sort_network(PEER_NKEYS // _SUBLANES)


def _top_values(scores):
    rows = [scores[r * _SUBLANES:(r + 1) * _SUBLANES, :] for r in range(PEER_NKEYS // _SUBLANES)]
    for a, b in _SORT_ROWS:
        rows[a], rows[b] = jnp.maximum(rows[a], rows[b]), jnp.minimum(rows[a], rows[b])
    vals = []
    for t in range(PEER_TOPK):
        m = jnp.max(rows[0], axis=0, keepdims=True)
        vals.append(m)
        if t + 1 < PEER_TOPK:
            hit = rows[0] == m
            for r in range(PEER_TOPK - 1 - t):
                rows[r] = jnp.where(hit, rows[r + 1], rows[r])
    return vals


def _gate_kernel(hn_ref, wq_ref, keys_ref, rank2_ref, e2_ref, n1_ref, e1_ref, s_scr, vals_scr):
    tm = hn_ref.shape[0]
    ncol = tm // LANES
    q = jnp.dot(hn_ref[...], wq_ref[...], preferred_element_type=F32).astype(BF16)
    for hc in range(2 * PEER_HEADS):
        s_scr[hc] = lax.dot_general(keys_ref[hc], q[:, hc * PEER_HALF:(hc + 1) * PEER_HALF],
                                    (((1,), (1,)), ((), ())), preferred_element_type=F32)

    for hc in range(2 * PEER_HEADS):
        h, c = divmod(hc, 2)

        def col_body(col, carry, hc=hc, h=h, c=c):
            cs = pl.ds(pl.multiple_of(col * LANES, LANES), LANES)
            cur = s_scr[hc, :, cs]
            vals = _top_values(cur)
            for r in range(PEER_TOPK):
                vals_scr[c, r, h:h + 1, cs] = vals[r]
            if c == 1:
                rank = jnp.zeros((PEER_NKEYS, LANES), F32)
                for r in range(PEER_TOPK):
                    rank = rank + jnp.where(cur < vals[r], 1.0, 0.0)
                rank2_ref[h, :, cs] = rank
            return carry

        lax.fori_loop(0, ncol, col_body, 0, unroll=True)

    def fin_body(col, carry):
        cs = pl.ds(pl.multiple_of(col * LANES, LANES), LANES)
        v1 = [vals_scr[0, a, :, cs] for a in range(PEER_TOPK)]
        v2 = [vals_scr[1, b, :, cs] for b in range(PEER_TOPK)]
        cands = [v1[a] + v2[b] for a, b in _CAND_PAIRS]
        top = cands[0]
        z = jnp.zeros_like(top)
        m = top
        for r in range(PEER_TOPK):
            m = _tree_max(cands)
            z = z + jnp.exp(m - top)
            if r + 1 < PEER_TOPK:
                cands = [jnp.where(cd == m, _NEG_INF, cd) for cd in cands]
        thr = m
        rz = 1.0 / z
        cnt = [sum(jnp.where(v1[a] + v2[b] >= thr, 1.0, 0.0) for aa, b in _CAND_PAIRS if aa == a)
               for a in range(PEER_TOPK)]
        bound = [_tree_min([jnp.where(cnt[a] >= float(k), v1[a], float("inf"))
                            for a in range(PEER_TOPK // k)])
                 for k in range(1, PEER_TOPK + 1)]
        for h in range(PEER_HEADS):
            s1 = s_scr[2 * h, :, cs]
            n1 = jnp.zeros((PEER_NKEYS, LANES), F32)
            for k in range(PEER_TOPK):
                n1 = n1 + jnp.where(s1 >= bound[k][h:h + 1, :], 1.0, 0.0)
            n1_ref[h, :, cs] = n1
            e1_ref[h, :, cs] = jnp.exp(s1 - v1[0][h:h + 1, :])
            s2 = s_scr[2 * h + 1, :, cs]
            e2_ref[h, :, cs] = jnp.exp(s2 - v2[0][h:h + 1, :]) * rz[h:h + 1, :]
        return carry

    lax.fori_loop(0, ncol, fin_body, 0)


def _peer_gate(hn, wq, keys, *, tm=256):
    t = hn.shape[0]
    out = jax.ShapeDtypeStruct((PEER_HEADS, PEER_NKEYS, t), F32)
    ospec = pl.BlockSpec((PEER_HEADS, PEER_NKEYS, tm), lambda ti: (0, 0, ti))
    return pl.pallas_call(
        _gate_kernel,
        grid=(t // tm,),
        in_specs=[
            pl.BlockSpec((tm, D_MODEL), lambda ti: (ti, 0)),
            pl.BlockSpec((D_MODEL, 2 * PEER_HEADS * PEER_HALF), lambda ti: (0, 0)),
            pl.BlockSpec((2 * PEER_HEADS, PEER_NKEYS, PEER_HALF), lambda ti: (0, 0, 0)),
        ],
        out_specs=[ospec, ospec, ospec, ospec],
        out_shape=[out, out, out, out],
        scratch_shapes=[pltpu.VMEM((2 * PEER_HEADS, PEER_NKEYS, tm), F32),
                        pltpu.VMEM((2, PEER_TOPK, PEER_HEADS, tm), F32)],
        compiler_params=pltpu.CompilerParams(
            dimension_semantics=("arbitrary",), vmem_limit_bytes=VMEM_LIMIT),
        name="peer_gate",
    )(hn, wq, keys)


def _gelu_tanh(x):
    k1 = float(np.sqrt(2.0 / np.pi))
    k2 = k1 * 0.044715
    hx = 0.5 * x
    return hx + hx * jnp.tanh(x * (k1 + k2 * (x * x)))


def _peer_kernel(hn_ref, u_ref, vt_ref, n1_ref, e1_ref, rank2_ref, e2_ref, x_ref,
                 y_ref, acc_ref, a0_scr, a1_scr, g0_scr, g1_scr, r2_scr, e2_scr, *, n_i1):
    eb = pl.program_id(1)
    tm = hn_ref.shape[0]
    rows = PEER_NKEYS // BF16_ROWS

    @pl.when(eb == 0)
    def _():
        acc_ref[...] = jnp.zeros_like(acc_ref)
        a1_scr[...] = jnp.zeros_like(a1_scr)
        g1_scr[...] = jnp.zeros_like(g1_scr)
        r2_scr[...] = rank2_ref[...].astype(BF16)
        e2_scr[...] = e2_ref[...].astype(BF16)

    def step(a_prev, g_prev, a_cur, g_cur):
        def activate(c):
            cs = slice(c * LANES, (c + 1) * LANES)
            return _gelu_tanh(a_prev[:, cs]) * g_prev[:, cs]

        def build_gate(c):
            cs = slice(c * LANES, (c + 1) * LANES)
            for j0 in range(0, n_i1, GATE_GROUP):
                group = range(j0, j0 + GATE_GROUP)
                gs = {j: jnp.zeros((rows, BF16_ROWS, LANES), BF16) for j in group}
                for h in range(PEER_HEADS):
                    r2 = r2_scr[h, :, cs].reshape(rows, BF16_ROWS, LANES)
                    e2 = e2_scr[h, :, cs].reshape(rows, BF16_ROWS, LANES)
                    for j in group:
                        n1row = jnp.broadcast_to(n1_ref[h, 0, j:j + 1, cs], (BF16_ROWS, LANES)).astype(BF16)
                        e1row = jnp.broadcast_to(e1_ref[h, 0, j:j + 1, cs], (BF16_ROWS, LANES)).astype(BF16)
                        gs[j] = gs[j] + jnp.where(r2 < n1row[None], e2, jnp.zeros_like(e2)) * e1row[None]
                for j in group:
                    g_cur[j * PEER_NKEYS:(j + 1) * PEER_NKEYS, cs] = gs[j].reshape(PEER_NKEYS, LANES)

        ncol = tm // LANES
        a_new = lax.dot_general(u_ref[...], hn_ref[...], (((1,), (1,)), ((), ())),
                                preferred_element_type=F32)
        p = jnp.concatenate([activate(c) for c in range(ncol)], axis=1)
        out = jnp.dot(vt_ref[0], p, preferred_element_type=F32)
        for c in range(ncol):
            build_gate(c)
        acc_ref[...] += out
        a_cur[...] = a_new.astype(BF16)

    @pl.when(eb % 2 == 0)
    def _():
        step(a1_scr, g1_scr, a0_scr, g0_scr)

    @pl.when(eb % 2 == 1)
    def _():
        step(a0_scr, g0_scr, a1_scr, g1_scr)

    @pl.when(eb == pl.num_programs(1) - 1)
    def _():
        y_ref[...] = x_ref[...] + acc_ref[...].T


def _peer_dense(hn, u, v, n1t, e1t, rank2t, e2t, x1, *, tm=512, n_i1=8):
    t = hn.shape[0]
    te = n_i1 * PEER_NKEYS
    vt = v.reshape(PEER_EXPERTS // te, te, D_MODEL).transpose(0, 2, 1)
    nblk = PEER_NKEYS // n_i1
    n1r = n1t.reshape(PEER_HEADS, nblk, n_i1, t)
    e1r = e1t.reshape(PEER_HEADS, nblk, n_i1, t)
    grid = (t // tm, nblk + 1)
    this_blk = lambda eb: jnp.minimum(eb, nblk - 1)
    prev_blk = lambda eb: jnp.maximum(eb - 1, 0)
    return pl.pallas_call(
        functools.partial(_peer_kernel, n_i1=n_i1),
        grid=grid,
        in_specs=[
            pl.BlockSpec((tm, D_MODEL), lambda ti, eb: (ti, 0)),
            pl.BlockSpec((te, D_MODEL), lambda ti, eb: (this_blk(eb), 0)),
            pl.BlockSpec((1, D_MODEL, te), lambda ti, eb: (prev_blk(eb), 0, 0)),
            pl.BlockSpec((PEER_HEADS, 1, n_i1, tm), lambda ti, eb: (0, this_blk(eb), 0, ti)),
            pl.BlockSpec((PEER_HEADS, 1, n_i1, tm), lambda ti, eb: (0, this_blk(eb), 0, ti)),
            pl.BlockSpec((PEER_HEADS, PEER_NKEYS, tm), lambda ti, eb: (0, 0, ti)),
            pl.BlockSpec((PEER_HEADS, PEER_NKEYS, tm), lambda ti, eb: (0, 0, ti)),
            pl.BlockSpec((tm, D_MODEL), lambda ti, eb: (ti, 0)),
        ],
        out_specs=pl.BlockSpec((tm, D_MODEL), lambda ti, eb: (ti, 0)),
        out_shape=jax.ShapeDtypeStruct((t, D_MODEL), F32),
        scratch_shapes=[pltpu.VMEM((D_MODEL, tm), F32),
                        pltpu.VMEM((te, tm), BF16), pltpu.VMEM((te, tm), BF16),
                        pltpu.VMEM((te, tm), BF16), pltpu.VMEM((te, tm), BF16),
                        pltpu.VMEM((PEER_HEADS, PEER_NKEYS, tm), BF16),
                        pltpu.VMEM((PEER_HEADS, PEER_NKEYS, tm), BF16)],
        compiler_params=pltpu.CompilerParams(
            dimension_semantics=("arbitrary", "arbitrary"),
            vmem_limit_bytes=VMEM_LIMIT),
        name="peer_dense",
    )(hn, u, vt, n1r, e1r, rank2t, e2t, x1)


def _row_rms(x, width):
    return x * lax.rsqrt(jnp.sum(x * x, axis=-1, keepdims=True) * (1.0 / width) + EPS)


def _proj_kernel(x_ref, g_in_ref, w_in_ref, g_q_ref, w_uq_ref, g_kv_ref, w_uk_ref, w_uv_ref,
                 g_qh_ref, g_kh_ref, cos_ref, sin_ref, ones_ref, perm_ref, v_one_ref, dft_c_ref,
                 q_ref, kt_ref, v_ref, f_ref):
    o0, o1, o2 = Q_LORA, Q_LORA + KV_LORA, Q_LORA + KV_LORA + LANES
    h = (_row_rms(x_ref[0], D_MODEL) * g_in_ref[...]).astype(BF16)
    p = jnp.dot(h, w_in_ref[...], preferred_element_type=F32)
    c_q = (_row_rms(p[:, :o0], Q_LORA) * g_q_ref[...]).astype(BF16)
    c_kv = (_row_rms(p[:, o0:o1], KV_LORA) * g_kv_ref[...]).astype(BF16)
    k_rope = p[:, o1:o2]
    f_ref[0] = jnp.dot(p[:, o2:].astype(BF16), dft_c_ref[...],
                       preferred_element_type=F32).astype(BF16)
    q = jnp.dot(c_q, w_uq_ref[...], preferred_element_type=F32)
    k = jnp.dot(c_kv, w_uk_ref[...], preferred_element_type=F32)
    v = jnp.dot(c_kv, w_uv_ref[...], preferred_element_type=F32)
    v_ref[0] = (v + v_one_ref[...]).astype(BF16)

    cos, sin = cos_ref[...], sin_ref[...]

    def select_sum(val, mat):
        hi = val.astype(BF16)
        lo = (val - hi.astype(F32)).astype(BF16)
        return (jnp.dot(hi, mat, preferred_element_type=F32)
                + jnp.dot(lo, mat, preferred_element_type=F32))

    def head_norm_rope(slot, gain):
        ss = select_sum(slot * slot, ones_ref[...])
        y = slot * lax.rsqrt(ss * (1.0 / QK_HEAD) + EPS) * gain
        return y * cos + select_sum(y, perm_ref[...]) * sin

    q_slots, k_slots = [], []
    for hd in range(N_HEADS):
        hs = slice(hd * LANES, (hd + 1) * LANES)
        q_slots.append(head_norm_rope(q[:, hs], g_qh_ref[...]))
        k_slots.append(head_norm_rope(k[:, hs] + k_rope, g_kh_ref[...]))
    q_ref[0] = jnp.concatenate(q_slots, axis=1).astype(BF16)
    kt_ref[0] = jnp.concatenate(k_slots, axis=1).T.astype(BF16)


def _slot_cols(w, width, n_in):
    w = w.reshape(n_in, N_HEADS, width)
    return jnp.pad(w, ((0, 0), (0, 0), (0, LANES - width))).reshape(n_in, N_HEADS * LANES)


def _project(x, attn_norm_g, w_in, q_lat_g, w_uq, kv_lat_g, w_ukv, q_head_g, k_head_g, *, tm=512):
    b, s, _ = x.shape
    o0, o1, o2 = Q_LORA, Q_LORA + KV_LORA, Q_LORA + KV_LORA + QK_ROPE
    k_rope_cols = jnp.pad(w_in[:, o1:o2], ((0, 0), (QK_NOPE, LANES - QK_HEAD)))
    w_in_p = jnp.concatenate([w_in[:, :o1], k_rope_cols, w_in[:, o2:]], axis=1).astype(BF16)
    w_uq_p = _slot_cols(w_uq, QK_HEAD, Q_LORA).astype(BF16)
    w_ukv_h = w_ukv.reshape(KV_LORA, N_HEADS, QK_NOPE + V_HEAD)
    w_uk_p = _slot_cols(w_ukv_h[:, :, :QK_NOPE].reshape(KV_LORA, -1), QK_NOPE, KV_LORA).astype(BF16)
    w_uv_p = _slot_cols(w_ukv_h[:, :, QK_NOPE:].reshape(KV_LORA, -1), V_HEAD, KV_LORA).astype(BF16)
    pad_gain = lambda g: jnp.pad(g, (0, LANES - QK_HEAD)).reshape(1, LANES)
    g_qh = pad_gain(q_head_g) * (QK_HEAD ** -0.5 * LOG2_E)
    g_kh = pad_gain(k_head_g)
    half = QK_ROPE // 2
    freqs = 1.0 / (ROPE_THETA ** (jnp.arange(half, dtype=F32) / half))
    ang = jnp.arange(s, dtype=F32)[:, None] * freqs[None, :]
    cos, sin = jnp.cos(ang), jnp.sin(ang)
    zeros = lambda n: jnp.zeros((s, n), F32)
    cos_t = jnp.concatenate([jnp.ones((s, QK_NOPE), F32), cos, cos, zeros(LANES - QK_HEAD)], axis=1)
    sin_t = jnp.concatenate([zeros(QK_NOPE), -sin, sin, zeros(LANES - QK_HEAD)], axis=1)
    lane = np.arange(LANES)
    partner = np.where((lane >= QK_NOPE) & (lane < QK_NOPE + half), lane + half,
                       np.where((lane >= QK_NOPE + half) & (lane < QK_HEAD), lane - half, -1))
    perm = jnp.asarray(lane[:, None] == partner[None, :], BF16)
    ones = jnp.ones((LANES, LANES), BF16)
    v_one = jnp.tile((jnp.arange(LANES) == V_HEAD).astype(F32), N_HEADS).reshape(1, -1)
    row = lambda g: g.reshape(1, -1)

    const = lambda shape: pl.BlockSpec(shape, lambda bi, si: (0,) * len(shape))
    tab = pl.BlockSpec((tm, LANES), lambda bi, si: (si, 0))
    wide = N_HEADS * LANES
    return pl.pallas_call(
        _proj_kernel,
        grid=(b, s // tm),
        in_specs=[
            pl.BlockSpec((1, tm, D_MODEL), lambda bi, si: (bi, si, 0)),
            const((1, D_MODEL)), const(w_in_p.shape), const((1, Q_LORA)), const(w_uq_p.shape),
            const((1, KV_LORA)), const(w_uk_p.shape), const(w_uv_p.shape),
            const((1, LANES)), const((1, LANES)), tab, tab, const((LANES, LANES)),
            const((LANES, LANES)), const((1, wide)),
            const((FNET_W, 2 * FNET_W)),
        ],
        out_specs=[
            pl.BlockSpec((1, tm, wide), lambda bi, si: (bi, si, 0)),
            pl.BlockSpec((1, wide, tm), lambda bi, si: (bi, 0, si)),
            pl.BlockSpec((1, tm, wide), lambda bi, si: (bi, si, 0)),
            pl.BlockSpec((1, tm, 2 * FNET_W), lambda bi, si: (bi, si, 0)),
        ],
        out_shape=[
            jax.ShapeDtypeStruct((b, s, wide), BF16),
            jax.ShapeDtypeStruct((b, wide, s), BF16),
            jax.ShapeDtypeStruct((b, s, wide), BF16),
            jax.ShapeDtypeStruct((b, s, 2 * FNET_W), BF16),
        ],
        compiler_params=pltpu.CompilerParams(
            dimension_semantics=("arbitrary", "arbitrary"), vmem_limit_bytes=VMEM_LIMIT),
        name="input_projection",
    )(x, row(attn_norm_g), w_in_p, row(q_lat_g), w_uq_p, row(kv_lat_g), w_uk_p, w_uv_p,
      g_qh, g_kh, cos_t, sin_t, ones, perm, v_one, _channel_dft(s))


def _mix_kernel(x_ref, a_ref, fm_ref, g_a_ref, g_f_ref, w_a_ref, w_f_ref, g_ffn_ref,
                x1_ref, hn_ref):
    a = (_row_rms(a_ref[...], ATTN_W) * g_a_ref[...]).astype(BF16)
    fm = (_row_rms(fm_ref[...], FNET_W) * g_f_ref[...]).astype(BF16)
    x1 = (x_ref[...] + jnp.dot(a, w_a_ref[...], preferred_element_type=F32)
          + jnp.dot(fm, w_f_ref[...], preferred_element_type=F32))
    x1_ref[...] = x1
    hn_ref[...] = (_row_rms(x1, D_MODEL) * g_ffn_ref[...]).astype(BF16)


def _mix(x, a, fm, attn_out_g, fnet_out_g, w_out, ffn_norm_g, *, tm=512):
    t = x.shape[0]
    row = lambda g: g.reshape(1, -1)
    tok = lambda w: pl.BlockSpec((tm, w), lambda ti: (ti, 0))
    const = lambda shape: pl.BlockSpec(shape, lambda ti: (0, 0))
    return pl.pallas_call(
        _mix_kernel,
        grid=(t // tm,),
        in_specs=[tok(D_MODEL), tok(ATTN_W), tok(FNET_W), const((1, ATTN_W)), const((1, FNET_W)),
                  const((ATTN_W, D_MODEL)), const((FNET_W, D_MODEL)), const((1, D_MODEL))],
        out_specs=[tok(D_MODEL), tok(D_MODEL)],
        out_shape=[jax.ShapeDtypeStruct((t, D_MODEL), F32), jax.ShapeDtypeStruct((t, D_MODEL), BF16)],
        compiler_params=pltpu.CompilerParams(
            dimension_semantics=("arbitrary",), vmem_limit_bytes=VMEM_LIMIT),
        name="output_mix",
    )(x, a, fm, row(attn_out_g), row(fnet_out_g), w_out[:ATTN_W].astype(BF16),
      w_out[ATTN_W:].astype(BF16), row(ffn_norm_g))


FFT_S1 = 64


def _dft(n):
    ang = -2.0 * np.pi * np.outer(np.arange(n), np.arange(n)) / n
    return np.cos(ang), np.sin(ang)


def _channel_dft(s):
    c, si = _dft(FNET_CH)
    eye = np.eye(FNET_GROUPS)
    scale = (FNET_CH * s) ** -0.5
    return jnp.asarray(np.concatenate([np.kron(eye, c), np.kron(eye, si)], axis=1) * scale, BF16)


def _fft_stage1_kernel(x_ref, dr_ref, di_ref, twr_ref, twi_ref, o_ref):
    x = x_ref[0]
    yr = jnp.dot(dr_ref[...], x, preferred_element_type=F32)
    yi = jnp.dot(di_ref[...], x, preferred_element_type=F32)
    w = FNET_W
    for j in range(x.shape[1] // (2 * w)):
        re, im = slice(2 * j * w, (2 * j + 1) * w), slice((2 * j + 1) * w, (2 * j + 2) * w)
        ar = yr[:, re] - yi[:, im]
        ai = yr[:, im] + yi[:, re]
        tw = slice(j * LANES, (j + 1) * LANES)
        twr = jnp.concatenate([twr_ref[:, tw]] * (w // LANES), axis=1)
        twi = jnp.concatenate([twi_ref[:, tw]] * (w // LANES), axis=1)
        o_ref[0, :, re] = (ar * twr - ai * twi).astype(BF16)
        o_ref[0, :, im] = (ar * twi + ai * twr).astype(BF16)


def _fft_stage2_kernel(a_ref, m_ref, o_ref):
    w = FNET_W
    for j in range(a_ref.shape[1]):
        xa = a_ref[0, j]
        stacked = jnp.concatenate([xa[:, :w], xa[:, w:]], axis=0)
        o_ref[0, j] = jnp.dot(m_ref[...], stacked, preferred_element_type=F32)


def _fnet(fx, *, tn2=8, kb=8):
    b, s, wide = fx.shape
    s1, s2 = FFT_S1, s // FFT_S1
    d1r, d1i = _dft(s1)
    d2r, d2i = _dft(s2)
    ang = -2.0 * np.pi / s * jnp.outer(jnp.arange(s1, dtype=F32), jnp.arange(s2, dtype=F32))
    twr = jnp.repeat(jnp.cos(ang), LANES, axis=1)
    twi = jnp.repeat(jnp.sin(ang), LANES, axis=1)
    cols = tn2 * wide
    x2 = fx.reshape(b, s1, s2 * wide)
    a = pl.pallas_call(
        _fft_stage1_kernel,
        grid=(b, s2 // tn2),
        in_specs=[
            pl.BlockSpec((1, s1, cols), lambda bi, ci: (bi, 0, ci)),
            pl.BlockSpec((s1, s1), lambda bi, ci: (0, 0)),
            pl.BlockSpec((s1, s1), lambda bi, ci: (0, 0)),
            pl.BlockSpec((s1, tn2 * LANES), lambda bi, ci: (0, ci)),
            pl.BlockSpec((s1, tn2 * LANES), lambda bi, ci: (0, ci)),
        ],
        out_specs=pl.BlockSpec((1, s1, cols), lambda bi, ci: (bi, 0, ci)),
        out_shape=jax.ShapeDtypeStruct((b, s1, s2 * wide), BF16),
        compiler_params=pltpu.CompilerParams(
            dimension_semantics=("arbitrary", "arbitrary"), vmem_limit_bytes=VMEM_LIMIT),
        name="fft_stage1",
    )(x2, jnp.asarray(d1r, BF16), jnp.asarray(d1i, BF16), twr, twi)
    a4 = a.reshape(b, s1, s2, wide)
    m2 = jnp.asarray(np.concatenate([d2r, -d2i], axis=1), BF16)
    y = pl.pallas_call(
        _fft_stage2_kernel,
        grid=(b, s1 // kb),
        in_specs=[
            pl.BlockSpec((1, kb, s2, wide), lambda bi, ki: (bi, ki, 0, 0)),
            pl.BlockSpec((s2, 2 * s2), lambda bi, ki: (0, 0)),
        ],
        out_specs=pl.BlockSpec((1, kb, s2, FNET_W), lambda bi, ki: (bi, ki, 0, 0)),
        out_shape=jax.ShapeDtypeStruct((b, s1, s2, FNET_W), F32),
        compiler_params=pltpu.CompilerParams(
            dimension_semantics=("arbitrary", "arbitrary"), vmem_limit_bytes=VMEM_LIMIT),
        name="fft_stage2",
    )(a4, m2)
    return y.transpose(0, 2, 1, 3).reshape(b, s, FNET_W)


def kernel(x_prompt, x_sample, attn_norm_g, w_in, q_lat_g, w_uq, kv_lat_g, w_ukv, q_head_g,
           k_head_g, attn_out_g, fnet_out_g, w_out, ffn_norm_g, peer_w_q, peer_sub_keys,
           peer_u, peer_v):
    l = 0
    keys = peer_sub_keys[l].reshape(2 * PEER_HEADS, PEER_NKEYS, PEER_HALF).astype(BF16)
    w_pq, u, v_tab = peer_w_q[l].astype(BF16), peer_u[l].astype(BF16), peer_v[l].astype(BF16)
    outs = []
    for x in (x_prompt, x_sample):
        q, kt, v, f = _project(x, attn_norm_g[l], w_in[l], q_lat_g[l], w_uq[l], kv_lat_g[l],
                               w_ukv[l], q_head_g[l], k_head_g[l])
        a = _flash_attention(q, kt, v).reshape(-1, ATTN_W)
        fm = _fnet(f).reshape(-1, FNET_W)
        x1, hn = _mix(x.reshape(-1, D_MODEL), a, fm, attn_out_g[l], fnet_out_g[l], w_out[l],
                      ffn_norm_g[l])
        rank2t, e2t, n1t, e1t = _peer_gate(hn, w_pq, keys)
        outs.append(_peer_dense(hn, u, v_tab, n1t, e1t, rank2t, e2t, x1).reshape(x.shape))
    return tuple(outs)
```

```python
import functools

import jax
import jax.numpy as jnp
import numpy as np
from jax import lax
from jax.experimental import pallas as pl
from jax.experimental.pallas import tpu as pltpu

D_MODEL = 1024
N_HEADS = 8
QK_NOPE = 64
QK_ROPE = 32
QK_HEAD = QK_NOPE + QK_ROPE
V_HEAD = 64
Q_LORA = 384
KV_LORA = 256
ATTN_W = N_HEADS * V_HEAD
ROPE_THETA = 10000.0
FNET_W = D_MODEL - ATTN_W
FNET_GROUPS = 4
FNET_CH = FNET_W // FNET_GROUPS
PEER_HEADS = 8
PEER_NKEYS = 128
PEER_EXPERTS = PEER_NKEYS * PEER_NKEYS
PEER_HALF = 128
PEER_TOPK = 16
EPS = 1e-6
LOG2_E = 1.4426950408889634

LANES = 128
BF16_ROWS = 16
GATE_GROUP = 4
VMEM_LIMIT = 48 * 1024 * 1024

F32 = jnp.float32
BF16 = jnp.bfloat16


def _flash_kernel(q_ref, kt_ref, v_ref, o_ref, m_scr, acc_scr, *, tk):
    seq = v_ref.shape[1]
    tq = q_ref.shape[1]
    nkv = seq // tk
    m_scr[...] = jnp.full(m_scr.shape, -jnp.inf, F32)
    acc_scr[...] = jnp.zeros(acc_scr.shape, F32)

    def body(j, carry):
        start = pl.multiple_of(j * tk, tk)
        heads = [slice(hh * LANES, (hh + 1) * LANES) for hh in range(2)]
        scores = [jnp.dot(q_ref[0, :, hs], kt_ref[0, hs, pl.ds(start, tk)],
                          preferred_element_type=F32) for hs in heads]
        for hh, hs in enumerate(heads):
            cols = [scores[hh][:, c * LANES:(c + 1) * LANES] for c in range(tk // LANES)]
            m_old = m_scr[hh]
            m_new = jnp.maximum(m_old, jnp.max(functools.reduce(jnp.maximum, cols),
                                               axis=1, keepdims=True))
            p = jnp.concatenate([jnp.exp2(c - m_new) for c in cols], axis=1).astype(BF16)
            pv = jnp.dot(p, v_ref[0, pl.ds(start, tk), hs], preferred_element_type=F32)
            acc_scr[hh] = jnp.exp2(m_old - m_new) * acc_scr[hh] + pv
            m_scr[hh] = m_new
        return carry

    lax.fori_loop(0, nkv, body, 0)
    outs = []
    for hh in range(2):
        acc = acc_scr[hh]
        outs.append(acc[:, :V_HEAD] / acc[:, V_HEAD:V_HEAD + 1])
    o_ref[0] = jnp.concatenate(outs, axis=1)


def _flash_attention(q, kt, v, *, tq=512, tk=512):
    b, s, _ = q.shape
    grid = (b, N_HEADS // 2, s // tq)
    return pl.pallas_call(
        functools.partial(_flash_kernel, tk=tk),
        grid=grid,
        in_specs=[
            pl.BlockSpec((1, tq, 2 * LANES), lambda bi, hp, qi: (bi, qi, hp)),
            pl.BlockSpec((1, 2 * LANES, s), lambda bi, hp, qi: (bi, hp, 0)),
            pl.BlockSpec((1, s, 2 * LANES), lambda bi, hp, qi: (bi, 0, hp)),
        ],
        out_specs=pl.BlockSpec((1, tq, LANES), lambda bi, hp, qi: (bi, qi, hp)),
        out_shape=jax.ShapeDtypeStruct((b, s, ATTN_W), F32),
        scratch_shapes=[pltpu.VMEM((2, tq, LANES), F32), pltpu.VMEM((2, tq, LANES), F32)],
        compiler_params=pltpu.CompilerParams(
            dimension_semantics=("arbitrary", "arbitrary", "arbitrary"),
            vmem_limit_bytes=VMEM_LIMIT),
        name="flash_attention",
    )(q, kt, v)


_CAND_PAIRS = tuple((a, b) for a in range(PEER_TOPK) for b in range(PEER_TOPK)
                    if (a + 1) * (b + 1) <= PEER_TOPK)
_NEG_INF = float("-inf")


def _tree_max(xs):
    xs = list(xs)
    while len(xs) > 1:
        nxt = [jnp.maximum(xs[i], xs[i + 1]) for i in range(0, len(xs) - 1, 2)]
        if len(xs) % 2:
            nxt.append(xs[-1])
        xs = nxt
    return xs[0]


def _tree_min(xs):
    xs = list(xs)
    while len(xs) > 1:
        nxt = [jnp.minimum(xs[i], xs[i + 1]) for i in range(0, len(xs) - 1, 2)]
        if len(xs) % 2:
            nxt.append(xs[-1])
        xs = nxt
    return xs[0]


def _sort_network(n):
    pairs = []
    p = 1
    while p < n:
        k = p
        while k >= 1:
            for j in range(k % p, n - k, 2 * k):
                for i in range(min(k, n - j - k)):
                    if (i + j) // (2 * p) == (i + j + k) // (2 * p):
                        pairs.append((i + j, i + j + k))
            k //= 2
        p *= 2
    return tuple(pairs)


_SUBLANES = 8
_SORT_ROWS = _sort_network(PEER_NKEYS // _SUBLANES)


def _top_values(scores):
    rows = [scores[r * _SUBLANES:(r + 1) * _SUBLANES, :] for r in range(PEER_NKEYS // _SUBLANES)]
    for a, b in _SORT_ROWS:
        rows[a], rows[b] = jnp.maximum(rows[a], rows[b]), jnp.minimum(rows[a], rows[b])
    vals = []
    for t in range(PEER_TOPK):
        m = jnp.max(rows[0], axis=0, keepdims=True)
        vals.append(m)
        if t + 1 < PEER_TOPK:
            hit = rows[0] == m
            for r in range(PEER_TOPK - 1 - t):
                rows[r] = jnp.where(hit, rows[r + 1], rows[r])
    return vals


def _gate_kernel(hn_ref, wq_ref, keys_ref, rank2_ref, e2_ref, n1_ref, e1_ref, s_scr, vals_scr):
    tm = hn_ref.shape[0]
    ncol = tm // LANES
    q = jnp.dot(hn_ref[...], wq_ref[...], preferred_element_type=F32).astype(BF16)
    for hc in range(2 * PEER_HEADS):
        s_scr[hc] = lax.dot_general(keys_ref[hc], q[:, hc * PEER_HALF:(hc + 1) * PEER_HALF],
                                    (((1,), (1,)), ((), ())), preferred_element_type=F32)

    for hc in range(2 * PEER_HEADS):
        h, c = divmod(hc, 2)

        def col_body(col, carry, hc=hc, h=h, c=c):
            cs = pl.ds(pl.multiple_of(col * LANES, LANES), LANES)
            cur = s_scr[hc, :, cs]
            vals = _top_values(cur)
            for r in range(PEER_TOPK):
                vals_scr[c, r, h:h + 1, cs] = vals[r]
            if c == 1:
                rank = jnp.zeros((PEER_NKEYS, LANES), F32)
                for r in range(PEER_TOPK):
                    rank = rank + jnp.where(cur < vals[r], 1.0, 0.0)
                rank2_ref[h, :, cs] = rank
            return carry

        lax.fori_loop(0, ncol, col_body, 0, unroll=True)

    def fin_body(col, carry):
        cs = pl.ds(pl.multiple_of(col * LANES, LANES), LANES)
        v1 = [vals_scr[0, a, :, cs] for a in range(PEER_TOPK)]
        v2 = [vals_scr[1, b, :, cs] for b in range(PEER_TOPK)]
        cands = [v1[a] + v2[b] for a, b in _CAND_PAIRS]
        top = cands[0]
        z = jnp.zeros_like(top)
        m = top
        for r in range(PEER_TOPK):
            m = _tree_max(cands)
            z = z + jnp.exp(m - top)
            if r + 1 < PEER_TOPK:
                cands = [jnp.where(cd == m, _NEG_INF, cd) for cd in cands]
        thr = m
        rz = 1.0 / z
        cnt = [sum(jnp.where(v1[a] + v2[b] >= thr, 1.0, 0.0) for aa, b in _CAND_PAIRS if aa == a)
               for a in range(PEER_TOPK)]
        bound = [_tree_min([jnp.where(cnt[a] >= float(k), v1[a], float("inf"))
                            for a in range(PEER_TOPK // k)])
                 for k in range(1, PEER_TOPK + 1)]
        for h in range(PEER_HEADS):
            s1 = s_scr[2 * h, :, cs]
            n1 = jnp.zeros((PEER_NKEYS, LANES), F32)
            for k in range(PEER_TOPK):
                n1 = n1 + jnp.where(s1 >= bound[k][h:h + 1, :], 1.0, 0.0)
            n1_ref[h, :, cs] = n1
            e1_ref[h, :, cs] = jnp.exp(s1 - v1[0][h:h + 1, :])
            s2 = s_scr[2 * h + 1, :, cs]
            e2_ref[h, :, cs] = jnp.exp(s2 - v2[0][h:h + 1, :]) * rz[h:h + 1, :]
        return carry

    lax.fori_loop(0, ncol, fin_body, 0)


def _peer_gate(hn, wq, keys, *, tm=256):
    t = hn.shape[0]
    out = jax.ShapeDtypeStruct((PEER_HEADS, PEER_NKEYS, t), F32)
    ospec = pl.BlockSpec((PEER_HEADS, PEER_NKEYS, tm), lambda ti: (0, 0, ti))
    return pl.pallas_call(
        _gate_kernel,
        grid=(t // tm,),
        in_specs=[
            pl.BlockSpec((tm, D_MODEL), lambda ti: (ti, 0)),
            pl.BlockSpec((D_MODEL, 2 * PEER_HEADS * PEER_HALF), lambda ti: (0, 0)),
            pl.BlockSpec((2 * PEER_HEADS, PEER_NKEYS, PEER_HALF), lambda ti: (0, 0, 0)),
        ],
        out_specs=[ospec, ospec, ospec, ospec],
        out_shape=[out, out, out, out],
        scratch_shapes=[pltpu.VMEM((2 * PEER_HEADS, PEER_NKEYS, tm), F32),
                        pltpu.VMEM((2, PEER_TOPK, PEER_HEADS, tm), F32)],
        compiler_params=pltpu.CompilerParams(
            dimension_semantics=("arbitrary",), vmem_limit_bytes=VMEM_LIMIT),
        name="peer_gate",
    )(hn, wq, keys)


def _gelu_tanh(x):
    k1 = float(np.sqrt(2.0 / np.pi))
    k2 = k1 * 0.044715
    hx = 0.5 * x
    return hx + hx * jnp.tanh(x * (k1 + k2 * (x * x)))


def _peer_kernel(hn_ref, u_ref, vt_ref, n1_ref, e1_ref, rank2_ref, e2_ref, x_ref,
                 y_ref, acc_ref, a_scr, g_scr, r2_scr, e2_scr, *, n_i1):
    eb = pl.program_id(1)
    tm = hn_ref.shape[0]
    rows = PEER_NKEYS // BF16_ROWS

    @pl.when(eb == 0)
    def _():
        acc_ref[...] = jnp.zeros_like(acc_ref)
        a_scr[1] = jnp.zeros(a_scr.shape[1:], BF16)
        g_scr[1] = jnp.zeros(g_scr.shape[1:], BF16)
        r2_scr[...] = rank2_ref[...].astype(BF16)
        e2_scr[...] = e2_ref[...].astype(BF16)

    def step(a_prev, g_prev, a_cur, g_cur):
        def activate(c):
            cs = slice(c * LANES, (c + 1) * LANES)
            return _gelu_tanh(a_prev[:, cs]) * g_prev[:, cs]

        def build_gate(c):
            cs = slice(c * LANES, (c + 1) * LANES)
            for j0 in range(0, n_i1, GATE_GROUP):
                group = range(j0, j0 + GATE_GROUP)
                gs = {j: jnp.zeros((rows, BF16_ROWS, LANES), BF16) for j in group}
                for h in range(PEER_HEADS):
                    r2 = r2_scr[h, :, cs].reshape(rows, BF16_ROWS, LANES)
                    e2 = e2_scr[h, :, cs].reshape(rows, BF16_ROWS, LANES)
                    for j in group:
                        n1row = jnp.broadcast_to(n1_ref[h, 0, j:j + 1, cs], (BF16_ROWS, LANES)).astype(BF16)
                        e1row = jnp.broadcast_to(e1_ref[h, 0, j:j + 1, cs], (BF16_ROWS, LANES)).astype(BF16)
                        gs[j] = gs[j] + jnp.where(r2 < n1row[None], e2, jnp.zeros_like(e2)) * e1row[None]
                for j in group:
                    g_cur[j * PEER_NKEYS:(j + 1) * PEER_NKEYS, cs] = gs[j].reshape(PEER_NKEYS, LANES)

        ncol = tm // LANES
        a_new = lax.dot_general(u_ref[...], hn_ref[...], (((1,), (1,)), ((), ())),
                                preferred_element_type=F32)
        p = jnp.concatenate([activate(c) for c in range(ncol)], axis=1)
        out = jnp.dot(vt_ref[0], p, preferred_element_type=F32)
        for c in range(ncol):
            build_gate(c)
        acc_ref[...] += out
        a_cur[...] = a_new.astype(BF16)

    cur = eb % 2
    step(a_scr.at[1 - cur], g_scr.at[1 - cur], a_scr.at[cur], g_scr.at[cur])

    @pl.when(eb == pl.num_programs(1) - 1)
    def _():
        y_ref[...] = x_ref[...] + acc_ref[...].T


def _peer_dense(hn, u, v, n1t, e1t, rank2t, e2t, x1, *, tm=512, n_i1=8):
    t = hn.shape[0]
    te = n_i1 * PEER_NKEYS
    vt = v.reshape(PEER_EXPERTS // te, te, D_MODEL).transpose(0, 2, 1)
    nblk = PEER_NKEYS // n_i1
    n1r = n1t.reshape(PEER_HEADS, nblk, n_i1, t)
    e1r = e1t.reshape(PEER_HEADS, nblk, n_i1, t)
    grid = (t // tm, nblk + 1)
    this_blk = lambda eb: jnp.minimum(eb, nblk - 1)
    prev_blk = lambda eb: jnp.maximum(eb - 1, 0)
    return pl.pallas_call(
        functools.partial(_peer_kernel, n_i1=n_i1),
        grid=grid,
        in_specs=[
            pl.BlockSpec((tm, D_MODEL), lambda ti, eb: (ti, 0)),
            pl.BlockSpec((te, D_MODEL), lambda ti, eb: (this_blk(eb), 0)),
            pl.BlockSpec((1, D_MODEL, te), lambda ti, eb: (prev_blk(eb), 0, 0)),
            pl.BlockSpec((PEER_HEADS, 1, n_i1, tm), lambda ti, eb: (0, this_blk(eb), 0, ti)),
            pl.BlockSpec((PEER_HEADS, 1, n_i1, tm), lambda ti, eb: (0, this_blk(eb), 0, ti)),
            pl.BlockSpec((PEER_HEADS, PEER_NKEYS, tm), lambda ti, eb: (0, 0, ti)),
            pl.BlockSpec((PEER_HEADS, PEER_NKEYS, tm), lambda ti, eb: (0, 0, ti)),
            pl.BlockSpec((tm, D_MODEL), lambda ti, eb: (ti, 0)),
        ],
        out_specs=pl.BlockSpec((tm, D_MODEL), lambda ti, eb: (ti, 0)),
        out_shape=jax.ShapeDtypeStruct((t, D_MODEL), F32),
        scratch_shapes=[pltpu.VMEM((D_MODEL, tm), F32),
                        pltpu.VMEM((2, te, tm), BF16), pltpu.VMEM((2, te, tm), BF16),
                        pltpu.VMEM((PEER_HEADS, PEER_NKEYS, tm), BF16),
                        pltpu.VMEM((PEER_HEADS, PEER_NKEYS, tm), BF16)],
        compiler_params=pltpu.CompilerParams(
            dimension_semantics=("arbitrary", "arbitrary"),
            vmem_limit_bytes=VMEM_LIMIT),
        name="peer_dense",
    )(hn, u, vt, n1r, e1r, rank2t, e2t, x1)


def _row_rms(x, width):
    return x * lax.rsqrt(jnp.sum(x * x, axis=-1, keepdims=True) * (1.0 / width) + EPS)


def _proj_kernel(x_ref, g_in_ref, w_in_ref, g_q_ref, w_uq_ref, g_kv_ref, w_uk_ref, w_uv_ref,
                 g_qh_ref, g_kh_ref, cos_ref, sin_ref, ones_ref, perm_ref, v_one_ref, dft_c_ref,
                 q_ref, kt_ref, v_ref, f_ref):
    o0, o1, o2 = Q_LORA, Q_LORA + KV_LORA, Q_LORA + KV_LORA + LANES
    h = (_row_rms(x_ref[0], D_MODEL) * g_in_ref[...]).astype(BF16)
    p = jnp.dot(h, w_in_ref[...], preferred_element_type=F32)
    c_q = (_row_rms(p[:, :o0], Q_LORA) * g_q_ref[...]).astype(BF16)
    c_kv = (_row_rms(p[:, o0:o1], KV_LORA) * g_kv_ref[...]).astype(BF16)
    k_rope = p[:, o1:o2]
    f_ref[0] = jnp.dot(p[:, o2:].astype(BF16), dft_c_ref[...],
                       preferred_element_type=F32).astype(BF16)
    q = jnp.dot(c_q, w_uq_ref[...], preferred_element_type=F32)
    k = jnp.dot(c_kv, w_uk_ref[...], preferred_element_type=F32)
    v = jnp.dot(c_kv, w_uv_ref[...], preferred_element_type=F32)
    v_ref[0] = (v + v_one_ref[...]).astype(BF16)

    cos, sin = cos_ref[...], sin_ref[...]

    def select_sum(val, mat):
        hi = val.astype(BF16)
        lo = (val - hi.astype(F32)).astype(BF16)
        return (jnp.dot(hi, mat, preferred_element_type=F32)
                + jnp.dot(lo, mat, preferred_element_type=F32))

    def head_norm_rope(slot, gain):
        ss = select_sum(slot * slot, ones_ref[...])
        y = slot * lax.rsqrt(ss * (1.0 / QK_HEAD) + EPS) * gain
        return y * cos + select_sum(y, perm_ref[...]) * sin

    q_slots, k_slots = [], []
    for hd in range(N_HEADS):
        hs = slice(hd * LANES, (hd + 1) * LANES)
        q_slots.append(head_norm_rope(q[:, hs], g_qh_ref[...]))
        k_slots.append(head_norm_rope(k[:, hs] + k_rope, g_kh_ref[...]))
    q_ref[0] = jnp.concatenate(q_slots, axis=1).astype(BF16)
    kt_ref[0] = jnp.concatenate(k_slots, axis=1).T.astype(BF16)


def _slot_cols(w, width, n_in):
    w = w.reshape(n_in, N_HEADS, width)
    return jnp.pad(w, ((0, 0), (0, 0), (0, LANES - width))).reshape(n_in, N_HEADS * LANES)


def _project(x, attn_norm_g, w_in, q_lat_g, w_uq, kv_lat_g, w_ukv, q_head_g, k_head_g, *, tm=512):
    b, s, _ = x.shape
    o0, o1, o2 = Q_LORA, Q_LORA + KV_LORA, Q_LORA + KV_LORA + QK_ROPE
    k_rope_cols = jnp.pad(w_in[:, o1:o2], ((0, 0), (QK_NOPE, LANES - QK_HEAD)))
    w_in_p = jnp.concatenate([w_in[:, :o1], k_rope_cols, w_in[:, o2:]], axis=1).astype(BF16)
    w_uq_p = _slot_cols(w_uq, QK_HEAD, Q_LORA).astype(BF16)
    w_ukv_h = w_ukv.reshape(KV_LORA, N_HEADS, QK_NOPE + V_HEAD)
    w_uk_p = _slot_cols(w_ukv_h[:, :, :QK_NOPE].reshape(KV_LORA, -1), QK_NOPE, KV_LORA).astype(BF16)
    w_uv_p = _slot_cols(w_ukv_h[:, :, QK_NOPE:].reshape(KV_LORA, -1), V_HEAD, KV_LORA).astype(BF16)
    pad_gain = lambda g: jnp.pad(g, (0, LANES - QK_HEAD)).reshape(1, LANES)
    g_qh = pad_gain(q_head_g) * (QK_HEAD ** -0.5 * LOG2_E)
    g_kh = pad_gain(k_head_g)
    half = QK_ROPE // 2
    freqs = 1.0 / (ROPE_THETA ** (jnp.arange(half, dtype=F32) / half))
    ang = jnp.arange(s, dtype=F32)[:, None] * freqs[None, :]
    cos, sin = jnp.cos(ang), jnp.sin(ang)
    zeros = lambda n: jnp.zeros((s, n), F32)
    cos_t = jnp.concatenate([jnp.ones((s, QK_NOPE), F32), cos, cos, zeros(LANES - QK_HEAD)], axis=1)
    sin_t = jnp.concatenate([zeros(QK_NOPE), -sin, sin, zeros(LANES - QK_HEAD)], axis=1)
    lane = np.arange(LANES)
    partner = np.where((lane >= QK_NOPE) & (lane < QK_NOPE + half), lane + half,
                       np.where((lane >= QK_NOPE + half) & (lane < QK_HEAD), lane - half, -1))
    perm = jnp.asarray(lane[:, None] == partner[None, :], BF16)
    ones = jnp.ones((LANES, LANES), BF16)
    v_one = jnp.tile((jnp.arange(LANES) == V_HEAD).astype(F32), N_HEADS).reshape(1, -1)
    row = lambda g: g.reshape(1, -1)

    const = lambda shape: pl.BlockSpec(shape, lambda bi, si: (0,) * len(shape))
    tab = pl.BlockSpec((tm, LANES), lambda bi, si: (si, 0))
    wide = N_HEADS * LANES
    return pl.pallas_call(
        _proj_kernel,
        grid=(b, s // tm),
        in_specs=[
            pl.BlockSpec((1, tm, D_MODEL), lambda bi, si: (bi, si, 0)),
            const((1, D_MODEL)), const(w_in_p.shape), const((1, Q_LORA)), const(w_uq_p.shape),
            const((1, KV_LORA)), const(w_uk_p.shape), const(w_uv_p.shape),
            const((1, LANES)), const((1, LANES)), tab, tab, const((LANES, LANES)),
            const((LANES, LANES)), const((1, wide)),
            const((FNET_W, 2 * FNET_W)),
        ],
        out_specs=[
            pl.BlockSpec((1, tm, wide), lambda bi, si: (bi, si, 0)),
            pl.BlockSpec((1, wide, tm), lambda bi, si: (bi, 0, si)),
            pl.BlockSpec((1, tm, wide), lambda bi, si: (bi, si, 0)),
            pl.BlockSpec((1, tm, 2 * FNET_W), lambda bi, si: (bi, si, 0)),
        ],
        out_shape=[
            jax.ShapeDtypeStruct((b, s, wide), BF16),
            jax.ShapeDtypeStruct((b, wide, s), BF16),
            jax.ShapeDtypeStruct((b, s, wide), BF16),
            jax.ShapeDtypeStruct((b, s, 2 * FNET_W), BF16),
        ],
        compiler_params=pltpu.CompilerParams(
            dimension_semantics=("arbitrary", "arbitrary"), vmem_limit_bytes=VMEM_LIMIT),
        name="input_projection",
    )(x, row(attn_norm_g), w_in_p, row(q_lat_g), w_uq_p, row(kv_lat_g), w_uk_p, w_uv_p,
      g_qh, g_kh, cos_t, sin_t, ones, perm, v_one, _channel_dft(s))


def _mix_kernel(x_ref, a_ref, fm_ref, g_a_ref, g_f_ref, w_a_ref, w_f_ref, g_ffn_ref,
                x1_ref, hn_ref):
    a = (_row_rms(a_ref[...], ATTN_W) * g_a_ref[...]).astype(BF16)
    fm = (_row_rms(fm_ref[...], FNET_W) * g_f_ref[...]).astype(BF16)
    x1 = (x_ref[...] + jnp.dot(a, w_a_ref[...], preferred_element_type=F32)
          + jnp.dot(fm, w_f_ref[...], preferred_element_type=F32))
    x1_ref[...] = x1
    hn_ref[...] = (_row_rms(x1, D_MODEL) * g_ffn_ref[...]).astype(BF16)


def _mix(x, a, fm, attn_out_g, fnet_out_g, w_out, ffn_norm_g, *, tm=512):
    t = x.shape[0]
    row = lambda g: g.reshape(1, -1)
    tok = lambda w: pl.BlockSpec((tm, w), lambda ti: (ti, 0))
    const = lambda shape: pl.BlockSpec(shape, lambda ti: (0, 0))
    return pl.pallas_call(
        _mix_kernel,
        grid=(t // tm,),
        in_specs=[tok(D_MODEL), tok(ATTN_W), tok(FNET_W), const((1, ATTN_W)), const((1, FNET_W)),
                  const((ATTN_W, D_MODEL)), const((FNET_W, D_MODEL)), const((1, D_MODEL))],
        out_specs=[tok(D_MODEL), tok(D_MODEL)],
        out_shape=[jax.ShapeDtypeStruct((t, D_MODEL), F32), jax.ShapeDtypeStruct((t, D_MODEL), BF16)],
        compiler_params=pltpu.CompilerParams(
            dimension_semantics=("arbitrary",), vmem_limit_bytes=VMEM_LIMIT),
        name="output_mix",
    )(x, a, fm, row(attn_out_g), row(fnet_out_g), w_out[:ATTN_W].astype(BF16),
      w_out[ATTN_W:].astype(BF16), row(ffn_norm_g))


FFT_S1 = 64


def _dft(n):
    ang = -2.0 * np.pi * np.outer(np.arange(n), np.arange(n)) / n
    return np.cos(ang), np.sin(ang)


def _channel_dft(s):
    c, si = _dft(FNET_CH)
    eye = np.eye(FNET_GROUPS)
    scale = (FNET_CH * s) ** -0.5
    return jnp.asarray(np.concatenate([np.kron(eye, c), np.kron(eye, si)], axis=1) * scale, BF16)


def _fft_stage1_kernel(x_ref, dr_ref, di_ref, twr_ref, twi_ref, o_ref):
    x = x_ref[0]
    yr = jnp.dot(dr_ref[...], x, preferred_element_type=F32)
    yi = jnp.dot(di_ref[...], x, preferred_element_type=F32)
    w = FNET_W
    for j in range(x.shape[1] // (2 * w)):
        re, im = slice(2 * j * w, (2 * j + 1) * w), slice((2 * j + 1) * w, (2 * j + 2) * w)
        ar = yr[:, re] - yi[:, im]
        ai = yr[:, im] + yi[:, re]
        tw = slice(j * LANES, (j + 1) * LANES)
        twr = jnp.concatenate([twr_ref[:, tw]] * (w // LANES), axis=1)
        twi = jnp.concatenate([twi_ref[:, tw]] * (w // LANES), axis=1)
        o_ref[0, :, re] = (ar * twr - ai * twi).astype(BF16)
        o_ref[0, :, im] = (ar * twi + ai * twr).astype(BF16)


def _fft_stage2_kernel(a_ref, m_ref, o_ref):
    w = FNET_W
    for j in range(a_ref.shape[1]):
        xa = a_ref[0, j]
        stacked = jnp.concatenate([xa[:, :w], xa[:, w:]], axis=0)
        o_ref[0, j] = jnp.dot(m_ref[...], stacked, preferred_element_type=F32)


def _fnet(fx, *, tn2=8, kb=8):
    b, s, wide = fx.shape
    s1, s2 = FFT_S1, s // FFT_S1
    d1r, d1i = _dft(s1)
    d2r, d2i = _dft(s2)
    ang = -2.0 * np.pi / s * jnp.outer(jnp.arange(s1, dtype=F32), jnp.arange(s2, dtype=F32))
    twr = jnp.repeat(jnp.cos(ang), LANES, axis=1)
    twi = jnp.repeat(jnp.sin(ang), LANES, axis=1)
    cols = tn2 * wide
    x2 = fx.reshape(b, s1, s2 * wide)
    a = pl.pallas_call(
        _fft_stage1_kernel,
        grid=(b, s2 // tn2),
        in_specs=[
            pl.BlockSpec((1, s1, cols), lambda bi, ci: (bi, 0, ci)),
            pl.BlockSpec((s1, s1), lambda bi, ci: (0, 0)),
            pl.BlockSpec((s1, s1), lambda bi, ci: (0, 0)),
            pl.BlockSpec((s1, tn2 * LANES), lambda bi, ci: (0, ci)),
            pl.BlockSpec((s1, tn2 * LANES), lambda bi, ci: (0, ci)),
        ],
        out_specs=pl.BlockSpec((1, s1, cols), lambda bi, ci: (bi, 0, ci)),
        out_shape=jax.ShapeDtypeStruct((b, s1, s2 * wide), BF16),
        compiler_params=pltpu.CompilerParams(
            dimension_semantics=("arbitrary", "arbitrary"), vmem_limit_bytes=VMEM_LIMIT),
        name="fft_stage1",
    )(x2, jnp.asarray(d1r, BF16), jnp.asarray(d1i, BF16), twr, twi)
    a4 = a.reshape(b, s1, s2, wide)
    m2 = jnp.asarray(np.concatenate([d2r, -d2i], axis=1), BF16)
    y = pl.pallas_call(
        _fft_stage2_kernel,
        grid=(b, s1 // kb),
        in_specs=[
            pl.BlockSpec((1, kb, s2, wide), lambda bi, ki: (bi, ki, 0, 0)),
            pl.BlockSpec((s2, 2 * s2), lambda bi, ki: (0, 0)),
        ],
        out_specs=pl.BlockSpec((1, kb, s2, FNET_W), lambda bi, ki: (bi, ki, 0, 0)),
        out_shape=jax.ShapeDtypeStruct((b, s1, s2, FNET_W), F32),
        compiler_params=pltpu.CompilerParams(
            dimension_semantics=("arbitrary", "arbitrary"), vmem_limit_bytes=VMEM_LIMIT),
        name="fft_stage2",
    )(a4, m2)
    return y.transpose(0, 2, 1, 3).reshape(b, s, FNET_W)


def kernel(x_prompt, x_sample, attn_norm_g, w_in, q_lat_g, w_uq, kv_lat_g, w_ukv, q_head_g,
           k_head_g, attn_out_g, fnet_out_g, w_out, ffn_norm_g, peer_w_q, peer_sub_keys,
           peer_u, peer_v):
    l = 0
    keys = peer_sub_keys[l].reshape(2 * PEER_HEADS, PEER_NKEYS, PEER_HALF).astype(BF16)
    w_pq, u, v_tab = peer_w_q[l].astype(BF16), peer_u[l].astype(BF16), peer_v[l].astype(BF16)
    outs = []
    for x in (x_prompt, x_sample):
        q, kt, v, f = _project(x, attn_norm_g[l], w_in[l], q_lat_g[l], w_uq[l], kv_lat_g[l],
                               w_ukv[l], q_head_g[l], k_head_g[l])
        a = _flash_attention(q, kt, v).reshape(-1, ATTN_W)
        fm = _fnet(f).reshape(-1, FNET_W)
        x1, hn = _mix(x.reshape(-1, D_MODEL), a, fm, attn_out_g[l], fnet_out_g[l], w_out[l],
                      ffn_norm_g[l])
        rank2t, e2t, n1t, e1t = _peer_gate(hn, w_pq, keys)
        outs.append(_peer_dense(hn, u, v_tab, n1t, e1t, rank2t, e2t, x1).reshape(x.shape))
    return tuple(outs)
```

```python
import functools

import jax
import jax.numpy as jnp
import numpy as np
from jax import lax
from jax.experimental import pallas as pl
from jax.experimental.pallas import tpu as pltpu

D_MODEL = 1024
N_HEADS = 8
QK_NOPE = 64
QK_ROPE = 32
QK_HEAD = QK_NOPE + QK_ROPE
V_HEAD = 64
Q_LORA = 384
KV_LORA = 256
ATTN_W = N_HEADS * V_HEAD
ROPE_THETA = 10000.0
FNET_W = D_MODEL - ATTN_W
FNET_GROUPS = 4
FNET_CH = FNET_W // FNET_GROUPS
PEER_HEADS = 8
PEER_NKEYS = 128
PEER_EXPERTS = PEER_NKEYS * PEER_NKEYS
PEER_HALF = 128
PEER_TOPK = 16
EPS = 1e-6
LOG2_E = 1.4426950408889634

LANES = 128
BF16_ROWS = 16
GATE_GROUP = 4
VMEM_LIMIT = 48 * 1024 * 1024

F32 = jnp.float32
BF16 = jnp.bfloat16


def _flash_kernel(q_ref, kt_ref, v_ref, o_ref, m_scr, acc_scr, *, tk):
    seq = v_ref.shape[1]
    tq = q_ref.shape[1]
    nkv = seq // tk
    m_scr[...] = jnp.full(m_scr.shape, -jnp.inf, F32)
    acc_scr[...] = jnp.zeros(acc_scr.shape, F32)

    def body(j, carry):
        start = pl.multiple_of(j * tk, tk)
        heads = [slice(hh * LANES, (hh + 1) * LANES) for hh in range(2)]
        scores = [jnp.dot(q_ref[0, :, hs], kt_ref[0, hs, pl.ds(start, tk)],
                          preferred_element_type=F32) for hs in heads]
        for hh, hs in enumerate(heads):
            cols = [scores[hh][:, c * LANES:(c + 1) * LANES] for c in range(tk // LANES)]
            m_old = m_scr[hh]
            m_new = jnp.maximum(m_old, jnp.max(functools.reduce(jnp.maximum, cols),
                                               axis=1, keepdims=True))
            p = jnp.concatenate([jnp.exp2(c - m_new) for c in cols], axis=1).astype(BF16)
            pv = jnp.dot(p, v_ref[0, pl.ds(start, tk), hs], preferred_element_type=F32)
            acc_scr[hh] = jnp.exp2(m_old - m_new) * acc_scr[hh] + pv
            m_scr[hh] = m_new
        return carry

    lax.fori_loop(0, nkv, body, 0)
    outs = []
    for hh in range(2):
        acc = acc_scr[hh]
        outs.append(acc[:, :V_HEAD] / acc[:, V_HEAD:V_HEAD + 1])
    o_ref[0] = jnp.concatenate(outs, axis=1)


def _flash_attention(q, kt, v, *, tq=512, tk=1024):
    b, s, _ = q.shape
    grid = (b, N_HEADS // 2, s // tq)
    return pl.pallas_call(
        functools.partial(_flash_kernel, tk=tk),
        grid=grid,
        in_specs=[
            pl.BlockSpec((1, tq, 2 * LANES), lambda bi, hp, qi: (bi, qi, hp)),
            pl.BlockSpec((1, 2 * LANES, s), lambda bi, hp, qi: (bi, hp, 0)),
            pl.BlockSpec((1, s, 2 * LANES), lambda bi, hp, qi: (bi, 0, hp)),
        ],
        out_specs=pl.BlockSpec((1, tq, LANES), lambda bi, hp, qi: (bi, qi, hp)),
        out_shape=jax.ShapeDtypeStruct((b, s, ATTN_W), F32),
        scratch_shapes=[pltpu.VMEM((2, tq, LANES), F32), pltpu.VMEM((2, tq, LANES), F32)],
        compiler_params=pltpu.CompilerParams(
            dimension_semantics=("arbitrary", "arbitrary", "arbitrary"),
            vmem_limit_bytes=VMEM_LIMIT),
        name="flash_attention",
    )(q, kt, v)


_CAND_PAIRS = tuple((a, b) for a in range(PEER_TOPK) for b in range(PEER_TOPK)
                    if (a + 1) * (b + 1) <= PEER_TOPK)
_NEG_INF = float("-inf")


def _tree_max(xs):
    xs = list(xs)
    while len(xs) > 1:
        nxt = [jnp.maximum(xs[i], xs[i + 1]) for i in range(0, len(xs) - 1, 2)]
        if len(xs) % 2:
            nxt.append(xs[-1])
        xs = nxt
    return xs[0]


def _tree_min(xs):
    xs = list(xs)
    while len(xs) > 1:
        nxt = [jnp.minimum(xs[i], xs[i + 1]) for i in range(0, len(xs) - 1, 2)]
        if len(xs) % 2:
            nxt.append(xs[-1])
        xs = nxt
    return xs[0]


def _sort_network(n):
    pairs = []
    p = 1
    while p < n:
        k = p
        while k >= 1:
            for j in range(k % p, n - k, 2 * k):
                for i in range(min(k, n - j - k)):
                    if (i + j) // (2 * p) == (i + j + k) // (2 * p):
                        pairs.append((i + j, i + j + k))
            k //= 2
        p *= 2
    return tuple(pairs)


_SUBLANES = 8
_SORT_ROWS = _sort_network(PEER_NKEYS // _SUBLANES)


def _top_values(scores):
    rows = [scores[r * _SUBLANES:(r + 1) * _SUBLANES, :] for r in range(PEER_NKEYS // _SUBLANES)]
    for a, b in _SORT_ROWS:
        rows[a], rows[b] = jnp.maximum(rows[a], rows[b]), jnp.minimum(rows[a], rows[b])
    vals = []
    for t in range(PEER_TOPK):
        m = jnp.max(rows[0], axis=0, keepdims=True)
        vals.append(m)
        if t + 1 < PEER_TOPK:
            hit = rows[0] == m
            for r in range(PEER_TOPK - 1 - t):
                rows[r] = jnp.where(hit, rows[r + 1], rows[r])
    return vals


def _gate_kernel(hn_ref, wq_ref, keys_ref, rank2_ref, e2_ref, n1_ref, e1_ref, s_scr, vals_scr):
    tm = hn_ref.shape[0]
    ncol = tm // LANES
    q = jnp.dot(hn_ref[...], wq_ref[...], preferred_element_type=F32).astype(BF16)
    for hc in range(2 * PEER_HEADS):
        s_scr[hc] = lax.dot_general(keys_ref[hc], q[:, hc * PEER_HALF:(hc + 1) * PEER_HALF],
                                    (((1,), (1,)), ((), ())), preferred_element_type=F32)

    for hc in range(2 * PEER_HEADS):
        h, c = divmod(hc, 2)

        def col_body(col, carry, hc=hc, h=h, c=c):
            cs = pl.ds(pl.multiple_of(col * LANES, LANES), LANES)
            cur = s_scr[hc, :, cs]
            vals = _top_values(cur)
            for r in range(PEER_TOPK):
                vals_scr[c, r, h:h + 1, cs] = vals[r]
            if c == 1:
                rank = jnp.zeros((PEER_NKEYS, LANES), F32)
                for r in range(PEER_TOPK):
                    rank = rank + jnp.where(cur < vals[r], 1.0, 0.0)
                rank2_ref[h, :, cs] = rank
            return carry

        lax.fori_loop(0, ncol, col_body, 0, unroll=True)

    def fin_body(col, carry):
        cs = pl.ds(pl.multiple_of(col * LANES, LANES), LANES)
        v1 = [vals_scr[0, a, :, cs] for a in range(PEER_TOPK)]
        v2 = [vals_scr[1, b, :, cs] for b in range(PEER_TOPK)]
        cands = [v1[a] + v2[b] for a, b in _CAND_PAIRS]
        top = cands[0]
        z = jnp.zeros_like(top)
        m = top
        for r in range(PEER_TOPK):
            m = _tree_max(cands)
            z = z + jnp.exp(m - top)
            if r + 1 < PEER_TOPK:
                cands = [jnp.where(cd == m, _NEG_INF, cd) for cd in cands]
        thr = m
        rz = 1.0 / z
        cnt = [sum(jnp.where(v1[a] + v2[b] >= thr, 1.0, 0.0) for aa, b in _CAND_PAIRS if aa == a)
               for a in range(PEER_TOPK)]
        bound = [_tree_min([jnp.where(cnt[a] >= float(k), v1[a], float("inf"))
                            for a in range(PEER_TOPK // k)])
                 for k in range(1, PEER_TOPK + 1)]
        for h in range(PEER_HEADS):
            s1 = s_scr[2 * h, :, cs]
            n1 = jnp.zeros((PEER_NKEYS, LANES), F32)
            for k in range(PEER_TOPK):
                n1 = n1 + jnp.where(s1 >= bound[k][h:h + 1, :], 1.0, 0.0)
            n1_ref[h, :, cs] = n1
            e1_ref[h, :, cs] = jnp.exp(s1 - v1[0][h:h + 1, :])
            s2 = s_scr[2 * h + 1, :, cs]
            e2_ref[h, :, cs] = jnp.exp(s2 - v2[0][h:h + 1, :]) * rz[h:h + 1, :]
        return carry

    lax.fori_loop(0, ncol, fin_body, 0)


def _peer_gate(hn, wq, keys, *, tm=256):
    t = hn.shape[0]
    out = jax.ShapeDtypeStruct((PEER_HEADS, PEER_NKEYS, t), F32)
    ospec = pl.BlockSpec((PEER_HEADS, PEER_NKEYS, tm), lambda ti: (0, 0, ti))
    return pl.pallas_call(
        _gate_kernel,
        grid=(t // tm,),
        in_specs=[
            pl.BlockSpec((tm, D_MODEL), lambda ti: (ti, 0)),
            pl.BlockSpec((D_MODEL, 2 * PEER_HEADS * PEER_HALF), lambda ti: (0, 0)),
            pl.BlockSpec((2 * PEER_HEADS, PEER_NKEYS, PEER_HALF), lambda ti: (0, 0, 0)),
        ],
        out_specs=[ospec, ospec, ospec, ospec],
        out_shape=[out, out, out, out],
        scratch_shapes=[pltpu.VMEM((2 * PEER_HEADS, PEER_NKEYS, tm), F32),
                        pltpu.VMEM((2, PEER_TOPK, PEER_HEADS, tm), F32)],
        compiler_params=pltpu.CompilerParams(
            dimension_semantics=("arbitrary",), vmem_limit_bytes=VMEM_LIMIT),
        name="peer_gate",
    )(hn, wq, keys)


def _gelu_tanh(x):
    k1 = float(np.sqrt(2.0 / np.pi))
    k2 = k1 * 0.044715
    hx = 0.5 * x
    return hx + hx * jnp.tanh(x * (k1 + k2 * (x * x)))


def _peer_kernel(hn_ref, u_ref, vt_ref, n1_ref, e1_ref, rank2_ref, e2_ref, x_ref,
                 y_ref, acc_ref, a0_scr, a1_scr, g0_scr, g1_scr, r2_scr, e2_scr, *, n_i1):
    eb = pl.program_id(1)
    tm = hn_ref.shape[0]
    rows = PEER_NKEYS // BF16_ROWS

    @pl.when(eb == 0)
    def _():
        acc_ref[...] = jnp.zeros_like(acc_ref)
        a1_scr[...] = jnp.zeros_like(a1_scr)
        g1_scr[...] = jnp.zeros_like(g1_scr)
        r2_scr[...] = rank2_ref[...].astype(BF16)
        e2_scr[...] = e2_ref[...].astype(BF16)

    def step(a_prev, g_prev, a_cur, g_cur):
        def activate(c):
            cs = slice(c * LANES, (c + 1) * LANES)
            return _gelu_tanh(a_prev[:, cs]) * g_prev[:, cs]

        def build_gate(c):
            cs = slice(c * LANES, (c + 1) * LANES)
            for j0 in range(0, n_i1, GATE_GROUP):
                group = range(j0, j0 + GATE_GROUP)
                gs = {j: jnp.zeros((rows, BF16_ROWS, LANES), BF16) for j in group}
                for h in range(PEER_HEADS):
                    r2 = r2_scr[h, :, cs].reshape(rows, BF16_ROWS, LANES)
                    e2 = e2_scr[h, :, cs].reshape(rows, BF16_ROWS, LANES)
                    for j in group:
                        n1row = jnp.broadcast_to(n1_ref[h, 0, j:j + 1, cs], (BF16_ROWS, LANES)).astype(BF16)
                        e1row = jnp.broadcast_to(e1_ref[h, 0, j:j + 1, cs], (BF16_ROWS, LANES)).astype(BF16)
                        gs[j] = gs[j] + jnp.where(r2 < n1row[None], e2, jnp.zeros_like(e2)) * e1row[None]
                for j in group:
                    g_cur[j * PEER_NKEYS:(j + 1) * PEER_NKEYS, cs] = gs[j].reshape(PEER_NKEYS, LANES)

        ncol = tm // LANES
        a_new = lax.dot_general(u_ref[...], hn_ref[...], (((1,), (1,)), ((), ())),
                                preferred_element_type=F32)
        p = jnp.concatenate([activate(c) for c in range(ncol)], axis=1)
        out = jnp.dot(vt_ref[0], p, preferred_element_type=F32)
        for c in range(ncol):
            build_gate(c)
        acc_ref[...] += out
        a_cur[...] = a_new.astype(BF16)

    @pl.when(eb % 2 == 0)
    def _():
        step(a1_scr, g1_scr, a0_scr, g0_scr)

    @pl.when(eb % 2 == 1)
    def _():
        step(a0_scr, g0_scr, a1_scr, g1_scr)

    @pl.when(eb == pl.num_programs(1) - 1)
    def _():
        y_ref[...] = x_ref[...] + acc_ref[...].T


def _peer_dense(hn, u, v, n1t, e1t, rank2t, e2t, x1, *, tm=512, n_i1=8):
    t = hn.shape[0]
    te = n_i1 * PEER_NKEYS
    vt = v.reshape(PEER_EXPERTS // te, te, D_MODEL).transpose(0, 2, 1)
    nblk = PEER_NKEYS // n_i1
    n1r = n1t.reshape(PEER_HEADS, nblk, n_i1, t)
    e1r = e1t.reshape(PEER_HEADS, nblk, n_i1, t)
    grid = (t // tm, nblk + 1)
    this_blk = lambda eb: jnp.minimum(eb, nblk - 1)
    prev_blk = lambda eb: jnp.maximum(eb - 1, 0)
    return pl.pallas_call(
        functools.partial(_peer_kernel, n_i1=n_i1),
        grid=grid,
        in_specs=[
            pl.BlockSpec((tm, D_MODEL), lambda ti, eb: (ti, 0)),
            pl.BlockSpec((te, D_MODEL), lambda ti, eb: (this_blk(eb), 0)),
            pl.BlockSpec((1, D_MODEL, te), lambda ti, eb: (prev_blk(eb), 0, 0)),
            pl.BlockSpec((PEER_HEADS, 1, n_i1, tm), lambda ti, eb: (0, this_blk(eb), 0, ti)),
            pl.BlockSpec((PEER_HEADS, 1, n_i1, tm), lambda ti, eb: (0, this_blk(eb), 0, ti)),
            pl.BlockSpec((PEER_HEADS, PEER_NKEYS, tm), lambda ti, eb: (0, 0, ti)),
            pl.BlockSpec((PEER_HEADS, PEER_NKEYS, tm), lambda ti, eb: (0, 0, ti)),
            pl.BlockSpec((tm, D_MODEL), lambda ti, eb: (ti, 0)),
        ],
        out_specs=pl.BlockSpec((tm, D_MODEL), lambda ti, eb: (ti, 0)),
        out_shape=jax.ShapeDtypeStruct((t, D_MODEL), F32),
        scratch_shapes=[pltpu.VMEM((D_MODEL, tm), F32),
                        pltpu.VMEM((te, tm), BF16), pltpu.VMEM((te, tm), BF16),
                        pltpu.VMEM((te, tm), BF16), pltpu.VMEM((te, tm), BF16),
                        pltpu.VMEM((PEER_HEADS, PEER_NKEYS, tm), BF16),
                        pltpu.VMEM((PEER_HEADS, PEER_NKEYS, tm), BF16)],
        compiler_params=pltpu.CompilerParams(
            dimension_semantics=("arbitrary", "arbitrary"),
            vmem_limit_bytes=VMEM_LIMIT),
        name="peer_dense",
    )(hn, u, vt, n1r, e1r, rank2t, e2t, x1)


def _row_rms(x, width):
    return x * lax.rsqrt(jnp.sum(x * x, axis=-1, keepdims=True) * (1.0 / width) + EPS)


def _proj_kernel(x_ref, g_in_ref, w_in_ref, g_q_ref, w_uq_ref, g_kv_ref, w_uk_ref, w_uv_ref,
                 g_qh_ref, g_kh_ref, cos_ref, sin_ref, ones_ref, perm_ref, v_one_ref, dft_c_ref,
                 q_ref, kt_ref, v_ref, f_ref):
    o0, o1, o2 = Q_LORA, Q_LORA + KV_LORA, Q_LORA + KV_LORA + LANES
    h = (_row_rms(x_ref[0], D_MODEL) * g_in_ref[...]).astype(BF16)
    p = jnp.dot(h, w_in_ref[...], preferred_element_type=F32)
    c_q = (_row_rms(p[:, :o0], Q_LORA) * g_q_ref[...]).astype(BF16)
    c_kv = (_row_rms(p[:, o0:o1], KV_LORA) * g_kv_ref[...]).astype(BF16)
    k_rope = p[:, o1:o2]
    f_ref[0] = jnp.dot(p[:, o2:].astype(BF16), dft_c_ref[...],
                       preferred_element_type=F32).astype(BF16)
    q = jnp.dot(c_q, w_uq_ref[...], preferred_element_type=F32)
    k = jnp.dot(c_kv, w_uk_ref[...], preferred_element_type=F32)
    v = jnp.dot(c_kv, w_uv_ref[...], preferred_element_type=F32)
    v_ref[0] = (v + v_one_ref[...]).astype(BF16)

    cos, sin = cos_ref[...], sin_ref[...]

    def select_sum(val, mat):
        return jnp.dot(val.astype(BF16), mat, preferred_element_type=F32)

    def head_norm_rope(slot, gain):
        ss = select_sum(slot * slot, ones_ref[...])
        y = slot * lax.rsqrt(ss * (1.0 / QK_HEAD) + EPS) * gain
        return y * cos + select_sum(y, perm_ref[...]) * sin

    q_slots, k_slots = [], []
    for hd in range(N_HEADS):
        hs = slice(hd * LANES, (hd + 1) * LANES)
        q_slots.append(head_norm_rope(q[:, hs], g_qh_ref[...]))
        k_slots.append(head_norm_rope(k[:, hs] + k_rope, g_kh_ref[...]))
    q_ref[0] = jnp.concatenate(q_slots, axis=1).astype(BF16)
    kt_ref[0] = jnp.concatenate(k_slots, axis=1).T.astype(BF16)


def _slot_cols(w, width, n_in):
    w = w.reshape(n_in, N_HEADS, width)
    return jnp.pad(w, ((0, 0), (0, 0), (0, LANES - width))).reshape(n_in, N_HEADS * LANES)


def _project(x, attn_norm_g, w_in, q_lat_g, w_uq, kv_lat_g, w_ukv, q_head_g, k_head_g, *, tm=512):
    b, s, _ = x.shape
    o0, o1, o2 = Q_LORA, Q_LORA + KV_LORA, Q_LORA + KV_LORA + QK_ROPE
    k_rope_cols = jnp.pad(w_in[:, o1:o2], ((0, 0), (QK_NOPE, LANES - QK_HEAD)))
    w_in_p = jnp.concatenate([w_in[:, :o1], k_rope_cols, w_in[:, o2:]], axis=1).astype(BF16)
    w_uq_p = _slot_cols(w_uq, QK_HEAD, Q_LORA).astype(BF16)
    w_ukv_h = w_ukv.reshape(KV_LORA, N_HEADS, QK_NOPE + V_HEAD)
    w_uk_p = _slot_cols(w_ukv_h[:, :, :QK_NOPE].reshape(KV_LORA, -1), QK_NOPE, KV_LORA).astype(BF16)
    w_uv_p = _slot_cols(w_ukv_h[:, :, QK_NOPE:].reshape(KV_LORA, -1), V_HEAD, KV_LORA).astype(BF16)
    pad_gain = lambda g: jnp.pad(g, (0, LANES - QK_HEAD)).reshape(1, LANES)
    g_qh = pad_gain(q_head_g) * (QK_HEAD ** -0.5 * LOG2_E)
    g_kh = pad_gain(k_head_g)
    half = QK_ROPE // 2
    freqs = 1.0 / (ROPE_THETA ** (jnp.arange(half, dtype=F32) / half))
    ang = jnp.arange(s, dtype=F32)[:, None] * freqs[None, :]
    cos, sin = jnp.cos(ang), jnp.sin(ang)
    zeros = lambda n: jnp.zeros((s, n), F32)
    cos_t = jnp.concatenate([jnp.ones((s, QK_NOPE), F32), cos, cos, zeros(LANES - QK_HEAD)], axis=1)
    sin_t = jnp.concatenate([zeros(QK_NOPE), -sin, sin, zeros(LANES - QK_HEAD)], axis=1)
    lane = np.arange(LANES)
    partner = np.where((lane >= QK_NOPE) & (lane < QK_NOPE + half), lane + half,
                       np.where((lane >= QK_NOPE + half) & (lane < QK_HEAD), lane - half, -1))
    perm = jnp.asarray(lane[:, None] == partner[None, :], BF16)
    ones = jnp.ones((LANES, LANES), BF16)
    v_one = jnp.tile((jnp.arange(LANES) == V_HEAD).astype(F32), N_HEADS).reshape(1, -1)
    row = lambda g: g.reshape(1, -1)

    const = lambda shape: pl.BlockSpec(shape, lambda bi, si: (0,) * len(shape))
    tab = pl.BlockSpec((tm, LANES), lambda bi, si: (si, 0))
    wide = N_HEADS * LANES
    return pl.pallas_call(
        _proj_kernel,
        grid=(b, s // tm),
        in_specs=[
            pl.BlockSpec((1, tm, D_MODEL), lambda bi, si: (bi, si, 0)),
            const((1, D_MODEL)), const(w_in_p.shape), const((1, Q_LORA)), const(w_uq_p.shape),
            const((1, KV_LORA)), const(w_uk_p.shape), const(w_uv_p.shape),
            const((1, LANES)), const((1, LANES)), tab, tab, const((LANES, LANES)),
            const((LANES, LANES)), const((1, wide)),
            const((FNET_W, 2 * FNET_W)),
        ],
        out_specs=[
            pl.BlockSpec((1, tm, wide), lambda bi, si: (bi, si, 0)),
            pl.BlockSpec((1, wide, tm), lambda bi, si: (bi, 0, si)),
            pl.BlockSpec((1, tm, wide), lambda bi, si: (bi, si, 0)),
            pl.BlockSpec((1, tm, 2 * FNET_W), lambda bi, si: (bi, si, 0)),
        ],
        out_shape=[
            jax.ShapeDtypeStruct((b, s, wide), BF16),
            jax.ShapeDtypeStruct((b, wide, s), BF16),
            jax.ShapeDtypeStruct((b, s, wide), BF16),
            jax.ShapeDtypeStruct((b, s, 2 * FNET_W), BF16),
        ],
        compiler_params=pltpu.CompilerParams(
            dimension_semantics=("arbitrary", "arbitrary"), vmem_limit_bytes=VMEM_LIMIT),
        name="input_projection",
    )(x, row(attn_norm_g), w_in_p, row(q_lat_g), w_uq_p, row(kv_lat_g), w_uk_p, w_uv_p,
      g_qh, g_kh, cos_t, sin_t, ones, perm, v_one, _channel_dft(s))


def _mix_kernel(x_ref, a_ref, fm_ref, g_a_ref, g_f_ref, w_a_ref, w_f_ref, g_ffn_ref,
                x1_ref, hn_ref):
    a = (_row_rms(a_ref[...], ATTN_W) * g_a_ref[...]).astype(BF16)
    fm = (_row_rms(fm_ref[...], FNET_W) * g_f_ref[...]).astype(BF16)
    x1 = (x_ref[...] + jnp.dot(a, w_a_ref[...], preferred_element_type=F32)
          + jnp.dot(fm, w_f_ref[...], preferred_element_type=F32))
    x1_ref[...] = x1
    hn_ref[...] = (_row_rms(x1, D_MODEL) * g_ffn_ref[...]).astype(BF16)


def _mix(x, a, fm, attn_out_g, fnet_out_g, w_out, ffn_norm_g, *, tm=512):
    t = x.shape[0]
    row = lambda g: g.reshape(1, -1)
    tok = lambda w: pl.BlockSpec((tm, w), lambda ti: (ti, 0))
    const = lambda shape: pl.BlockSpec(shape, lambda ti: (0, 0))
    return pl.pallas_call(
        _mix_kernel,
        grid=(t // tm,),
        in_specs=[tok(D_MODEL), tok(ATTN_W), tok(FNET_W), const((1, ATTN_W)), const((1, FNET_W)),
                  const((ATTN_W, D_MODEL)), const((FNET_W, D_MODEL)), const((1, D_MODEL))],
        out_specs=[tok(D_MODEL), tok(D_MODEL)],
        out_shape=[jax.ShapeDtypeStruct((t, D_MODEL), F32), jax.ShapeDtypeStruct((t, D_MODEL), BF16)],
        compiler_params=pltpu.CompilerParams(
            dimension_semantics=("arbitrary",), vmem_limit_bytes=VMEM_LIMIT),
        name="output_mix",
    )(x, a, fm, row(attn_out_g), row(fnet_out_g), w_out[:ATTN_W].astype(BF16),
      w_out[ATTN_W:].astype(BF16), row(ffn_norm_g))


FFT_S1 = 64


def _dft(n):
    ang = -2.0 * np.pi * np.outer(np.arange(n), np.arange(n)) / n
    return np.cos(ang), np.sin(ang)


def _channel_dft(s):
    c, si = _dft(FNET_CH)
    eye = np.eye(FNET_GROUPS)
    scale = (FNET_CH * s) ** -0.5
    return jnp.asarray(np.concatenate([np.kron(eye, c), np.kron(eye, si)], axis=1) * scale, BF16)


def _fft_stage1_kernel(x_ref, dr_ref, di_ref, twr_ref, twi_ref, o_ref):
    x = x_ref[0]
    yr = jnp.dot(dr_ref[...], x, preferred_element_type=F32)
    yi = jnp.dot(di_ref[...], x, preferred_element_type=F32)
    w = FNET_W
    for j in range(x.shape[1] // (2 * w)):
        re, im = slice(2 * j * w, (2 * j + 1) * w), slice((2 * j + 1) * w, (2 * j + 2) * w)
        ar = yr[:, re] - yi[:, im]
        ai = yr[:, im] + yi[:, re]
        tw = slice(j * LANES, (j + 1) * LANES)
        twr = jnp.concatenate([twr_ref[:, tw]] * (w // LANES), axis=1)
        twi = jnp.concatenate([twi_ref[:, tw]] * (w // LANES), axis=1)
        o_ref[0, :, re] = (ar * twr - ai * twi).astype(BF16)
        o_ref[0, :, im] = (ar * twi + ai * twr).astype(BF16)


def _fft_stage2_kernel(a_ref, m_ref, o_ref):
    w = FNET_W
    for j in range(a_ref.shape[1]):
        xa = a_ref[0, j]
        stacked = jnp.concatenate([xa[:, :w], xa[:, w:]], axis=0)
        o_ref[0, j] = jnp.dot(m_ref[...], stacked, preferred_element_type=F32)


def _fnet(fx, *, tn2=8, kb=8):
    b, s, wide = fx.shape
    s1, s2 = FFT_S1, s // FFT_S1
    d1r, d1i = _dft(s1)
    d2r, d2i = _dft(s2)
    ang = -2.0 * np.pi / s * jnp.outer(jnp.arange(s1, dtype=F32), jnp.arange(s2, dtype=F32))
    twr = jnp.repeat(jnp.cos(ang), LANES, axis=1)
    twi = jnp.repeat(jnp.sin(ang), LANES, axis=1)
    cols = tn2 * wide
    x2 = fx.reshape(b, s1, s2 * wide)
    a = pl.pallas_call(
        _fft_stage1_kernel,
        grid=(b, s2 // tn2),
        in_specs=[
            pl.BlockSpec((1, s1, cols), lambda bi, ci: (bi, 0, ci)),
            pl.BlockSpec((s1, s1), lambda bi, ci: (0, 0)),
            pl.BlockSpec((s1, s1), lambda bi, ci: (0, 0)),
            pl.BlockSpec((s1, tn2 * LANES), lambda bi, ci: (0, ci)),
            pl.BlockSpec((s1, tn2 * LANES), lambda bi, ci: (0, ci)),
        ],
        out_specs=pl.BlockSpec((1, s1, cols), lambda bi, ci: (bi, 0, ci)),
        out_shape=jax.ShapeDtypeStruct((b, s1, s2 * wide), BF16),
        compiler_params=pltpu.CompilerParams(
            dimension_semantics=("arbitrary", "arbitrary"), vmem_limit_bytes=VMEM_LIMIT),
        name="fft_stage1",
    )(x2, jnp.asarray(d1r, BF16), jnp.asarray(d1i, BF16), twr, twi)
    a4 = a.reshape(b, s1, s2, wide)
    m2 = jnp.asarray(np.concatenate([d2r, -d2i], axis=1), BF16)
    y = pl.pallas_call(
        _fft_stage2_kernel,
        grid=(b, s1 // kb),
        in_specs=[
            pl.BlockSpec((1, kb, s2, wide), lambda bi, ki: (bi, ki, 0, 0)),
            pl.BlockSpec((s2, 2 * s2), lambda bi, ki: (0, 0)),
        ],
        out_specs=pl.BlockSpec((1, kb, s2, FNET_W), lambda bi, ki: (bi, ki, 0, 0)),
        out_shape=jax.ShapeDtypeStruct((b, s1, s2, FNET_W), F32),
        compiler_params=pltpu.CompilerParams(
            dimension_semantics=("arbitrary", "arbitrary"), vmem_limit_bytes=VMEM_LIMIT),
        name="fft_stage2",
    )(a4, m2)
    return y.transpose(0, 2, 1, 3).reshape(b, s, FNET_W)


def kernel(x_prompt, x_sample, attn_norm_g, w_in, q_lat_g, w_uq, kv_lat_g, w_ukv, q_head_g,
           k_head_g, attn_out_g, fnet_out_g, w_out, ffn_norm_g, peer_w_q, peer_sub_keys,
           peer_u, peer_v):
    l = 0
    keys = peer_sub_keys[l].reshape(2 * PEER_HEADS, PEER_NKEYS, PEER_HALF).astype(BF16)
    w_pq, u, v_tab = peer_w_q[l].astype(BF16), peer_u[l].astype(BF16), peer_v[l].astype(BF16)
    outs = []
    for x in (x_prompt, x_sample):
        q, kt, v, f = _project(x, attn_norm_g[l], w_in[l], q_lat_g[l], w_uq[l], kv_lat_g[l],
                               w_ukv[l], q_head_g[l], k_head_g[l])
        a = _flash_attention(q, kt, v).reshape(-1, ATTN_W)
        fm = _fnet(f).reshape(-1, FNET_W)
        x1, hn = _mix(x.reshape(-1, D_MODEL), a, fm, attn_out_g[l], fnet_out_g[l], w_out[l],
                      ffn_norm_g[l])
        rank2t, e2t, n1t, e1t = _peer_gate(hn, w_pq, keys)
        outs.append(_peer_dense(hn, u, v_tab, n1t, e1t, rank2t, e2t, x1).reshape(x.shape))
    return tuple(outs)
```

```python
import functools

import jax
import jax.numpy as jnp
import numpy as np
from jax import lax
from jax.experimental import pallas as pl
from jax.experimental.pallas import tpu as pltpu

D_MODEL = 1024
N_HEADS = 8
QK_NOPE = 64
QK_ROPE = 32
QK_HEAD = QK_NOPE + QK_ROPE
V_HEAD = 64
Q_LORA = 384
KV_LORA = 256
ATTN_W = N_HEADS * V_HEAD
ROPE_THETA = 10000.0
FNET_W = D_MODEL - ATTN_W
FNET_GROUPS = 4
FNET_CH = FNET_W // FNET_GROUPS
PEER_HEADS = 8
PEER_NKEYS = 128
PEER_EXPERTS = PEER_NKEYS * PEER_NKEYS
PEER_HALF = 128
PEER_TOPK = 16
EPS = 1e-6
LOG2_E = 1.4426950408889634

LANES = 128
BF16_ROWS = 16
GATE_GROUP = 4
VMEM_LIMIT = 48 * 1024 * 1024

F32 = jnp.float32
BF16 = jnp.bfloat16


def _flash_kernel(q_ref, kt_ref, v_ref, o_ref, m_scr, acc_scr, *, tk):
    seq = v_ref.shape[1]
    tq = q_ref.shape[1]
    nkv = seq // tk
    m_scr[...] = jnp.full(m_scr.shape, -jnp.inf, F32)
    acc_scr[...] = jnp.zeros(acc_scr.shape, F32)

    def body(j, carry):
        start = pl.multiple_of(j * tk, tk)
        heads = [slice(hh * LANES, (hh + 1) * LANES) for hh in range(2)]
        scores = [jnp.dot(q_ref[0, :, hs], kt_ref[0, hs, pl.ds(start, tk)],
                          preferred_element_type=F32) for hs in heads]
        for hh, hs in enumerate(heads):
            cols = [scores[hh][:, c * LANES:(c + 1) * LANES] for c in range(tk // LANES)]
            m_old = m_scr[hh]
            m_new = jnp.maximum(m_old, jnp.max(functools.reduce(jnp.maximum, cols),
                                               axis=1, keepdims=True))
            p = jnp.concatenate([jnp.exp2(c - m_new) for c in cols], axis=1).astype(BF16)
            pv = jnp.dot(p, v_ref[0, pl.ds(start, tk), hs], preferred_element_type=F32)
            acc_scr[hh] = jnp.exp2(m_old - m_new) * acc_scr[hh] + pv
            m_scr[hh] = m_new
        return carry

    lax.fori_loop(0, nkv, body, 0)
    outs = []
    for hh in range(2):
        acc = acc_scr[hh]
        outs.append(acc[:, :V_HEAD] / acc[:, V_HEAD:V_HEAD + 1])
    o_ref[0] = jnp.concatenate(outs, axis=1)


def _flash_attention(q, kt, v, *, tq=512, tk=2048):
    b, s, _ = q.shape
    grid = (b, N_HEADS // 2, s // tq)
    return pl.pallas_call(
        functools.partial(_flash_kernel, tk=tk),
        grid=grid,
        in_specs=[
            pl.BlockSpec((1, tq, 2 * LANES), lambda bi, hp, qi: (bi, qi, hp)),
            pl.BlockSpec((1, 2 * LANES, s), lambda bi, hp, qi: (bi, hp, 0)),
            pl.BlockSpec((1, s, 2 * LANES), lambda bi, hp, qi: (bi, 0, hp)),
        ],
        out_specs=pl.BlockSpec((1, tq, LANES), lambda bi, hp, qi: (bi, qi, hp)),
        out_shape=jax.ShapeDtypeStruct((b, s, ATTN_W), F32),
        scratch_shapes=[pltpu.VMEM((2, tq, LANES), F32), pltpu.VMEM((2, tq, LANES), F32)],
        compiler_params=pltpu.CompilerParams(
            dimension_semantics=("arbitrary", "arbitrary", "arbitrary"),
            vmem_limit_bytes=VMEM_LIMIT),
        name="flash_attention",
    )(q, kt, v)


_CAND_PAIRS = tuple((a, b) for a in range(PEER_TOPK) for b in range(PEER_TOPK)
                    if (a + 1) * (b + 1) <= PEER_TOPK)
_NEG_INF = float("-inf")


def _tree_max(xs):
    xs = list(xs)
    while len(xs) > 1:
        nxt = [jnp.maximum(xs[i], xs[i + 1]) for i in range(0, len(xs) - 1, 2)]
        if len(xs) % 2:
            nxt.append(xs[-1])
        xs = nxt
    return xs[0]


def _tree_min(xs):
    xs = list(xs)
    while len(xs) > 1:
        nxt = [jnp.minimum(xs[i], xs[i + 1]) for i in range(0, len(xs) - 1, 2)]
        if len(xs) % 2:
            nxt.append(xs[-1])
        xs = nxt
    return xs[0]


def _sort_network(n):
    pairs = []
    p = 1
    while p < n:
        k = p
        while k >= 1:
            for j in range(k % p, n - k, 2 * k):
                for i in range(min(k, n - j - k)):
                    if (i + j) // (2 * p) == (i + j + k) // (2 * p):
                        pairs.append((i + j, i + j + k))
            k //= 2
        p *= 2
    return tuple(pairs)


_SUBLANES = 8
_SORT_ROWS = _sort_network(PEER_NKEYS // _SUBLANES)


def _top_values(scores):
    rows = [scores[r * _SUBLANES:(r + 1) * _SUBLANES, :] for r in range(PEER_NKEYS // _SUBLANES)]
    for a, b in _SORT_ROWS:
        rows[a], rows[b] = jnp.maximum(rows[a], rows[b]), jnp.minimum(rows[a], rows[b])
    vals = []
    for t in range(PEER_TOPK):
        m = jnp.max(rows[0], axis=0, keepdims=True)
        vals.append(m)
        if t + 1 < PEER_TOPK:
            hit = rows[0] == m
            for r in range(PEER_TOPK - 1 - t):
                rows[r] = jnp.where(hit, rows[r + 1], rows[r])
    return vals


def _gate_kernel(hn_ref, wq_ref, keys_ref, rank2_ref, e2_ref, n1_ref, e1_ref, s_scr, vals_scr):
    tm = hn_ref.shape[0]
    ncol = tm // LANES
    q = jnp.dot(hn_ref[...], wq_ref[...], preferred_element_type=F32).astype(BF16)
    for hc in range(2 * PEER_HEADS):
        s_scr[hc] = lax.dot_general(keys_ref[hc], q[:, hc * PEER_HALF:(hc + 1) * PEER_HALF],
                                    (((1,), (1,)), ((), ())), preferred_element_type=F32)

    for hc in range(2 * PEER_HEADS):
        h, c = divmod(hc, 2)

        def col_body(col, carry, hc=hc, h=h, c=c):
            cs = pl.ds(pl.multiple_of(col * LANES, LANES), LANES)
            cur = s_scr[hc, :, cs]
            vals = _top_values(cur)
            for r in range(PEER_TOPK):
                vals_scr[c, r, h:h + 1, cs] = vals[r]
            if c == 1:
                rank = jnp.zeros((PEER_NKEYS, LANES), F32)
                for r in range(PEER_TOPK):
                    rank = rank + jnp.where(cur < vals[r], 1.0, 0.0)
                rank2_ref[h, :, cs] = rank
            return carry

        lax.fori_loop(0, ncol, col_body, 0, unroll=True)

    def fin_body(col, carry):
        cs = pl.ds(pl.multiple_of(col * LANES, LANES), LANES)
        v1 = [vals_scr[0, a, :, cs] for a in range(PEER_TOPK)]
        v2 = [vals_scr[1, b, :, cs] for b in range(PEER_TOPK)]
        cands = [v1[a] + v2[b] for a, b in _CAND_PAIRS]
        top = cands[0]
        z = jnp.zeros_like(top)
        m = top
        for r in range(PEER_TOPK):
            m = _tree_max(cands)
            z = z + jnp.exp(m - top)
            if r + 1 < PEER_TOPK:
                cands = [jnp.where(cd == m, _NEG_INF, cd) for cd in cands]
        thr = m
        rz = 1.0 / z
        cnt = [sum(jnp.where(v1[a] + v2[b] >= thr, 1.0, 0.0) for aa, b in _CAND_PAIRS if aa == a)
               for a in range(PEER_TOPK)]
        bound = [_tree_min([jnp.where(cnt[a] >= float(k), v1[a], float("inf"))
                            for a in range(PEER_TOPK // k)])
                 for k in range(1, PEER_TOPK + 1)]
        for h in range(PEER_HEADS):
            s1 = s_scr[2 * h, :, cs]
            n1 = jnp.zeros((PEER_NKEYS, LANES), F32)
            for k in range(PEER_TOPK):
                n1 = n1 + jnp.where(s1 >= bound[k][h:h + 1, :], 1.0, 0.0)
            n1_ref[h, :, cs] = n1
            e1_ref[h, :, cs] = jnp.exp(s1 - v1[0][h:h + 1, :])
            s2 = s_scr[2 * h + 1, :, cs]
            e2_ref[h, :, cs] = jnp.exp(s2 - v2[0][h:h + 1, :]) * rz[h:h + 1, :]
        return carry

    lax.fori_loop(0, ncol, fin_body, 0)


def _peer_gate(hn, wq, keys, *, tm=256):
    t = hn.shape[0]
    out = jax.ShapeDtypeStruct((PEER_HEADS, PEER_NKEYS, t), F32)
    ospec = pl.BlockSpec((PEER_HEADS, PEER_NKEYS, tm), lambda ti: (0, 0, ti))
    return pl.pallas_call(
        _gate_kernel,
        grid=(t // tm,),
        in_specs=[
            pl.BlockSpec((tm, D_MODEL), lambda ti: (ti, 0)),
            pl.BlockSpec((D_MODEL, 2 * PEER_HEADS * PEER_HALF), lambda ti: (0, 0)),
            pl.BlockSpec((2 * PEER_HEADS, PEER_NKEYS, PEER_HALF), lambda ti: (0, 0, 0)),
        ],
        out_specs=[ospec, ospec, ospec, ospec],
        out_shape=[out, out, out, out],
        scratch_shapes=[pltpu.VMEM((2 * PEER_HEADS, PEER_NKEYS, tm), F32),
                        pltpu.VMEM((2, PEER_TOPK, PEER_HEADS, tm), F32)],
        compiler_params=pltpu.CompilerParams(
            dimension_semantics=("arbitrary",), vmem_limit_bytes=VMEM_LIMIT),
        name="peer_gate",
    )(hn, wq, keys)


def _gelu_tanh(x):
    k1 = float(np.sqrt(2.0 / np.pi))
    k2 = k1 * 0.044715
    hx = 0.5 * x
    return hx + hx * jnp.tanh(x * (k1 + k2 * (x * x)))


def _peer_kernel(hn_ref, u_ref, vt_ref, n1_ref, e1_ref, rank2_ref, e2_ref, x_ref,
                 y_ref, acc_ref, a0_scr, a1_scr, g0_scr, g1_scr, r2_scr, e2_scr, *, n_i1):
    eb = pl.program_id(1)
    tm = hn_ref.shape[0]
    rows = PEER_NKEYS // BF16_ROWS

    @pl.when(eb == 0)
    def _():
        acc_ref[...] = jnp.zeros_like(acc_ref)
        a1_scr[...] = jnp.zeros_like(a1_scr)
        g1_scr[...] = jnp.zeros_like(g1_scr)
        r2_scr[...] = rank2_ref[...].astype(BF16)
        e2_scr[...] = e2_ref[...].astype(BF16)

    def step(a_prev, g_prev, a_cur, g_cur):
        def activate(c):
            cs = slice(c * LANES, (c + 1) * LANES)
            return _gelu_tanh(a_prev[:, cs]) * g_prev[:, cs]

        def build_gate(c):
            cs = slice(c * LANES, (c + 1) * LANES)
            for j0 in range(0, n_i1, GATE_GROUP):
                group = range(j0, j0 + GATE_GROUP)
                gs = {j: jnp.zeros((rows, BF16_ROWS, LANES), BF16) for j in group}
                for h in range(PEER_HEADS):
                    r2 = r2_scr[h, :, cs].reshape(rows, BF16_ROWS, LANES)
                    e2 = e2_scr[h, :, cs].reshape(rows, BF16_ROWS, LANES)
                    for j in group:
                        n1row = jnp.broadcast_to(n1_ref[h, 0, j:j + 1, cs], (BF16_ROWS, LANES)).astype(BF16)
                        e1row = jnp.broadcast_to(e1_ref[h, 0, j:j + 1, cs], (BF16_ROWS, LANES)).astype(BF16)
                        gs[j] = gs[j] + jnp.where(r2 < n1row[None], e2, jnp.zeros_like(e2)) * e1row[None]
                for j in group:
                    g_cur[j * PEER_NKEYS:(j + 1) * PEER_NKEYS, cs] = gs[j].reshape(PEER_NKEYS, LANES)

        ncol = tm // LANES
        a_cur[...] = lax.dot_general(u_ref[...], hn_ref[...], (((1,), (1,)), ((), ())),
                                     preferred_element_type=F32).astype(BF16)
        for c in range(ncol):
            build_gate(c)
        p = jnp.concatenate([activate(c) for c in range(ncol)], axis=1)
        acc_ref[...] += jnp.dot(vt_ref[0], p, preferred_element_type=F32)

    step(a0_scr, g0_scr, a0_scr, g0_scr)

    @pl.when(eb == pl.num_programs(1) - 1)
    def _():
        y_ref[...] = x_ref[...] + acc_ref[...].T


def _peer_dense(hn, u, v, n1t, e1t, rank2t, e2t, x1, *, tm=512, n_i1=8):
    t = hn.shape[0]
    te = n_i1 * PEER_NKEYS
    vt = v.reshape(PEER_EXPERTS // te, te, D_MODEL).transpose(0, 2, 1)
    nblk = PEER_NKEYS // n_i1
    n1r = n1t.reshape(PEER_HEADS, nblk, n_i1, t)
    e1r = e1t.reshape(PEER_HEADS, nblk, n_i1, t)
    grid = (t // tm, nblk)
    this_blk = lambda eb: jnp.minimum(eb, nblk - 1)
    prev_blk = lambda eb: jnp.maximum(eb - 1, 0)
    return pl.pallas_call(
        functools.partial(_peer_kernel, n_i1=n_i1),
        grid=grid,
        in_specs=[
            pl.BlockSpec((tm, D_MODEL), lambda ti, eb: (ti, 0)),
            pl.BlockSpec((te, D_MODEL), lambda ti, eb: (this_blk(eb), 0)),
            pl.BlockSpec((1, D_MODEL, te), lambda ti, eb: (this_blk(eb), 0, 0)),
            pl.BlockSpec((PEER_HEADS, 1, n_i1, tm), lambda ti, eb: (0, this_blk(eb), 0, ti)),
            pl.BlockSpec((PEER_HEADS, 1, n_i1, tm), lambda ti, eb: (0, this_blk(eb), 0, ti)),
            pl.BlockSpec((PEER_HEADS, PEER_NKEYS, tm), lambda ti, eb: (0, 0, ti)),
            pl.BlockSpec((PEER_HEADS, PEER_NKEYS, tm), lambda ti, eb: (0, 0, ti)),
            pl.BlockSpec((tm, D_MODEL), lambda ti, eb: (ti, 0)),
        ],
        out_specs=pl.BlockSpec((tm, D_MODEL), lambda ti, eb: (ti, 0)),
        out_shape=jax.ShapeDtypeStruct((t, D_MODEL), F32),
        scratch_shapes=[pltpu.VMEM((D_MODEL, tm), F32),
                        pltpu.VMEM((te, tm), BF16), pltpu.VMEM((te, tm), BF16),
                        pltpu.VMEM((te, tm), BF16), pltpu.VMEM((te, tm), BF16),
                        pltpu.VMEM((PEER_HEADS, PEER_NKEYS, tm), BF16),
                        pltpu.VMEM((PEER_HEADS, PEER_NKEYS, tm), BF16)],
        compiler_params=pltpu.CompilerParams(
            dimension_semantics=("arbitrary", "arbitrary"),
            vmem_limit_bytes=VMEM_LIMIT),
        name="peer_dense",
    )(hn, u, vt, n1r, e1r, rank2t, e2t, x1)


def _row_rms(x, width):
    return x * lax.rsqrt(jnp.sum(x * x, axis=-1, keepdims=True) * (1.0 / width) + EPS)


def _proj_kernel(x_ref, g_in_ref, w_in_ref, g_q_ref, w_uq_ref, g_kv_ref, w_uk_ref, w_uv_ref,
                 g_qh_ref, g_kh_ref, cos_ref, sin_ref, ones_ref, perm_ref, v_one_ref, dft_c_ref,
                 q_ref, kt_ref, v_ref, f_ref):
    o0, o1, o2 = Q_LORA, Q_LORA + KV_LORA, Q_LORA + KV_LORA + LANES
    h = (_row_rms(x_ref[0], D_MODEL) * g_in_ref[...]).astype(BF16)
    p = jnp.dot(h, w_in_ref[...], preferred_element_type=F32)
    c_q = (_row_rms(p[:, :o0], Q_LORA) * g_q_ref[...]).astype(BF16)
    c_kv = (_row_rms(p[:, o0:o1], KV_LORA) * g_kv_ref[...]).astype(BF16)
    k_rope = p[:, o1:o2]
    f_ref[0] = jnp.dot(p[:, o2:].astype(BF16), dft_c_ref[...],
                       preferred_element_type=F32).astype(BF16)
    q = jnp.dot(c_q, w_uq_ref[...], preferred_element_type=F32)
    k = jnp.dot(c_kv, w_uk_ref[...], preferred_element_type=F32)
    v = jnp.dot(c_kv, w_uv_ref[...], preferred_element_type=F32)
    v_ref[0] = (v + v_one_ref[...]).astype(BF16)

    cos, sin = cos_ref[...], sin_ref[...]

    def select_sum(val, mat):
        return jnp.dot(val.astype(BF16), mat, preferred_element_type=F32)

    def head_norm_rope(slot, gain):
        ss = select_sum(slot * slot, ones_ref[...])
        y = slot * lax.rsqrt(ss * (1.0 / QK_HEAD) + EPS) * gain
        return y * cos + select_sum(y, perm_ref[...]) * sin

    q_slots, k_slots = [], []
    for hd in range(N_HEADS):
        hs = slice(hd * LANES, (hd + 1) * LANES)
        q_slots.append(head_norm_rope(q[:, hs], g_qh_ref[...]))
        k_slots.append(head_norm_rope(k[:, hs] + k_rope, g_kh_ref[...]))
    q_ref[0] = jnp.concatenate(q_slots, axis=1).astype(BF16)
    kt_ref[0] = jnp.concatenate(k_slots, axis=1).T.astype(BF16)


def _slot_cols(w, width, n_in):
    w = w.reshape(n_in, N_HEADS, width)
    return jnp.pad(w, ((0, 0), (0, 0), (0, LANES - width))).reshape(n_in, N_HEADS * LANES)


def _project(x, attn_norm_g, w_in, q_lat_g, w_uq, kv_lat_g, w_ukv, q_head_g, k_head_g, *, tm=512):
    b, s, _ = x.shape
    o0, o1, o2 = Q_LORA, Q_LORA + KV_LORA, Q_LORA + KV_LORA + QK_ROPE
    k_rope_cols = jnp.pad(w_in[:, o1:o2], ((0, 0), (QK_NOPE, LANES - QK_HEAD)))
    w_in_p = jnp.concatenate([w_in[:, :o1], k_rope_cols, w_in[:, o2:]], axis=1).astype(BF16)
    w_uq_p = _slot_cols(w_uq, QK_HEAD, Q_LORA).astype(BF16)
    w_ukv_h = w_ukv.reshape(KV_LORA, N_HEADS, QK_NOPE + V_HEAD)
    w_uk_p = _slot_cols(w_ukv_h[:, :, :QK_NOPE].reshape(KV_LORA, -1), QK_NOPE, KV_LORA).astype(BF16)
    w_uv_p = _slot_cols(w_ukv_h[:, :, QK_NOPE:].reshape(KV_LORA, -1), V_HEAD, KV_LORA).astype(BF16)
    pad_gain = lambda g: jnp.pad(g, (0, LANES - QK_HEAD)).reshape(1, LANES)
    g_qh = pad_gain(q_head_g) * (QK_HEAD ** -0.5 * LOG2_E)
    g_kh = pad_gain(k_head_g)
    half = QK_ROPE // 2
    freqs = 1.0 / (ROPE_THETA ** (jnp.arange(half, dtype=F32) / half))
    ang = jnp.arange(s, dtype=F32)[:, None] * freqs[None, :]
    cos, sin = jnp.cos(ang), jnp.sin(ang)
    zeros = lambda n: jnp.zeros((s, n), F32)
    cos_t = jnp.concatenate([jnp.ones((s, QK_NOPE), F32), cos, cos, zeros(LANES - QK_HEAD)], axis=1)
    sin_t = jnp.concatenate([zeros(QK_NOPE), -sin, sin, zeros(LANES - QK_HEAD)], axis=1)
    lane = np.arange(LANES)
    partner = np.where((lane >= QK_NOPE) & (lane < QK_NOPE + half), lane + half,
                       np.where((lane >= QK_NOPE + half) & (lane < QK_HEAD), lane - half, -1))
    perm = jnp.asarray(lane[:, None] == partner[None, :], BF16)
    ones = jnp.ones((LANES, LANES), BF16)
    v_one = jnp.tile((jnp.arange(LANES) == V_HEAD).astype(F32), N_HEADS).reshape(1, -1)
    row = lambda g: g.reshape(1, -1)

    const = lambda shape: pl.BlockSpec(shape, lambda bi, si: (0,) * len(shape))
    tab = pl.BlockSpec((tm, LANES), lambda bi, si: (si, 0))
    wide = N_HEADS * LANES
    return pl.pallas_call(
        _proj_kernel,
        grid=(b, s // tm),
        in_specs=[
            pl.BlockSpec((1, tm, D_MODEL), lambda bi, si: (bi, si, 0)),
            const((1, D_MODEL)), const(w_in_p.shape), const((1, Q_LORA)), const(w_uq_p.shape),
            const((1, KV_LORA)), const(w_uk_p.shape), const(w_uv_p.shape),
            const((1, LANES)), const((1, LANES)), tab, tab, const((LANES, LANES)),
            const((LANES, LANES)), const((1, wide)),
            const((FNET_W, 2 * FNET_W)),
        ],
        out_specs=[
            pl.BlockSpec((1, tm, wide), lambda bi, si: (bi, si, 0)),
            pl.BlockSpec((1, wide, tm), lambda bi, si: (bi, 0, si)),
            pl.BlockSpec((1, tm, wide), lambda bi, si: (bi, si, 0)),
            pl.BlockSpec((1, tm, 2 * FNET_W), lambda bi, si: (bi, si, 0)),
        ],
        out_shape=[
            jax.ShapeDtypeStruct((b, s, wide), BF16),
            jax.ShapeDtypeStruct((b, wide, s), BF16),
            jax.ShapeDtypeStruct((b, s, wide), BF16),
            jax.ShapeDtypeStruct((b, s, 2 * FNET_W), BF16),
        ],
        compiler_params=pltpu.CompilerParams(
            dimension_semantics=("arbitrary", "arbitrary"), vmem_limit_bytes=VMEM_LIMIT),
        name="input_projection",
    )(x, row(attn_norm_g), w_in_p, row(q_lat_g), w_uq_p, row(kv_lat_g), w_uk_p, w_uv_p,
      g_qh, g_kh, cos_t, sin_t, ones, perm, v_one, _channel_dft(s))


def _mix_kernel(x_ref, a_ref, fm_ref, g_a_ref, g_f_ref, w_a_ref, w_f_ref, g_ffn_ref,
                x1_ref, hn_ref):
    a = (_row_rms(a_ref[...], ATTN_W) * g_a_ref[...]).astype(BF16)
    fm = (_row_rms(fm_ref[...], FNET_W) * g_f_ref[...]).astype(BF16)
    x1 = (x_ref[...] + jnp.dot(a, w_a_ref[...], preferred_element_type=F32)
          + jnp.dot(fm, w_f_ref[...], preferred_element_type=F32))
    x1_ref[...] = x1
    hn_ref[...] = (_row_rms(x1, D_MODEL) * g_ffn_ref[...]).astype(BF16)


def _mix(x, a, fm, attn_out_g, fnet_out_g, w_out, ffn_norm_g, *, tm=512):
    t = x.shape[0]
    row = lambda g: g.reshape(1, -1)
    tok = lambda w: pl.BlockSpec((tm, w), lambda ti: (ti, 0))
    const = lambda shape: pl.BlockSpec(shape, lambda ti: (0, 0))
    return pl.pallas_call(
        _mix_kernel,
        grid=(t // tm,),
        in_specs=[tok(D_MODEL), tok(ATTN_W), tok(FNET_W), const((1, ATTN_W)), const((1, FNET_W)),
                  const((ATTN_W, D_MODEL)), const((FNET_W, D_MODEL)), const((1, D_MODEL))],
        out_specs=[tok(D_MODEL), tok(D_MODEL)],
        out_shape=[jax.ShapeDtypeStruct((t, D_MODEL), F32), jax.ShapeDtypeStruct((t, D_MODEL), BF16)],
        compiler_params=pltpu.CompilerParams(
            dimension_semantics=("arbitrary",), vmem_limit_bytes=VMEM_LIMIT),
        name="output_mix",
    )(x, a, fm, row(attn_out_g), row(fnet_out_g), w_out[:ATTN_W].astype(BF16),
      w_out[ATTN_W:].astype(BF16), row(ffn_norm_g))


FFT_S1 = 64


def _dft(n):
    ang = -2.0 * np.pi * np.outer(np.arange(n), np.arange(n)) / n
    return np.cos(ang), np.sin(ang)


def _channel_dft(s):
    c, si = _dft(FNET_CH)
    eye = np.eye(FNET_GROUPS)
    scale = (FNET_CH * s) ** -0.5
    return jnp.asarray(np.concatenate([np.kron(eye, c), np.kron(eye, si)], axis=1) * scale, BF16)


def _fft_stage1_kernel(x_ref, dr_ref, di_ref, twr_ref, twi_ref, o_ref):
    x = x_ref[0]
    yr = jnp.dot(dr_ref[...], x, preferred_element_type=F32)
    yi = jnp.dot(di_ref[...], x, preferred_element_type=F32)
    w = FNET_W
    for j in range(x.shape[1] // (2 * w)):
        re, im = slice(2 * j * w, (2 * j + 1) * w), slice((2 * j + 1) * w, (2 * j + 2) * w)
        ar = yr[:, re] - yi[:, im]
        ai = yr[:, im] + yi[:, re]
        tw = slice(j * LANES, (j + 1) * LANES)
        twr = jnp.concatenate([twr_ref[:, tw]] * (w // LANES), axis=1)
        twi = jnp.concatenate([twi_ref[:, tw]] * (w // LANES), axis=1)
        o_ref[0, :, re] = (ar * twr - ai * twi).astype(BF16)
        o_ref[0, :, im] = (ar * twi + ai * twr).astype(BF16)


def _fft_stage2_kernel(a_ref, m_ref, o_ref):
    w = FNET_W
    for j in range(a_ref.shape[1]):
        xa = a_ref[0, j]
        stacked = jnp.concatenate([xa[:, :w], xa[:, w:]], axis=0)
        o_ref[0, j] = jnp.dot(m_ref[...], stacked, preferred_element_type=F32)


def _fnet(fx, *, tn2=8, kb=8):
    b, s, wide = fx.shape
    s1, s2 = FFT_S1, s // FFT_S1
    d1r, d1i = _dft(s1)
    d2r, d2i = _dft(s2)
    ang = -2.0 * np.pi / s * jnp.outer(jnp.arange(s1, dtype=F32), jnp.arange(s2, dtype=F32))
    twr = jnp.repeat(jnp.cos(ang), LANES, axis=1)
    twi = jnp.repeat(jnp.sin(ang), LANES, axis=1)
    cols = tn2 * wide
    x2 = fx.reshape(b, s1, s2 * wide)
    a = pl.pallas_call(
        _fft_stage1_kernel,
        grid=(b, s2 // tn2),
        in_specs=[
            pl.BlockSpec((1, s1, cols), lambda bi, ci: (bi, 0, ci)),
            pl.BlockSpec((s1, s1), lambda bi, ci: (0, 0)),
            pl.BlockSpec((s1, s1), lambda bi, ci: (0, 0)),
            pl.BlockSpec((s1, tn2 * LANES), lambda bi, ci: (0, ci)),
            pl.BlockSpec((s1, tn2 * LANES), lambda bi, ci: (0, ci)),
        ],
        out_specs=pl.BlockSpec((1, s1, cols), lambda bi, ci: (bi, 0, ci)),
        out_shape=jax.ShapeDtypeStruct((b, s1, s2 * wide), BF16),
        compiler_params=pltpu.CompilerParams(
            dimension_semantics=("arbitrary", "arbitrary"), vmem_limit_bytes=VMEM_LIMIT),
        name="fft_stage1",
    )(x2, jnp.asarray(d1r, BF16), jnp.asarray(d1i, BF16), twr, twi)
    a4 = a.reshape(b, s1, s2, wide)
    m2 = jnp.asarray(np.concatenate([d2r, -d2i], axis=1), BF16)
    y = pl.pallas_call(
        _fft_stage2_kernel,
        grid=(b, s1 // kb),
        in_specs=[
            pl.BlockSpec((1, kb, s2, wide), lambda bi, ki: (bi, ki, 0, 0)),
            pl.BlockSpec((s2, 2 * s2), lambda bi, ki: (0, 0)),
        ],
        out_specs=pl.BlockSpec((1, kb, s2, FNET_W), lambda bi, ki: (bi, ki, 0, 0)),
        out_shape=jax.ShapeDtypeStruct((b, s1, s2, FNET_W), F32),
        compiler_params=pltpu.CompilerParams(
            dimension_semantics=("arbitrary", "arbitrary"), vmem_limit_bytes=VMEM_LIMIT),
        name="fft_stage2",
    )(a4, m2)
    return y.transpose(0, 2, 1, 3).reshape(b, s, FNET_W)


def kernel(x_prompt, x_sample, attn_norm_g, w_in, q_lat_g, w_uq, kv_lat_g, w_ukv, q_head_g,
           k_head_g, attn_out_g, fnet_out_g, w_out, ffn_norm_g, peer_w_q, peer_sub_keys,
           peer_u, peer_v):
    l = 0
    keys = peer_sub_keys[l].reshape(2 * PEER_HEADS, PEER_NKEYS, PEER_HALF).astype(BF16)
    w_pq, u, v_tab = peer_w_q[l].astype(BF16), peer_u[l].astype(BF16), peer_v[l].astype(BF16)
    outs = []
    for x in (x_prompt, x_sample):
        q, kt, v, f = _project(x, attn_norm_g[l], w_in[l], q_lat_g[l], w_uq[l], kv_lat_g[l],
                               w_ukv[l], q_head_g[l], k_head_g[l])
        a = _flash_attention(q, kt, v).reshape(-1, ATTN_W)
        fm = _fnet(f).reshape(-1, FNET_W)
        x1, hn = _mix(x.reshape(-1, D_MODEL), a, fm, attn_out_g[l], fnet_out_g[l], w_out[l],
                      ffn_norm_g[l])
        rank2t, e2t, n1t, e1t = _peer_gate(hn, w_pq, keys)
        outs.append(_peer_dense(hn, u, v_tab, n1t, e1t, rank2t, e2t, x1).reshape(x.shape))
    return tuple(outs)
```

```python
import functools

import jax
import jax.numpy as jnp
import numpy as np
from jax import lax
from jax.experimental import pallas as pl
from jax.experimental.pallas import tpu as pltpu

D_MODEL = 1024
N_HEADS = 8
QK_NOPE = 64
QK_ROPE = 32
QK_HEAD = QK_NOPE + QK_ROPE
V_HEAD = 64
Q_LORA = 384
KV_LORA = 256
ATTN_W = N_HEADS * V_HEAD
ROPE_THETA = 10000.0
FNET_W = D_MODEL - ATTN_W
FNET_GROUPS = 4
FNET_CH = FNET_W // FNET_GROUPS
PEER_HEADS = 8
PEER_NKEYS = 128
PEER_EXPERTS = PEER_NKEYS * PEER_NKEYS
PEER_HALF = 128
PEER_TOPK = 16
EPS = 1e-6
LOG2_E = 1.4426950408889634

LANES = 128
BF16_ROWS = 16
GATE_GROUP = 4
VMEM_LIMIT = 48 * 1024 * 1024

F32 = jnp.float32
BF16 = jnp.bfloat16


def _flash_kernel(q_ref, kt_ref, v_ref, o_ref, m_scr, acc_scr, *, tk):
    seq = v_ref.shape[1]
    tq = q_ref.shape[1]
    nkv = seq // tk
    m_scr[...] = jnp.full(m_scr.shape, -jnp.inf, F32)
    acc_scr[...] = jnp.zeros(acc_scr.shape, F32)

    def body(j, carry):
        start = pl.multiple_of(j * tk, tk)
        heads = [slice(hh * LANES, (hh + 1) * LANES) for hh in range(2)]
        scores = [jnp.dot(q_ref[0, :, hs], kt_ref[0, hs, pl.ds(start, tk)],
                          preferred_element_type=F32) for hs in heads]
        for hh, hs in enumerate(heads):
            cols = [scores[hh][:, c * LANES:(c + 1) * LANES] for c in range(tk // LANES)]
            m_old = m_scr[hh]
            m_new = jnp.maximum(m_old, jnp.max(functools.reduce(jnp.maximum, cols),
                                               axis=1, keepdims=True))
            p = jnp.concatenate([jnp.exp2(c - m_new) for c in cols], axis=1).astype(BF16)
            pv = jnp.dot(p, v_ref[0, pl.ds(start, tk), hs], preferred_element_type=F32)
            acc_scr[hh] = jnp.exp2(m_old - m_new) * acc_scr[hh] + pv
            m_scr[hh] = m_new
        return carry

    lax.fori_loop(0, nkv, body, 0)
    outs = []
    for hh in range(2):
        acc = acc_scr[hh]
        outs.append(acc[:, :V_HEAD] / acc[:, V_HEAD:V_HEAD + 1])
    o_ref[0] = jnp.concatenate(outs, axis=1)


def _flash_attention(q, kt, v, *, tq=512, tk=2048):
    b, s, _ = q.shape
    grid = (b, N_HEADS // 2, s // tq)
    return pl.pallas_call(
        functools.partial(_flash_kernel, tk=tk),
        grid=grid,
        in_specs=[
            pl.BlockSpec((1, tq, 2 * LANES), lambda bi, hp, qi: (bi, qi, hp)),
            pl.BlockSpec((1, 2 * LANES, s), lambda bi, hp, qi: (bi, hp, 0)),
            pl.BlockSpec((1, s, 2 * LANES), lambda bi, hp, qi: (bi, 0, hp)),
        ],
        out_specs=pl.BlockSpec((1, tq, LANES), lambda bi, hp, qi: (bi, qi, hp)),
        out_shape=jax.ShapeDtypeStruct((b, s, ATTN_W), F32),
        scratch_shapes=[pltpu.VMEM((2, tq, LANES), F32), pltpu.VMEM((2, tq, LANES), F32)],
        compiler_params=pltpu.CompilerParams(
            dimension_semantics=("arbitrary", "arbitrary", "arbitrary"),
            vmem_limit_bytes=VMEM_LIMIT),
        name="flash_attention",
    )(q, kt, v)


_CAND_PAIRS = tuple((a, b) for a in range(PEER_TOPK) for b in range(PEER_TOPK)
                    if (a + 1) * (b + 1) <= PEER_TOPK)
_NEG_INF = float("-inf")


def _tree_max(xs):
    xs = list(xs)
    while len(xs) > 1:
        nxt = [jnp.maximum(xs[i], xs[i + 1]) for i in range(0, len(xs) - 1, 2)]
        if len(xs) % 2:
            nxt.append(xs[-1])
        xs = nxt
    return xs[0]


def _tree_min(xs):
    xs = list(xs)
    while len(xs) > 1:
        nxt = [jnp.minimum(xs[i], xs[i + 1]) for i in range(0, len(xs) - 1, 2)]
        if len(xs) % 2:
            nxt.append(xs[-1])
        xs = nxt
    return xs[0]


def _sort_network(n):
    pairs = []
    p = 1
    while p < n:
        k = p
        while k >= 1:
            for j in range(k % p, n - k, 2 * k):
                for i in range(min(k, n - j - k)):
                    if (i + j) // (2 * p) == (i + j + k) // (2 * p):
                        pairs.append((i + j, i + j + k))
            k //= 2
        p *= 2
    return tuple(pairs)


_SUBLANES = 8
_SORT_ROWS = _sort_network(PEER_NKEYS // _SUBLANES)


def _top_values(scores):
    rows = [scores[r * _SUBLANES:(r + 1) * _SUBLANES, :] for r in range(PEER_NKEYS // _SUBLANES)]
    for a, b in _SORT_ROWS:
        rows[a], rows[b] = jnp.maximum(rows[a], rows[b]), jnp.minimum(rows[a], rows[b])
    vals = []
    for t in range(PEER_TOPK):
        m = jnp.max(rows[0], axis=0, keepdims=True)
        vals.append(m)
        if t + 1 < PEER_TOPK:
            hit = rows[0] == m
            for r in range(PEER_TOPK - 1 - t):
                rows[r] = jnp.where(hit, rows[r + 1], rows[r])
    return vals


def _gate_kernel(hn_ref, wq_ref, keys_ref, rank2_ref, e2_ref, n1_ref, e1_ref, s_scr, vals_scr):
    tm = hn_ref.shape[0]
    ncol = tm // LANES
    q = jnp.dot(hn_ref[...], wq_ref[...], preferred_element_type=F32).astype(BF16)
    for hc in range(2 * PEER_HEADS):
        s_scr[hc] = lax.dot_general(keys_ref[hc], q[:, hc * PEER_HALF:(hc + 1) * PEER_HALF],
                                    (((1,), (1,)), ((), ())), preferred_element_type=F32)

    for hc in range(2 * PEER_HEADS):
        h, c = divmod(hc, 2)

        def col_body(col, carry, hc=hc, h=h, c=c):
            cs = pl.ds(pl.multiple_of(col * LANES, LANES), LANES)
            cur = s_scr[hc, :, cs]
            vals = _top_values(cur)
            for r in range(PEER_TOPK):
                vals_scr[c, r, h:h + 1, cs] = vals[r]
            if c == 1:
                rank = jnp.zeros((PEER_NKEYS, LANES), F32)
                for r in range(PEER_TOPK):
                    rank = rank + jnp.where(cur < vals[r], 1.0, 0.0)
                rank2_ref[h, :, cs] = rank
            return carry

        lax.fori_loop(0, ncol, col_body, 0, unroll=True)

    def fin_body(col, carry):
        cs = pl.ds(pl.multiple_of(col * LANES, LANES), LANES)
        v1 = [vals_scr[0, a, :, cs] for a in range(PEER_TOPK)]
        v2 = [vals_scr[1, b, :, cs] for b in range(PEER_TOPK)]
        cands = [v1[a] + v2[b] for a, b in _CAND_PAIRS]
        top = cands[0]
        z = jnp.zeros_like(top)
        m = top
        for r in range(PEER_TOPK):
            m = _tree_max(cands)
            z = z + jnp.exp(m - top)
            if r + 1 < PEER_TOPK:
                cands = [jnp.where(cd == m, _NEG_INF, cd) for cd in cands]
        thr = m
        rz = 1.0 / z
        cnt = [sum(jnp.where(v1[a] + v2[b] >= thr, 1.0, 0.0) for aa, b in _CAND_PAIRS if aa == a)
               for a in range(PEER_TOPK)]
        bound = [_tree_min([jnp.where(cnt[a] >= float(k), v1[a], float("inf"))
                            for a in range(PEER_TOPK // k)])
                 for k in range(1, PEER_TOPK + 1)]
        for h in range(PEER_HEADS):
            s1 = s_scr[2 * h, :, cs]
            n1 = jnp.zeros((PEER_NKEYS, LANES), F32)
            for k in range(PEER_TOPK):
                n1 = n1 + jnp.where(s1 >= bound[k][h:h + 1, :], 1.0, 0.0)
            n1_ref[h, :, cs] = n1
            e1_ref[h, :, cs] = jnp.exp(s1 - v1[0][h:h + 1, :])
            s2 = s_scr[2 * h + 1, :, cs]
            e2_ref[h, :, cs] = jnp.exp(s2 - v2[0][h:h + 1, :]) * rz[h:h + 1, :]
        return carry

    lax.fori_loop(0, ncol, fin_body, 0)


def _peer_gate(hn, wq, keys, *, tm=256):
    t = hn.shape[0]
    out = jax.ShapeDtypeStruct((PEER_HEADS, PEER_NKEYS, t), F32)
    ospec = pl.BlockSpec((PEER_HEADS, PEER_NKEYS, tm), lambda ti: (0, 0, ti))
    return pl.pallas_call(
        _gate_kernel,
        grid=(t // tm,),
        in_specs=[
            pl.BlockSpec((tm, D_MODEL), lambda ti: (ti, 0)),
            pl.BlockSpec((D_MODEL, 2 * PEER_HEADS * PEER_HALF), lambda ti: (0, 0)),
            pl.BlockSpec((2 * PEER_HEADS, PEER_NKEYS, PEER_HALF), lambda ti: (0, 0, 0)),
        ],
        out_specs=[ospec, ospec, ospec, ospec],
        out_shape=[out, out, out, out],
        scratch_shapes=[pltpu.VMEM((2 * PEER_HEADS, PEER_NKEYS, tm), F32),
                        pltpu.VMEM((2, PEER_TOPK, PEER_HEADS, tm), F32)],
        compiler_params=pltpu.CompilerParams(
            dimension_semantics=("arbitrary",), vmem_limit_bytes=VMEM_LIMIT),
        name="peer_gate",
    )(hn, wq, keys)


def _gelu_tanh(x):
    k1 = float(np.sqrt(2.0 / np.pi))
    k2 = k1 * 0.044715
    hx = 0.5 * x
    return hx + hx * jnp.tanh(x * (k1 + k2 * (x * x)))


def _peer_kernel(hn_ref, u_ref, vt_ref, n1_ref, e1_ref, rank2_ref, e2_ref, x_ref,
                 y_ref, acc_ref, a_scr, g_scr, r2_scr, e2_scr, *, n_i1):
    eb = pl.program_id(1)
    tm = hn_ref.shape[0]
    rows = PEER_NKEYS // BF16_ROWS

    @pl.when(eb == 0)
    def _():
        acc_ref[...] = jnp.zeros_like(acc_ref)
        r2_scr[...] = rank2_ref[...].astype(BF16)
        e2_scr[...] = e2_ref[...].astype(BF16)

    a_scr[...] = lax.dot_general(u_ref[...], hn_ref[...], (((1,), (1,)), ((), ())),
                                 preferred_element_type=F32).astype(BF16)

    for c in range(tm // LANES):
        cs = slice(c * LANES, (c + 1) * LANES)
        for j0 in range(0, n_i1, GATE_GROUP):
            group = range(j0, j0 + GATE_GROUP)
            gs = {j: jnp.zeros((rows, BF16_ROWS, LANES), BF16) for j in group}
            for h in range(PEER_HEADS):
                r2 = r2_scr[h, :, cs].reshape(rows, BF16_ROWS, LANES)
                e2 = e2_scr[h, :, cs].reshape(rows, BF16_ROWS, LANES)
                for j in group:
                    n1row = jnp.broadcast_to(n1_ref[h, 0, j:j + 1, cs], (BF16_ROWS, LANES)).astype(BF16)
                    e1row = jnp.broadcast_to(e1_ref[h, 0, j:j + 1, cs], (BF16_ROWS, LANES)).astype(BF16)
                    gs[j] = gs[j] + jnp.where(r2 < n1row[None], e2, jnp.zeros_like(e2)) * e1row[None]
            for j in group:
                g_scr[j * PEER_NKEYS:(j + 1) * PEER_NKEYS, cs] = gs[j].reshape(PEER_NKEYS, LANES)

    p = _gelu_tanh(a_scr[...]) * g_scr[...]
    acc_ref[...] += jnp.dot(vt_ref[0], p, preferred_element_type=F32)

    @pl.when(eb == pl.num_programs(1) - 1)
    def _():
        y_ref[...] = x_ref[...] + acc_ref[...].T


def _peer_dense(hn, u, v, n1t, e1t, rank2t, e2t, x1, *, tm=512, n_i1=8):
    t = hn.shape[0]
    te = n_i1 * PEER_NKEYS
    vt = v.reshape(PEER_EXPERTS // te, te, D_MODEL).transpose(0, 2, 1)
    nblk = PEER_NKEYS // n_i1
    n1r = n1t.reshape(PEER_HEADS, nblk, n_i1, t)
    e1r = e1t.reshape(PEER_HEADS, nblk, n_i1, t)
    return pl.pallas_call(
        functools.partial(_peer_kernel, n_i1=n_i1),
        grid=(t // tm, nblk),
        in_specs=[
            pl.BlockSpec((tm, D_MODEL), lambda ti, eb: (ti, 0)),
            pl.BlockSpec((te, D_MODEL), lambda ti, eb: (eb, 0)),
            pl.BlockSpec((1, D_MODEL, te), lambda ti, eb: (eb, 0, 0)),
            pl.BlockSpec((PEER_HEADS, 1, n_i1, tm), lambda ti, eb: (0, eb, 0, ti)),
            pl.BlockSpec((PEER_HEADS, 1, n_i1, tm), lambda ti, eb: (0, eb, 0, ti)),
            pl.BlockSpec((PEER_HEADS, PEER_NKEYS, tm), lambda ti, eb: (0, 0, ti)),
            pl.BlockSpec((PEER_HEADS, PEER_NKEYS, tm), lambda ti, eb: (0, 0, ti)),
            pl.BlockSpec((tm, D_MODEL), lambda ti, eb: (ti, 0)),
        ],
        out_specs=pl.BlockSpec((tm, D_MODEL), lambda ti, eb: (ti, 0)),
        out_shape=jax.ShapeDtypeStruct((t, D_MODEL), F32),
        scratch_shapes=[pltpu.VMEM((D_MODEL, tm), F32),
                        pltpu.VMEM((te, tm), BF16), pltpu.VMEM((te, tm), BF16),
                        pltpu.VMEM((PEER_HEADS, PEER_NKEYS, tm), BF16),
                        pltpu.VMEM((PEER_HEADS, PEER_NKEYS, tm), BF16)],
        compiler_params=pltpu.CompilerParams(
            dimension_semantics=("arbitrary", "arbitrary"),
            vmem_limit_bytes=VMEM_LIMIT),
        name="peer_dense",
    )(hn, u, vt, n1r, e1r, rank2t, e2t, x1)


def _row_rms(x, width):
    return x * lax.rsqrt(jnp.sum(x * x, axis=-1, keepdims=True) * (1.0 / width) + EPS)


def _proj_kernel(x_ref, g_in_ref, w_in_ref, g_q_ref, w_uq_ref, g_kv_ref, w_uk_ref, w_uv_ref,
                 g_qh_ref, g_kh_ref, cos_ref, sin_ref, ones_ref, perm_ref, v_one_ref, dft_c_ref,
                 q_ref, kt_ref, v_ref, f_ref):
    o0, o1, o2 = Q_LORA, Q_LORA + KV_LORA, Q_LORA + KV_LORA + LANES
    h = (_row_rms(x_ref[0], D_MODEL) * g_in_ref[...]).astype(BF16)
    p = jnp.dot(h, w_in_ref[...], preferred_element_type=F32)
    c_q = (_row_rms(p[:, :o0], Q_LORA) * g_q_ref[...]).astype(BF16)
    c_kv = (_row_rms(p[:, o0:o1], KV_LORA) * g_kv_ref[...]).astype(BF16)
    k_rope = p[:, o1:o2]
    f_ref[0] = jnp.dot(p[:, o2:].astype(BF16), dft_c_ref[...],
                       preferred_element_type=F32).astype(BF16)
    q = jnp.dot(c_q, w_uq_ref[...], preferred_element_type=F32)
    k = jnp.dot(c_kv, w_uk_ref[...], preferred_element_type=F32)
    v = jnp.dot(c_kv, w_uv_ref[...], preferred_element_type=F32)
    v_ref[0] = (v + v_one_ref[...]).astype(BF16)

    cos, sin = cos_ref[...], sin_ref[...]

    def select_sum(val, mat):
        return jnp.dot(val.astype(BF16), mat, preferred_element_type=F32)

    def head_norm_rope(slot, gain):
        ss = select_sum(slot * slot, ones_ref[...])
        y = slot * lax.rsqrt(ss * (1.0 / QK_HEAD) + EPS) * gain
        return y * cos + select_sum(y, perm_ref[...]) * sin

    q_slots, k_slots = [], []
    for hd in range(N_HEADS):
        hs = slice(hd * LANES, (hd + 1) * LANES)
        q_slots.append(head_norm_rope(q[:, hs], g_qh_ref[...]))
        k_slots.append(head_norm_rope(k[:, hs] + k_rope, g_kh_ref[...]))
    q_ref[0] = jnp.concatenate(q_slots, axis=1).astype(BF16)
    kt_ref[0] = jnp.concatenate(k_slots, axis=1).T.astype(BF16)


def _slot_cols(w, width, n_in):
    w = w.reshape(n_in, N_HEADS, width)
    return jnp.pad(w, ((0, 0), (0, 0), (0, LANES - width))).reshape(n_in, N_HEADS * LANES)


def _project(x, attn_norm_g, w_in, q_lat_g, w_uq, kv_lat_g, w_ukv, q_head_g, k_head_g, *, tm=512):
    b, s, _ = x.shape
    o0, o1, o2 = Q_LORA, Q_LORA + KV_LORA, Q_LORA + KV_LORA + QK_ROPE
    k_rope_cols = jnp.pad(w_in[:, o1:o2], ((0, 0), (QK_NOPE, LANES - QK_HEAD)))
    w_in_p = jnp.concatenate([w_in[:, :o1], k_rope_cols, w_in[:, o2:]], axis=1).astype(BF16)
    w_uq_p = _slot_cols(w_uq, QK_HEAD, Q_LORA).astype(BF16)
    w_ukv_h = w_ukv.reshape(KV_LORA, N_HEADS, QK_NOPE + V_HEAD)
    w_uk_p = _slot_cols(w_ukv_h[:, :, :QK_NOPE].reshape(KV_LORA, -1), QK_NOPE, KV_LORA).astype(BF16)
    w_uv_p = _slot_cols(w_ukv_h[:, :, QK_NOPE:].reshape(KV_LORA, -1), V_HEAD, KV_LORA).astype(BF16)
    pad_gain = lambda g: jnp.pad(g, (0, LANES - QK_HEAD)).reshape(1, LANES)
    g_qh = pad_gain(q_head_g) * (QK_HEAD ** -0.5 * LOG2_E)
    g_kh = pad_gain(k_head_g)
    half = QK_ROPE // 2
    freqs = 1.0 / (ROPE_THETA ** (jnp.arange(half, dtype=F32) / half))
    ang = jnp.arange(s, dtype=F32)[:, None] * freqs[None, :]
    cos, sin = jnp.cos(ang), jnp.sin(ang)
    zeros = lambda n: jnp.zeros((s, n), F32)
    cos_t = jnp.concatenate([jnp.ones((s, QK_NOPE), F32), cos, cos, zeros(LANES - QK_HEAD)], axis=1)
    sin_t = jnp.concatenate([zeros(QK_NOPE), -sin, sin, zeros(LANES - QK_HEAD)], axis=1)
    lane = np.arange(LANES)
    partner = np.where((lane >= QK_NOPE) & (lane < QK_NOPE + half), lane + half,
                       np.where((lane >= QK_NOPE + half) & (lane < QK_HEAD), lane - half, -1))
    perm = jnp.asarray(lane[:, None] == partner[None, :], BF16)
    ones = jnp.ones((LANES, LANES), BF16)
    v_one = jnp.tile((jnp.arange(LANES) == V_HEAD).astype(F32), N_HEADS).reshape(1, -1)
    row = lambda g: g.reshape(1, -1)

    const = lambda shape: pl.BlockSpec(shape, lambda bi, si: (0,) * len(shape))
    tab = pl.BlockSpec((tm, LANES), lambda bi, si: (si, 0))
    wide = N_HEADS * LANES
    return pl.pallas_call(
        _proj_kernel,
        grid=(b, s // tm),
        in_specs=[
            pl.BlockSpec((1, tm, D_MODEL), lambda bi, si: (bi, si, 0)),
            const((1, D_MODEL)), const(w_in_p.shape), const((1, Q_LORA)), const(w_uq_p.shape),
            const((1, KV_LORA)), const(w_uk_p.shape), const(w_uv_p.shape),
            const((1, LANES)), const((1, LANES)), tab, tab, const((LANES, LANES)),
            const((LANES, LANES)), const((1, wide)),
            const((FNET_W, 2 * FNET_W)),
        ],
        out_specs=[
            pl.BlockSpec((1, tm, wide), lambda bi, si: (bi, si, 0)),
            pl.BlockSpec((1, wide, tm), lambda bi, si: (bi, 0, si)),
            pl.BlockSpec((1, tm, wide), lambda bi, si: (bi, si, 0)),
            pl.BlockSpec((1, tm, 2 * FNET_W), lambda bi, si: (bi, si, 0)),
        ],
        out_shape=[
            jax.ShapeDtypeStruct((b, s, wide), BF16),
            jax.ShapeDtypeStruct((b, wide, s), BF16),
            jax.ShapeDtypeStruct((b, s, wide), BF16),
            jax.ShapeDtypeStruct((b, s, 2 * FNET_W), BF16),
        ],
        compiler_params=pltpu.CompilerParams(
            dimension_semantics=("arbitrary", "arbitrary"), vmem_limit_bytes=VMEM_LIMIT),
        name="input_projection",
    )(x, row(attn_norm_g), w_in_p, row(q_lat_g), w_uq_p, row(kv_lat_g), w_uk_p, w_uv_p,
      g_qh, g_kh, cos_t, sin_t, ones, perm, v_one, _channel_dft(s))


def _mix_kernel(x_ref, a_ref, fm_ref, g_a_ref, g_f_ref, w_a_ref, w_f_ref, g_ffn_ref,
                x1_ref, hn_ref):
    a = (_row_rms(a_ref[...], ATTN_W) * g_a_ref[...]).astype(BF16)
    fm = (_row_rms(fm_ref[...], FNET_W) * g_f_ref[...]).astype(BF16)
    x1 = (x_ref[...] + jnp.dot(a, w_a_ref[...], preferred_element_type=F32)
          + jnp.dot(fm, w_f_ref[...], preferred_element_type=F32))
    x1_ref[...] = x1
    hn_ref[...] = (_row_rms(x1, D_MODEL) * g_ffn_ref[...]).astype(BF16)


def _mix(x, a, fm, attn_out_g, fnet_out_g, w_out, ffn_norm_g, *, tm=512):
    t = x.shape[0]
    row = lambda g: g.reshape(1, -1)
    tok = lambda w: pl.BlockSpec((tm, w), lambda ti: (ti, 0))
    const = lambda shape: pl.BlockSpec(shape, lambda ti: (0, 0))
    return pl.pallas_call(
        _mix_kernel,
        grid=(t // tm,),
        in_specs=[tok(D_MODEL), tok(ATTN_W), tok(FNET_W), const((1, ATTN_W)), const((1, FNET_W)),
                  const((ATTN_W, D_MODEL)), const((FNET_W, D_MODEL)), const((1, D_MODEL))],
        out_specs=[tok(D_MODEL), tok(D_MODEL)],
        out_shape=[jax.ShapeDtypeStruct((t, D_MODEL), F32), jax.ShapeDtypeStruct((t, D_MODEL), BF16)],
        compiler_params=pltpu.CompilerParams(
            dimension_semantics=("arbitrary",), vmem_limit_bytes=VMEM_LIMIT),
        name="output_mix",
    )(x, a, fm, row(attn_out_g), row(fnet_out_g), w_out[:ATTN_W].astype(BF16),
      w_out[ATTN_W:].astype(BF16), row(ffn_norm_g))


FFT_S1 = 64


def _dft(n):
    ang = -2.0 * np.pi * np.outer(np.arange(n), np.arange(n)) / n
    return np.cos(ang), np.sin(ang)


def _channel_dft(s):
    c, si = _dft(FNET_CH)
    eye = np.eye(FNET_GROUPS)
    scale = (FNET_CH * s) ** -0.5
    return jnp.asarray(np.concatenate([np.kron(eye, c), np.kron(eye, si)], axis=1) * scale, BF16)


def _fft_stage1_kernel(x_ref, dr_ref, di_ref, twr_ref, twi_ref, o_ref):
    x = x_ref[0]
    yr = jnp.dot(dr_ref[...], x, preferred_element_type=F32)
    yi = jnp.dot(di_ref[...], x, preferred_element_type=F32)
    w = FNET_W
    for j in range(x.shape[1] // (2 * w)):
        re, im = slice(2 * j * w, (2 * j + 1) * w), slice((2 * j + 1) * w, (2 * j + 2) * w)
        ar = yr[:, re] - yi[:, im]
        ai = yr[:, im] + yi[:, re]
        tw = slice(j * LANES, (j + 1) * LANES)
        twr = jnp.concatenate([twr_ref[:, tw]] * (w // LANES), axis=1)
        twi = jnp.concatenate([twi_ref[:, tw]] * (w // LANES), axis=1)
        o_ref[0, :, re] = (ar * twr - ai * twi).astype(BF16)
        o_ref[0, :, im] = (ar * twi + ai * twr).astype(BF16)


def _fft_stage2_kernel(a_ref, m_ref, o_ref):
    w = FNET_W
    for j in range(a_ref.shape[1]):
        xa = a_ref[0, j]
        stacked = jnp.concatenate([xa[:, :w], xa[:, w:]], axis=0)
        o_ref[0, j] = jnp.dot(m_ref[...], stacked, preferred_element_type=F32)


def _fnet(fx, *, tn2=8, kb=8):
    b, s, wide = fx.shape
    s1, s2 = FFT_S1, s // FFT_S1
    d1r, d1i = _dft(s1)
    d2r, d2i = _dft(s2)
    ang = -2.0 * np.pi / s * jnp.outer(jnp.arange(s1, dtype=F32), jnp.arange(s2, dtype=F32))
    twr = jnp.repeat(jnp.cos(ang), LANES, axis=1)
    twi = jnp.repeat(jnp.sin(ang), LANES, axis=1)
    cols = tn2 * wide
    x2 = fx.reshape(b, s1, s2 * wide)
    a = pl.pallas_call(
        _fft_stage1_kernel,
        grid=(b, s2 // tn2),
        in_specs=[
            pl.BlockSpec((1, s1, cols), lambda bi, ci: (bi, 0, ci)),
            pl.BlockSpec((s1, s1), lambda bi, ci: (0, 0)),
            pl.BlockSpec((s1, s1), lambda bi, ci: (0, 0)),
            pl.BlockSpec((s1, tn2 * LANES), lambda bi, ci: (0, ci)),
            pl.BlockSpec((s1, tn2 * LANES), lambda bi, ci: (0, ci)),
        ],
        out_specs=pl.BlockSpec((1, s1, cols), lambda bi, ci: (bi, 0, ci)),
        out_shape=jax.ShapeDtypeStruct((b, s1, s2 * wide), BF16),
        compiler_params=pltpu.CompilerParams(
            dimension_semantics=("arbitrary", "arbitrary"), vmem_limit_bytes=VMEM_LIMIT),
        name="fft_stage1",
    )(x2, jnp.asarray(d1r, BF16), jnp.asarray(d1i, BF16), twr, twi)
    a4 = a.reshape(b, s1, s2, wide)
    m2 = jnp.asarray(np.concatenate([d2r, -d2i], axis=1), BF16)
    y = pl.pallas_call(
        _fft_stage2_kernel,
        grid=(b, s1 // kb),
        in_specs=[
            pl.BlockSpec((1, kb, s2, wide), lambda bi, ki: (bi, ki, 0, 0)),
            pl.BlockSpec((s2, 2 * s2), lambda bi, ki: (0, 0)),
        ],
        out_specs=pl.BlockSpec((1, kb, s2, FNET_W), lambda bi, ki: (bi, ki, 0, 0)),
        out_shape=jax.ShapeDtypeStruct((b, s1, s2, FNET_W), F32),
        compiler_params=pltpu.CompilerParams(
            dimension_semantics=("arbitrary", "arbitrary"), vmem_limit_bytes=VMEM_LIMIT),
        name="fft_stage2",
    )(a4, m2)
    return y.transpose(0, 2, 1, 3).reshape(b, s, FNET_W)


def kernel(x_prompt, x_sample, attn_norm_g, w_in, q_lat_g, w_uq, kv_lat_g, w_ukv, q_head_g,
           k_head_g, attn_out_g, fnet_out_g, w_out, ffn_norm_g, peer_w_q, peer_sub_keys,
           peer_u, peer_v):
    l = 0
    keys = peer_sub_keys[l].reshape(2 * PEER_HEADS, PEER_NKEYS, PEER_HALF).astype(BF16)
    w_pq, u, v_tab = peer_w_q[l].astype(BF16), peer_u[l].astype(BF16), peer_v[l].astype(BF16)
    outs = []
    for x in (x_prompt, x_sample):
        q, kt, v, f = _project(x, attn_norm_g[l], w_in[l], q_lat_g[l], w_uq[l], kv_lat_g[l],
                               w_ukv[l], q_head_g[l], k_head_g[l])
        a = _flash_attention(q, kt, v).reshape(-1, ATTN_W)
        fm = _fnet(f).reshape(-1, FNET_W)
        x1, hn = _mix(x.reshape(-1, D_MODEL), a, fm, attn_out_g[l], fnet_out_g[l], w_out[l],
                      ffn_norm_g[l])
        rank2t, e2t, n1t, e1t = _peer_gate(hn, w_pq, keys)
        outs.append(_peer_dense(hn, u, v_tab, n1t, e1t, rank2t, e2t, x1).reshape(x.shape))
    return tuple(outs)
```

```python
import functools

import jax
import jax.numpy as jnp
import numpy as np
from jax import lax
from jax.experimental import pallas as pl
from jax.experimental.pallas import tpu as pltpu

D_MODEL = 1024
N_HEADS = 8
QK_NOPE = 64
QK_ROPE = 32
QK_HEAD = QK_NOPE + QK_ROPE
V_HEAD = 64
Q_LORA = 384
KV_LORA = 256
ATTN_W = N_HEADS * V_HEAD
ROPE_THETA = 10000.0
FNET_W = D_MODEL - ATTN_W
FNET_GROUPS = 4
FNET_CH = FNET_W // FNET_GROUPS
PEER_HEADS = 8
PEER_NKEYS = 128
PEER_EXPERTS = PEER_NKEYS * PEER_NKEYS
PEER_HALF = 128
PEER_TOPK = 16
EPS = 1e-6
LOG2_E = 1.4426950408889634

LANES = 128
BF16_ROWS = 16
GATE_GROUP = 4
VMEM_LIMIT = 48 * 1024 * 1024

F32 = jnp.float32
BF16 = jnp.bfloat16


def _flash_kernel(q_ref, kt_ref, v_ref, o_ref, m_scr, acc_scr, *, tk):
    seq = v_ref.shape[1]
    tq = q_ref.shape[1]
    nkv = seq // tk
    m_scr[...] = jnp.full(m_scr.shape, -jnp.inf, F32)
    acc_scr[...] = jnp.zeros(acc_scr.shape, F32)

    def body(j, carry):
        start = pl.multiple_of(j * tk, tk)
        heads = [slice(hh * LANES, (hh + 1) * LANES) for hh in range(2)]
        scores = [jnp.dot(q_ref[0, :, hs], kt_ref[0, hs, pl.ds(start, tk)],
                          preferred_element_type=F32) for hs in heads]
        for hh, hs in enumerate(heads):
            cols = [scores[hh][:, c * LANES:(c + 1) * LANES] for c in range(tk // LANES)]
            m_old = m_scr[hh]
            m_new = jnp.maximum(m_old, jnp.max(functools.reduce(jnp.maximum, cols),
                                               axis=1, keepdims=True))
            p = jnp.concatenate([jnp.exp2(c - m_new) for c in cols], axis=1).astype(BF16)
            pv = jnp.dot(p, v_ref[0, pl.ds(start, tk), hs], preferred_element_type=F32)
            acc_scr[hh] = jnp.exp2(m_old - m_new) * acc_scr[hh] + pv
            m_scr[hh] = m_new
        return carry

    lax.fori_loop(0, nkv, body, 0, unroll=2)
    outs = []
    for hh in range(2):
        acc = acc_scr[hh]
        outs.append(acc[:, :V_HEAD] / acc[:, V_HEAD:V_HEAD + 1])
    o_ref[0] = jnp.concatenate(outs, axis=1)


def _flash_attention(q, kt, v, *, tq=512, tk=2048):
    b, s, _ = q.shape
    grid = (b, N_HEADS // 2, s // tq)
    return pl.pallas_call(
        functools.partial(_flash_kernel, tk=tk),
        grid=grid,
        in_specs=[
            pl.BlockSpec((1, tq, 2 * LANES), lambda bi, hp, qi: (bi, qi, hp)),
            pl.BlockSpec((1, 2 * LANES, s), lambda bi, hp, qi: (bi, hp, 0)),
            pl.BlockSpec((1, s, 2 * LANES), lambda bi, hp, qi: (bi, 0, hp)),
        ],
        out_specs=pl.BlockSpec((1, tq, LANES), lambda bi, hp, qi: (bi, qi, hp)),
        out_shape=jax.ShapeDtypeStruct((b, s, ATTN_W), F32),
        scratch_shapes=[pltpu.VMEM((2, tq, LANES), F32), pltpu.VMEM((2, tq, LANES), F32)],
        compiler_params=pltpu.CompilerParams(
            dimension_semantics=("arbitrary", "arbitrary", "arbitrary"),
            vmem_limit_bytes=VMEM_LIMIT),
        name="flash_attention",
    )(q, kt, v)


_CAND_PAIRS = tuple((a, b) for a in range(PEER_TOPK) for b in range(PEER_TOPK)
                    if (a + 1) * (b + 1) <= PEER_TOPK)
_NEG_INF = float("-inf")


def _tree_max(xs):
    xs = list(xs)
    while len(xs) > 1:
        nxt = [jnp.maximum(xs[i], xs[i + 1]) for i in range(0, len(xs) - 1, 2)]
        if len(xs) % 2:
            nxt.append(xs[-1])
        xs = nxt
    return xs[0]


def _tree_min(xs):
    xs = list(xs)
    while len(xs) > 1:
        nxt = [jnp.minimum(xs[i], xs[i + 1]) for i in range(0, len(xs) - 1, 2)]
        if len(xs) % 2:
            nxt.append(xs[-1])
        xs = nxt
    return xs[0]


def _sort_network(n):
    pairs = []
    p = 1
    while p < n:
        k = p
        while k >= 1:
            for j in range(k % p, n - k, 2 * k):
                for i in range(min(k, n - j - k)):
                    if (i + j) // (2 * p) == (i + j + k) // (2 * p):
                        pairs.append((i + j, i + j + k))
            k //= 2
        p *= 2
    return tuple(pairs)


_SUBLANES = 8
_SORT_ROWS = _sort_network(PEER_NKEYS // _SUBLANES)


def _top_values(scores):
    rows = [scores[r * _SUBLANES:(r + 1) * _SUBLANES, :] for r in range(PEER_NKEYS // _SUBLANES)]
    for a, b in _SORT_ROWS:
        rows[a], rows[b] = jnp.maximum(rows[a], rows[b]), jnp.minimum(rows[a], rows[b])
    vals = []
    for t in range(PEER_TOPK):
        m = jnp.max(rows[0], axis=0, keepdims=True)
        vals.append(m)
        if t + 1 < PEER_TOPK:
            hit = rows[0] == m
            for r in range(PEER_TOPK - 1 - t):
                rows[r] = jnp.where(hit, rows[r + 1], rows[r])
    return vals


def _gate_kernel(hn_ref, wq_ref, keys_ref, rank2_ref, e2_ref, n1_ref, e1_ref, s_scr, vals_scr):
    tm = hn_ref.shape[0]
    ncol = tm // LANES
    q = jnp.dot(hn_ref[...], wq_ref[...], preferred_element_type=F32).astype(BF16)
    for hc in range(2 * PEER_HEADS):
        s_scr[hc] = lax.dot_general(keys_ref[hc], q[:, hc * PEER_HALF:(hc + 1) * PEER_HALF],
                                    (((1,), (1,)), ((), ())), preferred_element_type=F32)

    for hc in range(2 * PEER_HEADS):
        h, c = divmod(hc, 2)

        def col_body(col, carry, hc=hc, h=h, c=c):
            cs = pl.ds(pl.multiple_of(col * LANES, LANES), LANES)
            cur = s_scr[hc, :, cs]
            vals = _top_values(cur)
            for r in range(PEER_TOPK):
                vals_scr[c, r, h:h + 1, cs] = vals[r]
            if c == 1:
                rank = jnp.zeros((PEER_NKEYS, LANES), F32)
                for r in range(PEER_TOPK):
                    rank = rank + jnp.where(cur < vals[r], 1.0, 0.0)
                rank2_ref[h, :, cs] = rank
            return carry

        lax.fori_loop(0, ncol, col_body, 0, unroll=True)

    def fin_body(col, carry):
        cs = pl.ds(pl.multiple_of(col * LANES, LANES), LANES)
        v1 = [vals_scr[0, a, :, cs] for a in range(PEER_TOPK)]
        v2 = [vals_scr[1, b, :, cs] for b in range(PEER_TOPK)]
        cands = [v1[a] + v2[b] for a, b in _CAND_PAIRS]
        top = cands[0]
        z = jnp.zeros_like(top)
        m = top
        for r in range(PEER_TOPK):
            m = _tree_max(cands)
            z = z + jnp.exp(m - top)
            if r + 1 < PEER_TOPK:
                cands = [jnp.where(cd == m, _NEG_INF, cd) for cd in cands]
        thr = m
        rz = 1.0 / z
        cnt = [sum(jnp.where(v1[a] + v2[b] >= thr, 1.0, 0.0) for aa, b in _CAND_PAIRS if aa == a)
               for a in range(PEER_TOPK)]
        bound = [_tree_min([jnp.where(cnt[a] >= float(k), v1[a], float("inf"))
                            for a in range(PEER_TOPK // k)])
                 for k in range(1, PEER_TOPK + 1)]
        for h in range(PEER_HEADS):
            s1 = s_scr[2 * h, :, cs]
            n1 = jnp.zeros((PEER_NKEYS, LANES), F32)
            for k in range(PEER_TOPK):
                n1 = n1 + jnp.where(s1 >= bound[k][h:h + 1, :], 1.0, 0.0)
            n1_ref[h, :, cs] = n1
            e1_ref[h, :, cs] = jnp.exp(s1 - v1[0][h:h + 1, :])
            s2 = s_scr[2 * h + 1, :, cs]
            e2_ref[h, :, cs] = jnp.exp(s2 - v2[0][h:h + 1, :]) * rz[h:h + 1, :]
        return carry

    lax.fori_loop(0, ncol, fin_body, 0)


def _peer_gate(hn, wq, keys, *, tm=256):
    t = hn.shape[0]
    out = jax.ShapeDtypeStruct((PEER_HEADS, PEER_NKEYS, t), F32)
    ospec = pl.BlockSpec((PEER_HEADS, PEER_NKEYS, tm), lambda ti: (0, 0, ti))
    return pl.pallas_call(
        _gate_kernel,
        grid=(t // tm,),
        in_specs=[
            pl.BlockSpec((tm, D_MODEL), lambda ti: (ti, 0)),
            pl.BlockSpec((D_MODEL, 2 * PEER_HEADS * PEER_HALF), lambda ti: (0, 0)),
            pl.BlockSpec((2 * PEER_HEADS, PEER_NKEYS, PEER_HALF), lambda ti: (0, 0, 0)),
        ],
        out_specs=[ospec, ospec, ospec, ospec],
        out_shape=[out, out, out, out],
        scratch_shapes=[pltpu.VMEM((2 * PEER_HEADS, PEER_NKEYS, tm), F32),
                        pltpu.VMEM((2, PEER_TOPK, PEER_HEADS, tm), F32)],
        compiler_params=pltpu.CompilerParams(
            dimension_semantics=("arbitrary",), vmem_limit_bytes=VMEM_LIMIT),
        name="peer_gate",
    )(hn, wq, keys)


def _gelu_tanh(x):
    k1 = float(np.sqrt(2.0 / np.pi))
    k2 = k1 * 0.044715
    hx = 0.5 * x
    return hx + hx * jnp.tanh(x * (k1 + k2 * (x * x)))


def _peer_kernel(hn_ref, u_ref, vt_ref, n1_ref, e1_ref, rank2_ref, e2_ref, x_ref,
                 y_ref, acc_ref, a_scr, g_scr, r2_scr, e2_scr, *, n_i1):
    eb = pl.program_id(1)
    tm = hn_ref.shape[0]
    rows = PEER_NKEYS // BF16_ROWS

    @pl.when(eb == 0)
    def _():
        acc_ref[...] = jnp.zeros_like(acc_ref)
        r2_scr[...] = rank2_ref[...].astype(BF16)
        e2_scr[...] = e2_ref[...].astype(BF16)

    a_scr[...] = lax.dot_general(u_ref[...], hn_ref[...], (((1,), (1,)), ((), ())),
                                 preferred_element_type=F32).astype(BF16)

    for c in range(tm // LANES):
        cs = slice(c * LANES, (c + 1) * LANES)
        for j0 in range(0, n_i1, GATE_GROUP):
            group = range(j0, j0 + GATE_GROUP)
            gs = {j: jnp.zeros((rows, BF16_ROWS, LANES), BF16) for j in group}
            for h in range(PEER_HEADS):
                r2 = r2_scr[h, :, cs].reshape(rows, BF16_ROWS, LANES)
                e2 = e2_scr[h, :, cs].reshape(rows, BF16_ROWS, LANES)
                for j in group:
                    n1row = jnp.broadcast_to(n1_ref[h, 0, j:j + 1, cs], (BF16_ROWS, LANES)).astype(BF16)
                    e1row = jnp.broadcast_to(e1_ref[h, 0, j:j + 1, cs], (BF16_ROWS, LANES)).astype(BF16)
                    gs[j] = gs[j] + jnp.where(r2 < n1row[None], e2, jnp.zeros_like(e2)) * e1row[None]
            for j in group:
                g_scr[j * PEER_NKEYS:(j + 1) * PEER_NKEYS, cs] = gs[j].reshape(PEER_NKEYS, LANES)

    p = _gelu_tanh(a_scr[...]) * g_scr[...]
    acc_ref[...] += jnp.dot(vt_ref[0], p, preferred_element_type=F32)

    @pl.when(eb == pl.num_programs(1) - 1)
    def _():
        y_ref[...] = x_ref[...] + acc_ref[...].T


def _peer_dense(hn, u, v, n1t, e1t, rank2t, e2t, x1, *, tm=512, n_i1=8):
    t = hn.shape[0]
    te = n_i1 * PEER_NKEYS
    vt = v.reshape(PEER_EXPERTS // te, te, D_MODEL).transpose(0, 2, 1)
    nblk = PEER_NKEYS // n_i1
    n1r = n1t.reshape(PEER_HEADS, nblk, n_i1, t)
    e1r = e1t.reshape(PEER_HEADS, nblk, n_i1, t)
    return pl.pallas_call(
        functools.partial(_peer_kernel, n_i1=n_i1),
        grid=(t // tm, nblk),
        in_specs=[
            pl.BlockSpec((tm, D_MODEL), lambda ti, eb: (ti, 0)),
            pl.BlockSpec((te, D_MODEL), lambda ti, eb: (eb, 0)),
            pl.BlockSpec((1, D_MODEL, te), lambda ti, eb: (eb, 0, 0)),
            pl.BlockSpec((PEER_HEADS, 1, n_i1, tm), lambda ti, eb: (0, eb, 0, ti)),
            pl.BlockSpec((PEER_HEADS, 1, n_i1, tm), lambda ti, eb: (0, eb, 0, ti)),
            pl.BlockSpec((PEER_HEADS, PEER_NKEYS, tm), lambda ti, eb: (0, 0, ti)),
            pl.BlockSpec((PEER_HEADS, PEER_NKEYS, tm), lambda ti, eb: (0, 0, ti)),
            pl.BlockSpec((tm, D_MODEL), lambda ti, eb: (ti, 0)),
        ],
        out_specs=pl.BlockSpec((tm, D_MODEL), lambda ti, eb: (ti, 0)),
        out_shape=jax.ShapeDtypeStruct((t, D_MODEL), F32),
        scratch_shapes=[pltpu.VMEM((D_MODEL, tm), F32),
                        pltpu.VMEM((te, tm), BF16), pltpu.VMEM((te, tm), BF16),
                        pltpu.VMEM((PEER_HEADS, PEER_NKEYS, tm), BF16),
                        pltpu.VMEM((PEER_HEADS, PEER_NKEYS, tm), BF16)],
        compiler_params=pltpu.CompilerParams(
            dimension_semantics=("arbitrary", "arbitrary"),
            vmem_limit_bytes=VMEM_LIMIT),
        name="peer_dense",
    )(hn, u, vt, n1r, e1r, rank2t, e2t, x1)


def _row_rms(x, width):
    return x * lax.rsqrt(jnp.sum(x * x, axis=-1, keepdims=True) * (1.0 / width) + EPS)


def _proj_kernel(x_ref, g_in_ref, w_in_ref, g_q_ref, w_uq_ref, g_kv_ref, w_uk_ref, w_uv_ref,
                 g_qh_ref, g_kh_ref, cos_ref, sin_ref, ones_ref, perm_ref, v_one_ref, dft_c_ref,
                 q_ref, kt_ref, v_ref, f_ref):
    o0, o1, o2 = Q_LORA, Q_LORA + KV_LORA, Q_LORA + KV_LORA + LANES
    h = (_row_rms(x_ref[0], D_MODEL) * g_in_ref[...]).astype(BF16)
    p = jnp.dot(h, w_in_ref[...], preferred_element_type=F32)
    c_q = (_row_rms(p[:, :o0], Q_LORA) * g_q_ref[...]).astype(BF16)
    c_kv = (_row_rms(p[:, o0:o1], KV_LORA) * g_kv_ref[...]).astype(BF16)
    k_rope = p[:, o1:o2]
    f_ref[0] = jnp.dot(p[:, o2:].astype(BF16), dft_c_ref[...],
                       preferred_element_type=F32).astype(BF16)
    q = jnp.dot(c_q, w_uq_ref[...], preferred_element_type=F32)
    k = jnp.dot(c_kv, w_uk_ref[...], preferred_element_type=F32)
    v = jnp.dot(c_kv, w_uv_ref[...], preferred_element_type=F32)
    v_ref[0] = (v + v_one_ref[...]).astype(BF16)

    cos, sin = cos_ref[...], sin_ref[...]

    def select_sum(val, mat):
        return jnp.dot(val.astype(BF16), mat, preferred_element_type=F32)

    def head_norm_rope(slot, gain):
        ss = select_sum(slot * slot, ones_ref[...])
        y = slot * lax.rsqrt(ss * (1.0 / QK_HEAD) + EPS) * gain
        return y * cos + select_sum(y, perm_ref[...]) * sin

    q_slots, k_slots = [], []
    for hd in range(N_HEADS):
        hs = slice(hd * LANES, (hd + 1) * LANES)
        q_slots.append(head_norm_rope(q[:, hs], g_qh_ref[...]))
        k_slots.append(head_norm_rope(k[:, hs] + k_rope, g_kh_ref[...]))
    q_ref[0] = jnp.concatenate(q_slots, axis=1).astype(BF16)
    kt_ref[0] = jnp.concatenate(k_slots, axis=1).T.astype(BF16)


def _slot_cols(w, width, n_in):
    w = w.reshape(n_in, N_HEADS, width)
    return jnp.pad(w, ((0, 0), (0, 0), (0, LANES - width))).reshape(n_in, N_HEADS * LANES)


def _project(x, attn_norm_g, w_in, q_lat_g, w_uq, kv_lat_g, w_ukv, q_head_g, k_head_g, *, tm=512):
    b, s, _ = x.shape
    o0, o1, o2 = Q_LORA, Q_LORA + KV_LORA, Q_LORA + KV_LORA + QK_ROPE
    k_rope_cols = jnp.pad(w_in[:, o1:o2], ((0, 0), (QK_NOPE, LANES - QK_HEAD)))
    w_in_p = jnp.concatenate([w_in[:, :o1], k_rope_cols, w_in[:, o2:]], axis=1).astype(BF16)
    w_uq_p = _slot_cols(w_uq, QK_HEAD, Q_LORA).astype(BF16)
    w_ukv_h = w_ukv.reshape(KV_LORA, N_HEADS, QK_NOPE + V_HEAD)
    w_uk_p = _slot_cols(w_ukv_h[:, :, :QK_NOPE].reshape(KV_LORA, -1), QK_NOPE, KV_LORA).astype(BF16)
    w_uv_p = _slot_cols(w_ukv_h[:, :, QK_NOPE:].reshape(KV_LORA, -1), V_HEAD, KV_LORA).astype(BF16)
    pad_gain = lambda g: jnp.pad(g, (0, LANES - QK_HEAD)).reshape(1, LANES)
    g_qh = pad_gain(q_head_g) * (QK_HEAD ** -0.5 * LOG2_E)
    g_kh = pad_gain(k_head_g)
    half = QK_ROPE // 2
    freqs = 1.0 / (ROPE_THETA ** (jnp.arange(half, dtype=F32) / half))
    ang = jnp.arange(s, dtype=F32)[:, None] * freqs[None, :]
    cos, sin = jnp.cos(ang), jnp.sin(ang)
    zeros = lambda n: jnp.zeros((s, n), F32)
    cos_t = jnp.concatenate([jnp.ones((s, QK_NOPE), F32), cos, cos, zeros(LANES - QK_HEAD)], axis=1)
    sin_t = jnp.concatenate([zeros(QK_NOPE), -sin, sin, zeros(LANES - QK_HEAD)], axis=1)
    lane = np.arange(LANES)
    partner = np.where((lane >= QK_NOPE) & (lane < QK_NOPE + half), lane + half,
                       np.where((lane >= QK_NOPE + half) & (lane < QK_HEAD), lane - half, -1))
    perm = jnp.asarray(lane[:, None] == partner[None, :], BF16)
    ones = jnp.ones((LANES, LANES), BF16)
    v_one = jnp.tile((jnp.arange(LANES) == V_HEAD).astype(F32), N_HEADS).reshape(1, -1)
    row = lambda g: g.reshape(1, -1)

    const = lambda shape: pl.BlockSpec(shape, lambda bi, si: (0,) * len(shape))
    tab = pl.BlockSpec((tm, LANES), lambda bi, si: (si, 0))
    wide = N_HEADS * LANES
    return pl.pallas_call(
        _proj_kernel,
        grid=(b, s // tm),
        in_specs=[
            pl.BlockSpec((1, tm, D_MODEL), lambda bi, si: (bi, si, 0)),
            const((1, D_MODEL)), const(w_in_p.shape), const((1, Q_LORA)), const(w_uq_p.shape),
            const((1, KV_LORA)), const(w_uk_p.shape), const(w_uv_p.shape),
            const((1, LANES)), const((1, LANES)), tab, tab, const((LANES, LANES)),
            const((LANES, LANES)), const((1, wide)),
            const((FNET_W, 2 * FNET_W)),
        ],
        out_specs=[
            pl.BlockSpec((1, tm, wide), lambda bi, si: (bi, si, 0)),
            pl.BlockSpec((1, wide, tm), lambda bi, si: (bi, 0, si)),
            pl.BlockSpec((1, tm, wide), lambda bi, si: (bi, si, 0)),
            pl.BlockSpec((1, tm, 2 * FNET_W), lambda bi, si: (bi, si, 0)),
        ],
        out_shape=[
            jax.ShapeDtypeStruct((b, s, wide), BF16),
            jax.ShapeDtypeStruct((b, wide, s), BF16),
            jax.ShapeDtypeStruct((b, s, wide), BF16),
            jax.ShapeDtypeStruct((b, s, 2 * FNET_W), BF16),
        ],
        compiler_params=pltpu.CompilerParams(
            dimension_semantics=("arbitrary", "arbitrary"), vmem_limit_bytes=VMEM_LIMIT),
        name="input_projection",
    )(x, row(attn_norm_g), w_in_p, row(q_lat_g), w_uq_p, row(kv_lat_g), w_uk_p, w_uv_p,
      g_qh, g_kh, cos_t, sin_t, ones, perm, v_one, _channel_dft(s))


def _mix_kernel(x_ref, a_ref, fm_ref, g_a_ref, g_f_ref, w_a_ref, w_f_ref, g_ffn_ref,
                x1_ref, hn_ref):
    a = (_row_rms(a_ref[...], ATTN_W) * g_a_ref[...]).astype(BF16)
    fm = (_row_rms(fm_ref[...], FNET_W) * g_f_ref[...]).astype(BF16)
    x1 = (x_ref[...] + jnp.dot(a, w_a_ref[...], preferred_element_type=F32)
          + jnp.dot(fm, w_f_ref[...], preferred_element_type=F32))
    x1_ref[...] = x1
    hn_ref[...] = (_row_rms(x1, D_MODEL) * g_ffn_ref[...]).astype(BF16)


def _mix(x, a, fm, attn_out_g, fnet_out_g, w_out, ffn_norm_g, *, tm=512):
    t = x.shape[0]
    row = lambda g: g.reshape(1, -1)
    tok = lambda w: pl.BlockSpec((tm, w), lambda ti: (ti, 0))
    const = lambda shape: pl.BlockSpec(shape, lambda ti: (0, 0))
    return pl.pallas_call(
        _mix_kernel,
        grid=(t // tm,),
        in_specs=[tok(D_MODEL), tok(ATTN_W), tok(FNET_W), const((1, ATTN_W)), const((1, FNET_W)),
                  const((ATTN_W, D_MODEL)), const((FNET_W, D_MODEL)), const((1, D_MODEL))],
        out_specs=[tok(D_MODEL), tok(D_MODEL)],
        out_shape=[jax.ShapeDtypeStruct((t, D_MODEL), F32), jax.ShapeDtypeStruct((t, D_MODEL), BF16)],
        compiler_params=pltpu.CompilerParams(
            dimension_semantics=("arbitrary",), vmem_limit_bytes=VMEM_LIMIT),
        name="output_mix",
    )(x, a, fm, row(attn_out_g), row(fnet_out_g), w_out[:ATTN_W].astype(BF16),
      w_out[ATTN_W:].astype(BF16), row(ffn_norm_g))


FFT_S1 = 64


def _dft(n):
    ang = -2.0 * np.pi * np.outer(np.arange(n), np.arange(n)) / n
    return np.cos(ang), np.sin(ang)


def _channel_dft(s):
    c, si = _dft(FNET_CH)
    eye = np.eye(FNET_GROUPS)
    scale = (FNET_CH * s) ** -0.5
    return jnp.asarray(np.concatenate([np.kron(eye, c), np.kron(eye, si)], axis=1) * scale, BF16)


def _fft_stage1_kernel(x_ref, dr_ref, di_ref, twr_ref, twi_ref, o_ref):
    x = x_ref[0]
    yr = jnp.dot(dr_ref[...], x, preferred_element_type=F32)
    yi = jnp.dot(di_ref[...], x, preferred_element_type=F32)
    w = FNET_W
    for j in range(x.shape[1] // (2 * w)):
        re, im = slice(2 * j * w, (2 * j + 1) * w), slice((2 * j + 1) * w, (2 * j + 2) * w)
        ar = yr[:, re] - yi[:, im]
        ai = yr[:, im] + yi[:, re]
        tw = slice(j * LANES, (j + 1) * LANES)
        twr = jnp.concatenate([twr_ref[:, tw]] * (w // LANES), axis=1)
        twi = jnp.concatenate([twi_ref[:, tw]] * (w // LANES), axis=1)
        o_ref[0, :, re] = (ar * twr - ai * twi).astype(BF16)
        o_ref[0, :, im] = (ar * twi + ai * twr).astype(BF16)


def _fft_stage2_kernel(a_ref, m_ref, o_ref):
    w = FNET_W
    for j in range(a_ref.shape[1]):
        xa = a_ref[0, j]
        stacked = jnp.concatenate([xa[:, :w], xa[:, w:]], axis=0)
        o_ref[0, j] = jnp.dot(m_ref[...], stacked, preferred_element_type=F32)


def _fnet(fx, *, tn2=8, kb=8):
    b, s, wide = fx.shape
    s1, s2 = FFT_S1, s // FFT_S1
    d1r, d1i = _dft(s1)
    d2r, d2i = _dft(s2)
    ang = -2.0 * np.pi / s * jnp.outer(jnp.arange(s1, dtype=F32), jnp.arange(s2, dtype=F32))
    twr = jnp.repeat(jnp.cos(ang), LANES, axis=1)
    twi = jnp.repeat(jnp.sin(ang), LANES, axis=1)
    cols = tn2 * wide
    x2 = fx.reshape(b, s1, s2 * wide)
    a = pl.pallas_call(
        _fft_stage1_kernel,
        grid=(b, s2 // tn2),
        in_specs=[
            pl.BlockSpec((1, s1, cols), lambda bi, ci: (bi, 0, ci)),
            pl.BlockSpec((s1, s1), lambda bi, ci: (0, 0)),
            pl.BlockSpec((s1, s1), lambda bi, ci: (0, 0)),
            pl.BlockSpec((s1, tn2 * LANES), lambda bi, ci: (0, ci)),
            pl.BlockSpec((s1, tn2 * LANES), lambda bi, ci: (0, ci)),
        ],
        out_specs=pl.BlockSpec((1, s1, cols), lambda bi, ci: (bi, 0, ci)),
        out_shape=jax.ShapeDtypeStruct((b, s1, s2 * wide), BF16),
        compiler_params=pltpu.CompilerParams(
            dimension_semantics=("arbitrary", "arbitrary"), vmem_limit_bytes=VMEM_LIMIT),
        name="fft_stage1",
    )(x2, jnp.asarray(d1r, BF16), jnp.asarray(d1i, BF16), twr, twi)
    a4 = a.reshape(b, s1, s2, wide)
    m2 = jnp.asarray(np.concatenate([d2r, -d2i], axis=1), BF16)
    y = pl.pallas_call(
        _fft_stage2_kernel,
        grid=(b, s1 // kb),
        in_specs=[
            pl.BlockSpec((1, kb, s2, wide), lambda bi, ki: (bi, ki, 0, 0)),
            pl.BlockSpec((s2, 2 * s2), lambda bi, ki: (0, 0)),
        ],
        out_specs=pl.BlockSpec((1, kb, s2, FNET_W), lambda bi, ki: (bi, ki, 0, 0)),
        out_shape=jax.ShapeDtypeStruct((b, s1, s2, FNET_W), F32),
        compiler_params=pltpu.CompilerParams(
            dimension_semantics=("arbitrary", "arbitrary"), vmem_limit_bytes=VMEM_LIMIT),
        name="fft_stage2",
    )(a4, m2)
    return y.transpose(0, 2, 1, 3).reshape(b, s, FNET_W)


def kernel(x_prompt, x_sample, attn_norm_g, w_in, q_lat_g, w_uq, kv_lat_g, w_ukv, q_head_g,
           k_head_g, attn_out_g, fnet_out_g, w_out, ffn_norm_g, peer_w_q, peer_sub_keys,
           peer_u, peer_v):
    l = 0
    keys = peer_sub_keys[l].reshape(2 * PEER_HEADS, PEER_NKEYS, PEER_HALF).astype(BF16)
    w_pq, u, v_tab = peer_w_q[l].astype(BF16), peer_u[l].astype(BF16), peer_v[l].astype(BF16)
    outs = []
    for x in (x_prompt, x_sample):
        q, kt, v, f = _project(x, attn_norm_g[l], w_in[l], q_lat_g[l], w_uq[l], kv_lat_g[l],
                               w_ukv[l], q_head_g[l], k_head_g[l])
        a = _flash_attention(q, kt, v).reshape(-1, ATTN_W)
        fm = _fnet(f).reshape(-1, FNET_W)
        x1, hn = _mix(x.reshape(-1, D_MODEL), a, fm, attn_out_g[l], fnet_out_g[l], w_out[l],
                      ffn_norm_g[l])
        rank2t, e2t, n1t, e1t = _peer_gate(hn, w_pq, keys)
        outs.append(_peer_dense(hn, u, v_tab, n1t, e1t, rank2t, e2t, x1).reshape(x.shape))
    return tuple(outs)
```

```python
import functools

import jax
import jax.numpy as jnp
import numpy as np
from jax import lax
from jax.experimental import pallas as pl
from jax.experimental.pallas import tpu as pltpu

D_MODEL = 1024
N_HEADS = 8
QK_NOPE = 64
QK_ROPE = 32
QK_HEAD = QK_NOPE + QK_ROPE
V_HEAD = 64
Q_LORA = 384
KV_LORA = 256
ATTN_W = N_HEADS * V_HEAD
ROPE_THETA = 10000.0
FNET_W = D_MODEL - ATTN_W
FNET_GROUPS = 4
FNET_CH = FNET_W // FNET_GROUPS
PEER_HEADS = 8
PEER_NKEYS = 128
PEER_EXPERTS = PEER_NKEYS * PEER_NKEYS
PEER_HALF = 128
PEER_TOPK = 16
EPS = 1e-6
LOG2_E = 1.4426950408889634

LANES = 128
BF16_ROWS = 16
GATE_GROUP = 4
VMEM_LIMIT = 48 * 1024 * 1024

F32 = jnp.float32
BF16 = jnp.bfloat16


def _flash_kernel(q_ref, kt_ref, v_ref, o_ref, m_scr, acc_scr, *, tk):
    seq = v_ref.shape[1]
    tq = q_ref.shape[1]
    nkv = seq // tk
    m_scr[...] = jnp.full(m_scr.shape, -jnp.inf, F32)
    acc_scr[...] = jnp.zeros(acc_scr.shape, F32)

    def body(j, carry):
        start = pl.multiple_of(j * tk, tk)
        heads = [slice(hh * LANES, (hh + 1) * LANES) for hh in range(2)]
        scores = [jnp.dot(q_ref[0, :, hs], kt_ref[0, hs, pl.ds(start, tk)],
                          preferred_element_type=F32) for hs in heads]
        for hh, hs in enumerate(heads):
            cols = [scores[hh][:, c * LANES:(c + 1) * LANES] for c in range(tk // LANES)]
            m_old = m_scr[hh]
            m_new = jnp.maximum(m_old, jnp.max(functools.reduce(jnp.maximum, cols),
                                               axis=1, keepdims=True))
            p = jnp.concatenate([jnp.exp2(c - m_new) for c in cols], axis=1).astype(BF16)
            pv = jnp.dot(p, v_ref[0, pl.ds(start, tk), hs], preferred_element_type=F32)
            acc_scr[hh] = jnp.exp2(m_old - m_new) * acc_scr[hh] + pv
            m_scr[hh] = m_new
        return carry

    lax.fori_loop(0, nkv, body, 0, unroll=2)
    outs = []
    for hh in range(2):
        acc = acc_scr[hh]
        outs.append(acc[:, :V_HEAD] / acc[:, V_HEAD:V_HEAD + 1])
    o_ref[0] = jnp.concatenate(outs, axis=1)


def _flash_attention(q, kt, v, *, tq=512, tk=2048):
    b, s, _ = q.shape
    grid = (b, N_HEADS // 2, s // tq)
    return pl.pallas_call(
        functools.partial(_flash_kernel, tk=tk),
        grid=grid,
        in_specs=[
            pl.BlockSpec((1, tq, 2 * LANES), lambda bi, hp, qi: (bi, qi, hp)),
            pl.BlockSpec((1, 2 * LANES, s), lambda bi, hp, qi: (bi, hp, 0)),
            pl.BlockSpec((1, s, 2 * LANES), lambda bi, hp, qi: (bi, 0, hp)),
        ],
        out_specs=pl.BlockSpec((1, tq, LANES), lambda bi, hp, qi: (bi, qi, hp)),
        out_shape=jax.ShapeDtypeStruct((b, s, ATTN_W), F32),
        scratch_shapes=[pltpu.VMEM((2, tq, LANES), F32), pltpu.VMEM((2, tq, LANES), F32)],
        compiler_params=pltpu.CompilerParams(
            dimension_semantics=("arbitrary", "arbitrary", "arbitrary"),
            vmem_limit_bytes=VMEM_LIMIT),
        name="flash_attention",
    )(q, kt, v)


_CAND_PAIRS = tuple((a, b) for a in range(PEER_TOPK) for b in range(PEER_TOPK)
                    if (a + 1) * (b + 1) <= PEER_TOPK)
_NEG_INF = float("-inf")


def _tree_max(xs):
    xs = list(xs)
    while len(xs) > 1:
        nxt = [jnp.maximum(xs[i], xs[i + 1]) for i in range(0, len(xs) - 1, 2)]
        if len(xs) % 2:
            nxt.append(xs[-1])
        xs = nxt
    return xs[0]


def _tree_min(xs):
    xs = list(xs)
    while len(xs) > 1:
        nxt = [jnp.minimum(xs[i], xs[i + 1]) for i in range(0, len(xs) - 1, 2)]
        if len(xs) % 2:
            nxt.append(xs[-1])
        xs = nxt
    return xs[0]


def _sort_network(n):
    pairs = []
    p = 1
    while p < n:
        k = p
        while k >= 1:
            for j in range(k % p, n - k, 2 * k):
                for i in range(min(k, n - j - k)):
                    if (i + j) // (2 * p) == (i + j + k) // (2 * p):
                        pairs.append((i + j, i + j + k))
            k //= 2
        p *= 2
    return tuple(pairs)


_SUBLANES = 8
_SORT_ROWS = _sort_network(PEER_NKEYS // _SUBLANES)


def _top_values(scores):
    rows = [scores[r * _SUBLANES:(r + 1) * _SUBLANES, :] for r in range(PEER_NKEYS // _SUBLANES)]
    for a, b in _SORT_ROWS:
        rows[a], rows[b] = jnp.maximum(rows[a], rows[b]), jnp.minimum(rows[a], rows[b])
    vals = []
    for t in range(PEER_TOPK):
        m = jnp.max(rows[0], axis=0, keepdims=True)
        vals.append(m)
        if t + 1 < PEER_TOPK:
            hit = rows[0] == m
            for r in range(PEER_TOPK - 1 - t):
                rows[r] = jnp.where(hit, rows[r + 1], rows[r])
    return vals


def _gate_kernel(hn_ref, wq_ref, keys_ref, rank2_ref, e2_ref, n1_ref, e1_ref, s_scr, vals_scr):
    tm = hn_ref.shape[0]
    ncol = tm // LANES
    q = jnp.dot(hn_ref[...], wq_ref[...], preferred_element_type=F32).astype(BF16)
    for hc in range(2 * PEER_HEADS):
        s_scr[hc] = lax.dot_general(keys_ref[hc], q[:, hc * PEER_HALF:(hc + 1) * PEER_HALF],
                                    (((1,), (1,)), ((), ())), preferred_element_type=F32)

    for hc in range(2 * PEER_HEADS):
        h, c = divmod(hc, 2)

        def col_body(col, carry, hc=hc, h=h, c=c):
            cs = pl.ds(pl.multiple_of(col * LANES, LANES), LANES)
            cur = s_scr[hc, :, cs]
            vals = _top_values(cur)
            for r in range(PEER_TOPK):
                vals_scr[c, r, h:h + 1, cs] = vals[r]
            if c == 1:
                rank = jnp.zeros((PEER_NKEYS, LANES), F32)
                for r in range(PEER_TOPK):
                    rank = rank + jnp.where(cur < vals[r], 1.0, 0.0)
                rank2_ref[h, :, cs] = rank
            return carry

        lax.fori_loop(0, ncol, col_body, 0, unroll=True)

    def fin_body(col, carry):
        cs = pl.ds(pl.multiple_of(col * LANES, LANES), LANES)
        v1 = [vals_scr[0, a, :, cs] for a in range(PEER_TOPK)]
        v2 = [vals_scr[1, b, :, cs] for b in range(PEER_TOPK)]
        cands = [v1[a] + v2[b] for a, b in _CAND_PAIRS]
        top = cands[0]
        z = jnp.zeros_like(top)
        m = top
        for r in range(PEER_TOPK):
            m = _tree_max(cands)
            z = z + jnp.exp(m - top)
            if r + 1 < PEER_TOPK:
                cands = [jnp.where(cd == m, _NEG_INF, cd) for cd in cands]
        thr = m
        rz = 1.0 / z
        cnt = [sum(jnp.where(v1[a] + v2[b] >= thr, 1.0, 0.0) for aa, b in _CAND_PAIRS if aa == a)
               for a in range(PEER_TOPK)]
        bound = [_tree_min([jnp.where(cnt[a] >= float(k), v1[a], float("inf"))
                            for a in range(PEER_TOPK // k)])
                 for k in range(1, PEER_TOPK + 1)]
        for h in range(PEER_HEADS):
            s1 = s_scr[2 * h, :, cs]
            n1 = jnp.zeros((PEER_NKEYS, LANES), F32)
            for k in range(PEER_TOPK):
                n1 = n1 + jnp.where(s1 >= bound[k][h:h + 1, :], 1.0, 0.0)
            n1_ref[h, :, cs] = n1
            e1_ref[h, :, cs] = jnp.exp(s1 - v1[0][h:h + 1, :])
            s2 = s_scr[2 * h + 1, :, cs]
            e2_ref[h, :, cs] = jnp.exp(s2 - v2[0][h:h + 1, :]) * rz[h:h + 1, :]
        return carry

    lax.fori_loop(0, ncol, fin_body, 0)


def _peer_gate(hn, wq, keys, *, tm=256):
    t = hn.shape[0]
    out = jax.ShapeDtypeStruct((PEER_HEADS, PEER_NKEYS, t), F32)
    ospec = pl.BlockSpec((PEER_HEADS, PEER_NKEYS, tm), lambda ti: (0, 0, ti))
    return pl.pallas_call(
        _gate_kernel,
        grid=(t // tm,),
        in_specs=[
            pl.BlockSpec((tm, D_MODEL), lambda ti: (ti, 0)),
            pl.BlockSpec((D_MODEL, 2 * PEER_HEADS * PEER_HALF), lambda ti: (0, 0)),
            pl.BlockSpec((2 * PEER_HEADS, PEER_NKEYS, PEER_HALF), lambda ti: (0, 0, 0)),
        ],
        out_specs=[ospec, ospec, ospec, ospec],
        out_shape=[out, out, out, out],
        scratch_shapes=[pltpu.VMEM((2 * PEER_HEADS, PEER_NKEYS, tm), F32),
                        pltpu.VMEM((2, PEER_TOPK, PEER_HEADS, tm), F32)],
        compiler_params=pltpu.CompilerParams(
            dimension_semantics=("arbitrary",), vmem_limit_bytes=VMEM_LIMIT),
        name="peer_gate",
    )(hn, wq, keys)


def _gelu_tanh(x):
    k1 = float(np.sqrt(2.0 / np.pi))
    k2 = k1 * 0.044715
    hx = 0.5 * x
    return hx + hx * jnp.tanh(x * (k1 + k2 * (x * x)))


def _peer_kernel(hn_ref, u_ref, vt_ref, n1_ref, e1_ref, rank2_ref, e2_ref, x_ref,
                 y_ref, acc_ref, a_scr, g_scr, r2_scr, e2_scr, *, n_i1):
    eb = pl.program_id(1)
    tm = hn_ref.shape[0]
    rows = PEER_NKEYS // BF16_ROWS

    @pl.when(eb == 0)
    def _():
        acc_ref[...] = jnp.zeros_like(acc_ref)
        r2_scr[...] = rank2_ref[...].astype(BF16)
        e2_scr[...] = e2_ref[...].astype(BF16)

    a_scr[...] = lax.dot_general(u_ref[...], hn_ref[...], (((1,), (1,)), ((), ())),
                                 preferred_element_type=F32).astype(BF16)

    for c in range(tm // LANES):
        cs = slice(c * LANES, (c + 1) * LANES)
        for j0 in range(0, n_i1, GATE_GROUP):
            group = range(j0, j0 + GATE_GROUP)
            gs = {j: jnp.zeros((rows, BF16_ROWS, LANES), BF16) for j in group}
            for h in range(PEER_HEADS):
                r2 = r2_scr[h, :, cs].reshape(rows, BF16_ROWS, LANES)
                e2 = e2_scr[h, :, cs].reshape(rows, BF16_ROWS, LANES)
                for j in group:
                    n1row = jnp.broadcast_to(n1_ref[h, 0, j:j + 1, cs], (BF16_ROWS, LANES)).astype(BF16)
                    e1row = jnp.broadcast_to(e1_ref[h, 0, j:j + 1, cs], (BF16_ROWS, LANES)).astype(BF16)
                    gs[j] = gs[j] + jnp.where(r2 < n1row[None], e2, jnp.zeros_like(e2)) * e1row[None]
            for j in group:
                g_scr[j * PEER_NKEYS:(j + 1) * PEER_NKEYS, cs] = gs[j].reshape(PEER_NKEYS, LANES)

    p = _gelu_tanh(a_scr[...]) * g_scr[...]
    acc_ref[...] += jnp.dot(vt_ref[0], p, preferred_element_type=F32)

    @pl.when(eb == pl.num_programs(1) - 1)
    def _():
        y_ref[...] = x_ref[...] + acc_ref[...].T


def _peer_dense(hn, u, v, n1t, e1t, rank2t, e2t, x1, *, tm=512, n_i1=16):
    t = hn.shape[0]
    te = n_i1 * PEER_NKEYS
    vt = v.reshape(PEER_EXPERTS // te, te, D_MODEL).transpose(0, 2, 1)
    nblk = PEER_NKEYS // n_i1
    n1r = n1t.reshape(PEER_HEADS, nblk, n_i1, t)
    e1r = e1t.reshape(PEER_HEADS, nblk, n_i1, t)
    return pl.pallas_call(
        functools.partial(_peer_kernel, n_i1=n_i1),
        grid=(t // tm, nblk),
        in_specs=[
            pl.BlockSpec((tm, D_MODEL), lambda ti, eb: (ti, 0)),
            pl.BlockSpec((te, D_MODEL), lambda ti, eb: (eb, 0)),
            pl.BlockSpec((1, D_MODEL, te), lambda ti, eb: (eb, 0, 0)),
            pl.BlockSpec((PEER_HEADS, 1, n_i1, tm), lambda ti, eb: (0, eb, 0, ti)),
            pl.BlockSpec((PEER_HEADS, 1, n_i1, tm), lambda ti, eb: (0, eb, 0, ti)),
            pl.BlockSpec((PEER_HEADS, PEER_NKEYS, tm), lambda ti, eb: (0, 0, ti)),
            pl.BlockSpec((PEER_HEADS, PEER_NKEYS, tm), lambda ti, eb: (0, 0, ti)),
            pl.BlockSpec((tm, D_MODEL), lambda ti, eb: (ti, 0)),
        ],
        out_specs=pl.BlockSpec((tm, D_MODEL), lambda ti, eb: (ti, 0)),
        out_shape=jax.ShapeDtypeStruct((t, D_MODEL), F32),
        scratch_shapes=[pltpu.VMEM((D_MODEL, tm), F32),
                        pltpu.VMEM((te, tm), BF16), pltpu.VMEM((te, tm), BF16),
                        pltpu.VMEM((PEER_HEADS, PEER_NKEYS, tm), BF16),
                        pltpu.VMEM((PEER_HEADS, PEER_NKEYS, tm), BF16)],
        compiler_params=pltpu.CompilerParams(
            dimension_semantics=("arbitrary", "arbitrary"),
            vmem_limit_bytes=VMEM_LIMIT),
        name="peer_dense",
    )(hn, u, vt, n1r, e1r, rank2t, e2t, x1)


def _row_rms(x, width):
    return x * lax.rsqrt(jnp.sum(x * x, axis=-1, keepdims=True) * (1.0 / width) + EPS)


def _proj_kernel(x_ref, g_in_ref, w_in_ref, g_q_ref, w_uq_ref, g_kv_ref, w_uk_ref, w_uv_ref,
                 g_qh_ref, g_kh_ref, cos_ref, sin_ref, ones_ref, perm_ref, v_one_ref, dft_c_ref,
                 q_ref, kt_ref, v_ref, f_ref):
    o0, o1, o2 = Q_LORA, Q_LORA + KV_LORA, Q_LORA + KV_LORA + LANES
    h = (_row_rms(x_ref[0], D_MODEL) * g_in_ref[...]).astype(BF16)
    p = jnp.dot(h, w_in_ref[...], preferred_element_type=F32)
    c_q = (_row_rms(p[:, :o0], Q_LORA) * g_q_ref[...]).astype(BF16)
    c_kv = (_row_rms(p[:, o0:o1], KV_LORA) * g_kv_ref[...]).astype(BF16)
    k_rope = p[:, o1:o2]
    f_ref[0] = jnp.dot(p[:, o2:].astype(BF16), dft_c_ref[...],
                       preferred_element_type=F32).astype(BF16)
    q = jnp.dot(c_q, w_uq_ref[...], preferred_element_type=F32)
    k = jnp.dot(c_kv, w_uk_ref[...], preferred_element_type=F32)
    v = jnp.dot(c_kv, w_uv_ref[...], preferred_element_type=F32)
    v_ref[0] = (v + v_one_ref[...]).astype(BF16)

    cos, sin = cos_ref[...], sin_ref[...]

    def select_sum(val, mat):
        return jnp.dot(val.astype(BF16), mat, preferred_element_type=F32)

    def head_norm_rope(slot, gain):
        ss = select_sum(slot * slot, ones_ref[...])
        y = slot * lax.rsqrt(ss * (1.0 / QK_HEAD) + EPS) * gain
        return y * cos + select_sum(y, perm_ref[...]) * sin

    q_slots, k_slots = [], []
    for hd in range(N_HEADS):
        hs = slice(hd * LANES, (hd + 1) * LANES)
        q_slots.append(head_norm_rope(q[:, hs], g_qh_ref[...]))
        k_slots.append(head_norm_rope(k[:, hs] + k_rope, g_kh_ref[...]))
    q_ref[0] = jnp.concatenate(q_slots, axis=1).astype(BF16)
    kt_ref[0] = jnp.concatenate(k_slots, axis=1).T.astype(BF16)


def _slot_cols(w, width, n_in):
    w = w.reshape(n_in, N_HEADS, width)
    return jnp.pad(w, ((0, 0), (0, 0), (0, LANES - width))).reshape(n_in, N_HEADS * LANES)


def _project(x, attn_norm_g, w_in, q_lat_g, w_uq, kv_lat_g, w_ukv, q_head_g, k_head_g, *, tm=512):
    b, s, _ = x.shape
    o0, o1, o2 = Q_LORA, Q_LORA + KV_LORA, Q_LORA + KV_LORA + QK_ROPE
    k_rope_cols = jnp.pad(w_in[:, o1:o2], ((0, 0), (QK_NOPE, LANES - QK_HEAD)))
    w_in_p = jnp.concatenate([w_in[:, :o1], k_rope_cols, w_in[:, o2:]], axis=1).astype(BF16)
    w_uq_p = _slot_cols(w_uq, QK_HEAD, Q_LORA).astype(BF16)
    w_ukv_h = w_ukv.reshape(KV_LORA, N_HEADS, QK_NOPE + V_HEAD)
    w_uk_p = _slot_cols(w_ukv_h[:, :, :QK_NOPE].reshape(KV_LORA, -1), QK_NOPE, KV_LORA).astype(BF16)
    w_uv_p = _slot_cols(w_ukv_h[:, :, QK_NOPE:].reshape(KV_LORA, -1), V_HEAD, KV_LORA).astype(BF16)
    pad_gain = lambda g: jnp.pad(g, (0, LANES - QK_HEAD)).reshape(1, LANES)
    g_qh = pad_gain(q_head_g) * (QK_HEAD ** -0.5 * LOG2_E)
    g_kh = pad_gain(k_head_g)
    half = QK_ROPE // 2
    freqs = 1.0 / (ROPE_THETA ** (jnp.arange(half, dtype=F32) / half))
    ang = jnp.arange(s, dtype=F32)[:, None] * freqs[None, :]
    cos, sin = jnp.cos(ang), jnp.sin(ang)
    zeros = lambda n: jnp.zeros((s, n), F32)
    cos_t = jnp.concatenate([jnp.ones((s, QK_NOPE), F32), cos, cos, zeros(LANES - QK_HEAD)], axis=1)
    sin_t = jnp.concatenate([zeros(QK_NOPE), -sin, sin, zeros(LANES - QK_HEAD)], axis=1)
    lane = np.arange(LANES)
    partner = np.where((lane >= QK_NOPE) & (lane < QK_NOPE + half), lane + half,
                       np.where((lane >= QK_NOPE + half) & (lane < QK_HEAD), lane - half, -1))
    perm = jnp.asarray(lane[:, None] == partner[None, :], BF16)
    ones = jnp.ones((LANES, LANES), BF16)
    v_one = jnp.tile((jnp.arange(LANES) == V_HEAD).astype(F32), N_HEADS).reshape(1, -1)
    row = lambda g: g.reshape(1, -1)

    const = lambda shape: pl.BlockSpec(shape, lambda bi, si: (0,) * len(shape))
    tab = pl.BlockSpec((tm, LANES), lambda bi, si: (si, 0))
    wide = N_HEADS * LANES
    return pl.pallas_call(
        _proj_kernel,
        grid=(b, s // tm),
        in_specs=[
            pl.BlockSpec((1, tm, D_MODEL), lambda bi, si: (bi, si, 0)),
            const((1, D_MODEL)), const(w_in_p.shape), const((1, Q_LORA)), const(w_uq_p.shape),
            const((1, KV_LORA)), const(w_uk_p.shape), const(w_uv_p.shape),
            const((1, LANES)), const((1, LANES)), tab, tab, const((LANES, LANES)),
            const((LANES, LANES)), const((1, wide)),
            const((FNET_W, 2 * FNET_W)),
        ],
        out_specs=[
            pl.BlockSpec((1, tm, wide), lambda bi, si: (bi, si, 0)),
            pl.BlockSpec((1, wide, tm), lambda bi, si: (bi, 0, si)),
            pl.BlockSpec((1, tm, wide), lambda bi, si: (bi, si, 0)),
            pl.BlockSpec((1, tm, 2 * FNET_W), lambda bi, si: (bi, si, 0)),
        ],
        out_shape=[
            jax.ShapeDtypeStruct((b, s, wide), BF16),
            jax.ShapeDtypeStruct((b, wide, s), BF16),
            jax.ShapeDtypeStruct((b, s, wide), BF16),
            jax.ShapeDtypeStruct((b, s, 2 * FNET_W), BF16),
        ],
        compiler_params=pltpu.CompilerParams(
            dimension_semantics=("arbitrary", "arbitrary"), vmem_limit_bytes=VMEM_LIMIT),
        name="input_projection",
    )(x, row(attn_norm_g), w_in_p, row(q_lat_g), w_uq_p, row(kv_lat_g), w_uk_p, w_uv_p,
      g_qh, g_kh, cos_t, sin_t, ones, perm, v_one, _channel_dft(s))


def _mix_kernel(x_ref, a_ref, fm_ref, g_a_ref, g_f_ref, w_a_ref, w_f_ref, g_ffn_ref,
                x1_ref, hn_ref):
    a = (_row_rms(a_ref[...], ATTN_W) * g_a_ref[...]).astype(BF16)
    fm = (_row_rms(fm_ref[...], FNET_W) * g_f_ref[...]).astype(BF16)
    x1 = (x_ref[...] + jnp.dot(a, w_a_ref[...], preferred_element_type=F32)
          + jnp.dot(fm, w_f_ref[...], preferred_element_type=F32))
    x1_ref[...] = x1
    hn_ref[...] = (_row_rms(x1, D_MODEL) * g_ffn_ref[...]).astype(BF16)


def _mix(x, a, fm, attn_out_g, fnet_out_g, w_out, ffn_norm_g, *, tm=512):
    t = x.shape[0]
    row = lambda g: g.reshape(1, -1)
    tok = lambda w: pl.BlockSpec((tm, w), lambda ti: (ti, 0))
    const = lambda shape: pl.BlockSpec(shape, lambda ti: (0, 0))
    return pl.pallas_call(
        _mix_kernel,
        grid=(t // tm,),
        in_specs=[tok(D_MODEL), tok(ATTN_W), tok(FNET_W), const((1, ATTN_W)), const((1, FNET_W)),
                  const((ATTN_W, D_MODEL)), const((FNET_W, D_MODEL)), const((1, D_MODEL))],
        out_specs=[tok(D_MODEL), tok(D_MODEL)],
        out_shape=[jax.ShapeDtypeStruct((t, D_MODEL), F32), jax.ShapeDtypeStruct((t, D_MODEL), BF16)],
        compiler_params=pltpu.CompilerParams(
            dimension_semantics=("arbitrary",), vmem_limit_bytes=VMEM_LIMIT),
        name="output_mix",
    )(x, a, fm, row(attn_out_g), row(fnet_out_g), w_out[:ATTN_W].astype(BF16),
      w_out[ATTN_W:].astype(BF16), row(ffn_norm_g))


FFT_S1 = 64


def _dft(n):
    ang = -2.0 * np.pi * np.outer(np.arange(n), np.arange(n)) / n
    return np.cos(ang), np.sin(ang)


def _channel_dft(s):
    c, si = _dft(FNET_CH)
    eye = np.eye(FNET_GROUPS)
    scale = (FNET_CH * s) ** -0.5
    return jnp.asarray(np.concatenate([np.kron(eye, c), np.kron(eye, si)], axis=1) * scale, BF16)


def _fft_stage1_kernel(x_ref, dr_ref, di_ref, twr_ref, twi_ref, o_ref):
    x = x_ref[0]
    yr = jnp.dot(dr_ref[...], x, preferred_element_type=F32)
    yi = jnp.dot(di_ref[...], x, preferred_element_type=F32)
    w = FNET_W
    for j in range(x.shape[1] // (2 * w)):
        re, im = slice(2 * j * w, (2 * j + 1) * w), slice((2 * j + 1) * w, (2 * j + 2) * w)
        ar = yr[:, re] - yi[:, im]
        ai = yr[:, im] + yi[:, re]
        tw = slice(j * LANES, (j + 1) * LANES)
        twr = jnp.concatenate([twr_ref[:, tw]] * (w // LANES), axis=1)
        twi = jnp.concatenate([twi_ref[:, tw]] * (w // LANES), axis=1)
        o_ref[0, :, re] = (ar * twr - ai * twi).astype(BF16)
        o_ref[0, :, im] = (ar * twi + ai * twr).astype(BF16)


def _fft_stage2_kernel(a_ref, m_ref, o_ref):
    w = FNET_W
    for j in range(a_ref.shape[1]):
        xa = a_ref[0, j]
        stacked = jnp.concatenate([xa[:, :w], xa[:, w:]], axis=0)
        o_ref[0, j] = jnp.dot(m_ref[...], stacked, preferred_element_type=F32)


def _fnet(fx, *, tn2=8, kb=8):
    b, s, wide = fx.shape
    s1, s2 = FFT_S1, s // FFT_S1
    d1r, d1i = _dft(s1)
    d2r, d2i = _dft(s2)
    ang = -2.0 * np.pi / s * jnp.outer(jnp.arange(s1, dtype=F32), jnp.arange(s2, dtype=F32))
    twr = jnp.repeat(jnp.cos(ang), LANES, axis=1)
    twi = jnp.repeat(jnp.sin(ang), LANES, axis=1)
    cols = tn2 * wide
    x2 = fx.reshape(b, s1, s2 * wide)
    a = pl.pallas_call(
        _fft_stage1_kernel,
        grid=(b, s2 // tn2),
        in_specs=[
            pl.BlockSpec((1, s1, cols), lambda bi, ci: (bi, 0, ci)),
            pl.BlockSpec((s1, s1), lambda bi, ci: (0, 0)),
            pl.BlockSpec((s1, s1), lambda bi, ci: (0, 0)),
            pl.BlockSpec((s1, tn2 * LANES), lambda bi, ci: (0, ci)),
            pl.BlockSpec((s1, tn2 * LANES), lambda bi, ci: (0, ci)),
        ],
        out_specs=pl.BlockSpec((1, s1, cols), lambda bi, ci: (bi, 0, ci)),
        out_shape=jax.ShapeDtypeStruct((b, s1, s2 * wide), BF16),
        compiler_params=pltpu.CompilerParams(
            dimension_semantics=("arbitrary", "arbitrary"), vmem_limit_bytes=VMEM_LIMIT),
        name="fft_stage1",
    )(x2, jnp.asarray(d1r, BF16), jnp.asarray(d1i, BF16), twr, twi)
    a4 = a.reshape(b, s1, s2, wide)
    m2 = jnp.asarray(np.concatenate([d2r, -d2i], axis=1), BF16)
    y = pl.pallas_call(
        _fft_stage2_kernel,
        grid=(b, s1 // kb),
        in_specs=[
            pl.BlockSpec((1, kb, s2, wide), lambda bi, ki: (bi, ki, 0, 0)),
            pl.BlockSpec((s2, 2 * s2), lambda bi, ki: (0, 0)),
        ],
        out_specs=pl.BlockSpec((1, kb, s2, FNET_W), lambda bi, ki: (bi, ki, 0, 0)),
        out_shape=jax.ShapeDtypeStruct((b, s1, s2, FNET_W), F32),
        compiler_params=pltpu.CompilerParams(
            dimension_semantics=("arbitrary", "arbitrary"), vmem_limit_bytes=VMEM_LIMIT),
        name="fft_stage2",
    )(a4, m2)
    return y.transpose(0, 2, 1, 3).reshape(b, s, FNET_W)


def kernel(x_prompt, x_sample, attn_norm_g, w_in, q_lat_g, w_uq, kv_lat_g, w_ukv, q_head_g,
           k_head_g, attn_out_g, fnet_out_g, w_out, ffn_norm_g, peer_w_q, peer_sub_keys,
           peer_u, peer_v):
    l = 0
    keys = peer_sub_keys[l].reshape(2 * PEER_HEADS, PEER_NKEYS, PEER_HALF).astype(BF16)
    w_pq, u, v_tab = peer_w_q[l].astype(BF16), peer_u[l].astype(BF16), peer_v[l].astype(BF16)
    outs = []
    for x in (x_prompt, x_sample):
        q, kt, v, f = _project(x, attn_norm_g[l], w_in[l], q_lat_g[l], w_uq[l], kv_lat_g[l],
                               w_ukv[l], q_head_g[l], k_head_g[l])
        a = _flash_attention(q, kt, v).reshape(-1, ATTN_W)
        fm = _fnet(f).reshape(-1, FNET_W)
        x1, hn = _mix(x.reshape(-1, D_MODEL), a, fm, attn_out_g[l], fnet_out_g[l], w_out[l],
                      ffn_norm_g[l])
        rank2t, e2t, n1t, e1t = _peer_gate(hn, w_pq, keys)
        outs.append(_peer_dense(hn, u, v_tab, n1t, e1t, rank2t, e2t, x1).reshape(x.shape))
    return tuple(outs)
```

```python
import functools

import jax
import jax.numpy as jnp
import numpy as np
from jax import lax
from jax.experimental import pallas as pl
from jax.experimental.pallas import tpu as pltpu

D_MODEL = 1024
N_HEADS = 8
QK_NOPE = 64
QK_ROPE = 32
QK_HEAD = QK_NOPE + QK_ROPE
V_HEAD = 64
Q_LORA = 384
KV_LORA = 256
ATTN_W = N_HEADS * V_HEAD
ROPE_THETA = 10000.0
FNET_W = D_MODEL - ATTN_W
FNET_GROUPS = 4
FNET_CH = FNET_W // FNET_GROUPS
PEER_HEADS = 8
PEER_NKEYS = 128
PEER_EXPERTS = PEER_NKEYS * PEER_NKEYS
PEER_HALF = 128
PEER_TOPK = 16
EPS = 1e-6
LOG2_E = 1.4426950408889634

LANES = 128
BF16_ROWS = 16
GATE_GROUP = 4
VMEM_LIMIT = 48 * 1024 * 1024

F32 = jnp.float32
BF16 = jnp.bfloat16


def _flash_kernel(q_ref, kt_ref, v_ref, o_ref, m_scr, acc_scr, *, tk):
    seq = v_ref.shape[1]
    tq = q_ref.shape[1]
    nkv = seq // tk
    m_scr[...] = jnp.full(m_scr.shape, -jnp.inf, F32)
    acc_scr[...] = jnp.zeros(acc_scr.shape, F32)

    def body(j, carry):
        start = pl.multiple_of(j * tk, tk)
        heads = [slice(hh * LANES, (hh + 1) * LANES) for hh in range(2)]
        scores = [jnp.dot(q_ref[0, :, hs], kt_ref[0, hs, pl.ds(start, tk)],
                          preferred_element_type=F32) for hs in heads]
        for hh, hs in enumerate(heads):
            cols = [scores[hh][:, c * LANES:(c + 1) * LANES] for c in range(tk // LANES)]
            m_old = m_scr[hh]
            m_new = jnp.maximum(m_old, jnp.max(functools.reduce(jnp.maximum, cols),
                                               axis=1, keepdims=True))
            p = jnp.concatenate([jnp.exp2(c - m_new) for c in cols], axis=1).astype(BF16)
            pv = jnp.dot(p, v_ref[0, pl.ds(start, tk), hs], preferred_element_type=F32)
            acc_scr[hh] = jnp.exp2(m_old - m_new) * acc_scr[hh] + pv
            m_scr[hh] = m_new
        return carry

    lax.fori_loop(0, nkv, body, 0, unroll=2)
    outs = []
    for hh in range(2):
        acc = acc_scr[hh]
        outs.append(acc[:, :V_HEAD] / acc[:, V_HEAD:V_HEAD + 1])
    o_ref[0] = jnp.concatenate(outs, axis=1)


def _flash_attention(q, kt, v, *, tq=1024, tk=1024):
    b, s, _ = q.shape
    grid = (b, N_HEADS // 2, s // tq)
    return pl.pallas_call(
        functools.partial(_flash_kernel, tk=tk),
        grid=grid,
        in_specs=[
            pl.BlockSpec((1, tq, 2 * LANES), lambda bi, hp, qi: (bi, qi, hp)),
            pl.BlockSpec((1, 2 * LANES, s), lambda bi, hp, qi: (bi, hp, 0)),
            pl.BlockSpec((1, s, 2 * LANES), lambda bi, hp, qi: (bi, 0, hp)),
        ],
        out_specs=pl.BlockSpec((1, tq, LANES), lambda bi, hp, qi: (bi, qi, hp)),
        out_shape=jax.ShapeDtypeStruct((b, s, ATTN_W), F32),
        scratch_shapes=[pltpu.VMEM((2, tq, LANES), F32), pltpu.VMEM((2, tq, LANES), F32)],
        compiler_params=pltpu.CompilerParams(
            dimension_semantics=("arbitrary", "arbitrary", "arbitrary"),
            vmem_limit_bytes=VMEM_LIMIT),
        name="flash_attention",
    )(q, kt, v)


_CAND_PAIRS = tuple((a, b) for a in range(PEER_TOPK) for b in range(PEER_TOPK)
                    if (a + 1) * (b + 1) <= PEER_TOPK)
_NEG_INF = float("-inf")


def _tree_max(xs):
    xs = list(xs)
    while len(xs) > 1:
        nxt = [jnp.maximum(xs[i], xs[i + 1]) for i in range(0, len(xs) - 1, 2)]
        if len(xs) % 2:
            nxt.append(xs[-1])
        xs = nxt
    return xs[0]


def _tree_min(xs):
    xs = list(xs)
    while len(xs) > 1:
        nxt = [jnp.minimum(xs[i], xs[i + 1]) for i in range(0, len(xs) - 1, 2)]
        if len(xs) % 2:
            nxt.append(xs[-1])
        xs = nxt
    return xs[0]


def _sort_network(n):
    pairs = []
    p = 1
    while p < n:
        k = p
        while k >= 1:
            for j in range(k % p, n - k, 2 * k):
                for i in range(min(k, n - j - k)):
                    if (i + j) // (2 * p) == (i + j + k) // (2 * p):
                        pairs.append((i + j, i + j + k))
            k //= 2
        p *= 2
    return tuple(pairs)


_SUBLANES = 8
_SORT_ROWS = _sort_network(PEER_NKEYS // _SUBLANES)


def _top_values(scores):
    rows = [scores[r * _SUBLANES:(r + 1) * _SUBLANES, :] for r in range(PEER_NKEYS // _SUBLANES)]
    for a, b in _SORT_ROWS:
        rows[a], rows[b] = jnp.maximum(rows[a], rows[b]), jnp.minimum(rows[a], rows[b])
    vals = []
    for t in range(PEER_TOPK):
        m = jnp.max(rows[0], axis=0, keepdims=True)
        vals.append(m)
        if t + 1 < PEER_TOPK:
            hit = rows[0] == m
            for r in range(PEER_TOPK - 1 - t):
                rows[r] = jnp.where(hit, rows[r + 1], rows[r])
    return vals


def _gate_kernel(hn_ref, wq_ref, keys_ref, rank2_ref, e2_ref, n1_ref, e1_ref, s_scr, vals_scr):
    tm = hn_ref.shape[0]
    ncol = tm // LANES
    q = jnp.dot(hn_ref[...], wq_ref[...], preferred_element_type=F32).astype(BF16)
    for hc in range(2 * PEER_HEADS):
        s_scr[hc] = lax.dot_general(keys_ref[hc], q[:, hc * PEER_HALF:(hc + 1) * PEER_HALF],
                                    (((1,), (1,)), ((), ())), preferred_element_type=F32)

    for hc in range(2 * PEER_HEADS):
        h, c = divmod(hc, 2)

        def col_body(col, carry, hc=hc, h=h, c=c):
            cs = pl.ds(pl.multiple_of(col * LANES, LANES), LANES)
            cur = s_scr[hc, :, cs]
            vals = _top_values(cur)
            for r in range(PEER_TOPK):
                vals_scr[c, r, h:h + 1, cs] = vals[r]
            if c == 1:
                rank = jnp.zeros((PEER_NKEYS, LANES), F32)
                for r in range(PEER_TOPK):
                    rank = rank + jnp.where(cur < vals[r], 1.0, 0.0)
                rank2_ref[h, :, cs] = rank
            return carry

        lax.fori_loop(0, ncol, col_body, 0, unroll=True)

    def fin_body(col, carry):
        cs = pl.ds(pl.multiple_of(col * LANES, LANES), LANES)
        v1 = [vals_scr[0, a, :, cs] for a in range(PEER_TOPK)]
        v2 = [vals_scr[1, b, :, cs] for b in range(PEER_TOPK)]
        cands = [v1[a] + v2[b] for a, b in _CAND_PAIRS]
        top = cands[0]
        z = jnp.zeros_like(top)
        m = top
        for r in range(PEER_TOPK):
            m = _tree_max(cands)
            z = z + jnp.exp(m - top)
            if r + 1 < PEER_TOPK:
                cands = [jnp.where(cd == m, _NEG_INF, cd) for cd in cands]
        thr = m
        rz = 1.0 / z
        cnt = [sum(jnp.where(v1[a] + v2[b] >= thr, 1.0, 0.0) for aa, b in _CAND_PAIRS if aa == a)
               for a in range(PEER_TOPK)]
        bound = [_tree_min([jnp.where(cnt[a] >= float(k), v1[a], float("inf"))
                            for a in range(PEER_TOPK // k)])
                 for k in range(1, PEER_TOPK + 1)]
        for h in range(PEER_HEADS):
            s1 = s_scr[2 * h, :, cs]
            n1 = jnp.zeros((PEER_NKEYS, LANES), F32)
            for k in range(PEER_TOPK):
                n1 = n1 + jnp.where(s1 >= bound[k][h:h + 1, :], 1.0, 0.0)
            n1_ref[h, :, cs] = n1
            e1_ref[h, :, cs] = jnp.exp(s1 - v1[0][h:h + 1, :])
            s2 = s_scr[2 * h + 1, :, cs]
            e2_ref[h, :, cs] = jnp.exp(s2 - v2[0][h:h + 1, :]) * rz[h:h + 1, :]
        return carry

    lax.fori_loop(0, ncol, fin_body, 0)


def _peer_gate(hn, wq, keys, *, tm=256):
    t = hn.shape[0]
    out = jax.ShapeDtypeStruct((PEER_HEADS, PEER_NKEYS, t), F32)
    ospec = pl.BlockSpec((PEER_HEADS, PEER_NKEYS, tm), lambda ti: (0, 0, ti))
    return pl.pallas_call(
        _gate_kernel,
        grid=(t // tm,),
        in_specs=[
            pl.BlockSpec((tm, D_MODEL), lambda ti: (ti, 0)),
            pl.BlockSpec((D_MODEL, 2 * PEER_HEADS * PEER_HALF), lambda ti: (0, 0)),
            pl.BlockSpec((2 * PEER_HEADS, PEER_NKEYS, PEER_HALF), lambda ti: (0, 0, 0)),
        ],
        out_specs=[ospec, ospec, ospec, ospec],
        out_shape=[out, out, out, out],
        scratch_shapes=[pltpu.VMEM((2 * PEER_HEADS, PEER_NKEYS, tm), F32),
                        pltpu.VMEM((2, PEER_TOPK, PEER_HEADS, tm), F32)],
        compiler_params=pltpu.CompilerParams(
            dimension_semantics=("arbitrary",), vmem_limit_bytes=VMEM_LIMIT),
        name="peer_gate",
    )(hn, wq, keys)


def _gelu_tanh(x):
    k1 = float(np.sqrt(2.0 / np.pi))
    k2 = k1 * 0.044715
    hx = 0.5 * x
    return hx + hx * jnp.tanh(x * (k1 + k2 * (x * x)))


def _peer_kernel(hn_ref, u_ref, vt_ref, n1_ref, e1_ref, rank2_ref, e2_ref, x_ref,
                 y_ref, acc_ref, a_scr, g_scr, r2_scr, e2_scr, *, n_i1):
    eb = pl.program_id(1)
    tm = hn_ref.shape[0]
    rows = PEER_NKEYS // BF16_ROWS

    @pl.when(eb == 0)
    def _():
        acc_ref[...] = jnp.zeros_like(acc_ref)
        r2_scr[...] = rank2_ref[...].astype(BF16)
        e2_scr[...] = e2_ref[...].astype(BF16)

    a_scr[...] = lax.dot_general(u_ref[...], hn_ref[...], (((1,), (1,)), ((), ())),
                                 preferred_element_type=F32).astype(BF16)

    for c in range(tm // LANES):
        cs = slice(c * LANES, (c + 1) * LANES)
        for j0 in range(0, n_i1, GATE_GROUP):
            group = range(j0, j0 + GATE_GROUP)
            gs = {j: jnp.zeros((rows, BF16_ROWS, LANES), BF16) for j in group}
            for h in range(PEER_HEADS):
                r2 = r2_scr[h, :, cs].reshape(rows, BF16_ROWS, LANES)
                e2 = e2_scr[h, :, cs].reshape(rows, BF16_ROWS, LANES)
                for j in group:
                    n1row = jnp.broadcast_to(n1_ref[h, 0, j:j + 1, cs], (BF16_ROWS, LANES)).astype(BF16)
                    e1row = jnp.broadcast_to(e1_ref[h, 0, j:j + 1, cs], (BF16_ROWS, LANES)).astype(BF16)
                    gs[j] = gs[j] + jnp.where(r2 < n1row[None], e2, jnp.zeros_like(e2)) * e1row[None]
            for j in group:
                g_scr[j * PEER_NKEYS:(j + 1) * PEER_NKEYS, cs] = gs[j].reshape(PEER_NKEYS, LANES)

    p = _gelu_tanh(a_scr[...]) * g_scr[...]
    acc_ref[...] += jnp.dot(vt_ref[0], p, preferred_element_type=F32)

    @pl.when(eb == pl.num_programs(1) - 1)
    def _():
        y_ref[...] = x_ref[...] + acc_ref[...].T


def _peer_dense(hn, u, v, n1t, e1t, rank2t, e2t, x1, *, tm=512, n_i1=16):
    t = hn.shape[0]
    te = n_i1 * PEER_NKEYS
    vt = v.reshape(PEER_EXPERTS // te, te, D_MODEL).transpose(0, 2, 1)
    nblk = PEER_NKEYS // n_i1
    n1r = n1t.reshape(PEER_HEADS, nblk, n_i1, t)
    e1r = e1t.reshape(PEER_HEADS, nblk, n_i1, t)
    return pl.pallas_call(
        functools.partial(_peer_kernel, n_i1=n_i1),
        grid=(t // tm, nblk),
        in_specs=[
            pl.BlockSpec((tm, D_MODEL), lambda ti, eb: (ti, 0)),
            pl.BlockSpec((te, D_MODEL), lambda ti, eb: (eb, 0)),
            pl.BlockSpec((1, D_MODEL, te), lambda ti, eb: (eb, 0, 0)),
            pl.BlockSpec((PEER_HEADS, 1, n_i1, tm), lambda ti, eb: (0, eb, 0, ti)),
            pl.BlockSpec((PEER_HEADS, 1, n_i1, tm), lambda ti, eb: (0, eb, 0, ti)),
            pl.BlockSpec((PEER_HEADS, PEER_NKEYS, tm), lambda ti, eb: (0, 0, ti)),
            pl.BlockSpec((PEER_HEADS, PEER_NKEYS, tm), lambda ti, eb: (0, 0, ti)),
            pl.BlockSpec((tm, D_MODEL), lambda ti, eb: (ti, 0)),
        ],
        out_specs=pl.BlockSpec((tm, D_MODEL), lambda ti, eb: (ti, 0)),
        out_shape=jax.ShapeDtypeStruct((t, D_MODEL), F32),
        scratch_shapes=[pltpu.VMEM((D_MODEL, tm), F32),
                        pltpu.VMEM((te, tm), BF16), pltpu.VMEM((te, tm), BF16),
                        pltpu.VMEM((PEER_HEADS, PEER_NKEYS, tm), BF16),
                        pltpu.VMEM((PEER_HEADS, PEER_NKEYS, tm), BF16)],
        compiler_params=pltpu.CompilerParams(
            dimension_semantics=("arbitrary", "arbitrary"),
            vmem_limit_bytes=VMEM_LIMIT),
        name="peer_dense",
    )(hn, u, vt, n1r, e1r, rank2t, e2t, x1)


def _row_rms(x, width):
    return x * lax.rsqrt(jnp.sum(x * x, axis=-1, keepdims=True) * (1.0 / width) + EPS)


def _proj_kernel(x_ref, g_in_ref, w_in_ref, g_q_ref, w_uq_ref, g_kv_ref, w_uk_ref, w_uv_ref,
                 g_qh_ref, g_kh_ref, cos_ref, sin_ref, ones_ref, perm_ref, v_one_ref, dft_c_ref,
                 q_ref, kt_ref, v_ref, f_ref):
    o0, o1, o2 = Q_LORA, Q_LORA + KV_LORA, Q_LORA + KV_LORA + LANES
    h = (_row_rms(x_ref[0], D_MODEL) * g_in_ref[...]).astype(BF16)
    p = jnp.dot(h, w_in_ref[...], preferred_element_type=F32)
    c_q = (_row_rms(p[:, :o0], Q_LORA) * g_q_ref[...]).astype(BF16)
    c_kv = (_row_rms(p[:, o0:o1], KV_LORA) * g_kv_ref[...]).astype(BF16)
    k_rope = p[:, o1:o2]
    f_ref[0] = jnp.dot(p[:, o2:].astype(BF16), dft_c_ref[...],
                       preferred_element_type=F32).astype(BF16)
    q = jnp.dot(c_q, w_uq_ref[...], preferred_element_type=F32)
    k = jnp.dot(c_kv, w_uk_ref[...], preferred_element_type=F32)
    v = jnp.dot(c_kv, w_uv_ref[...], preferred_element_type=F32)
    v_ref[0] = (v + v_one_ref[...]).astype(BF16)

    cos, sin = cos_ref[...], sin_ref[...]

    def select_sum(val, mat):
        return jnp.dot(val.astype(BF16), mat, preferred_element_type=F32)

    def head_norm_rope(slot, gain):
        ss = select_sum(slot * slot, ones_ref[...])
        y = slot * lax.rsqrt(ss * (1.0 / QK_HEAD) + EPS) * gain
        return y * cos + select_sum(y, perm_ref[...]) * sin

    q_slots, k_slots = [], []
    for hd in range(N_HEADS):
        hs = slice(hd * LANES, (hd + 1) * LANES)
        q_slots.append(head_norm_rope(q[:, hs], g_qh_ref[...]))
        k_slots.append(head_norm_rope(k[:, hs] + k_rope, g_kh_ref[...]))
    q_ref[0] = jnp.concatenate(q_slots, axis=1).astype(BF16)
    kt_ref[0] = jnp.concatenate(k_slots, axis=1).T.astype(BF16)


def _slot_cols(w, width, n_in):
    w = w.reshape(n_in, N_HEADS, width)
    return jnp.pad(w, ((0, 0), (0, 0), (0, LANES - width))).reshape(n_in, N_HEADS * LANES)


def _project(x, attn_norm_g, w_in, q_lat_g, w_uq, kv_lat_g, w_ukv, q_head_g, k_head_g, *, tm=512):
    b, s, _ = x.shape
    o0, o1, o2 = Q_LORA, Q_LORA + KV_LORA, Q_LORA + KV_LORA + QK_ROPE
    k_rope_cols = jnp.pad(w_in[:, o1:o2], ((0, 0), (QK_NOPE, LANES - QK_HEAD)))
    w_in_p = jnp.concatenate([w_in[:, :o1], k_rope_cols, w_in[:, o2:]], axis=1).astype(BF16)
    w_uq_p = _slot_cols(w_uq, QK_HEAD, Q_LORA).astype(BF16)
    w_ukv_h = w_ukv.reshape(KV_LORA, N_HEADS, QK_NOPE + V_HEAD)
    w_uk_p = _slot_cols(w_ukv_h[:, :, :QK_NOPE].reshape(KV_LORA, -1), QK_NOPE, KV_LORA).astype(BF16)
    w_uv_p = _slot_cols(w_ukv_h[:, :, QK_NOPE:].reshape(KV_LORA, -1), V_HEAD, KV_LORA).astype(BF16)
    pad_gain = lambda g: jnp.pad(g, (0, LANES - QK_HEAD)).reshape(1, LANES)
    g_qh = pad_gain(q_head_g) * (QK_HEAD ** -0.5 * LOG2_E)
    g_kh = pad_gain(k_head_g)
    half = QK_ROPE // 2
    freqs = 1.0 / (ROPE_THETA ** (jnp.arange(half, dtype=F32) / half))
    ang = jnp.arange(s, dtype=F32)[:, None] * freqs[None, :]
    cos, sin = jnp.cos(ang), jnp.sin(ang)
    zeros = lambda n: jnp.zeros((s, n), F32)
    cos_t = jnp.concatenate([jnp.ones((s, QK_NOPE), F32), cos, cos, zeros(LANES - QK_HEAD)], axis=1)
    sin_t = jnp.concatenate([zeros(QK_NOPE), -sin, sin, zeros(LANES - QK_HEAD)], axis=1)
    lane = np.arange(LANES)
    partner = np.where((lane >= QK_NOPE) & (lane < QK_NOPE + half), lane + half,
                       np.where((lane >= QK_NOPE + half) & (lane < QK_HEAD), lane - half, -1))
    perm = jnp.asarray(lane[:, None] == partner[None, :], BF16)
    ones = jnp.ones((LANES, LANES), BF16)
    v_one = jnp.tile((jnp.arange(LANES) == V_HEAD).astype(F32), N_HEADS).reshape(1, -1)
    row = lambda g: g.reshape(1, -1)

    const = lambda shape: pl.BlockSpec(shape, lambda bi, si: (0,) * len(shape))
    tab = pl.BlockSpec((tm, LANES), lambda bi, si: (si, 0))
    wide = N_HEADS * LANES
    return pl.pallas_call(
        _proj_kernel,
        grid=(b, s // tm),
        in_specs=[
            pl.BlockSpec((1, tm, D_MODEL), lambda bi, si: (bi, si, 0)),
            const((1, D_MODEL)), const(w_in_p.shape), const((1, Q_LORA)), const(w_uq_p.shape),
            const((1, KV_LORA)), const(w_uk_p.shape), const(w_uv_p.shape),
            const((1, LANES)), const((1, LANES)), tab, tab, const((LANES, LANES)),
            const((LANES, LANES)), const((1, wide)),
            const((FNET_W, 2 * FNET_W)),
        ],
        out_specs=[
            pl.BlockSpec((1, tm, wide), lambda bi, si: (bi, si, 0)),
            pl.BlockSpec((1, wide, tm), lambda bi, si: (bi, 0, si)),
            pl.BlockSpec((1, tm, wide), lambda bi, si: (bi, si, 0)),
            pl.BlockSpec((1, tm, 2 * FNET_W), lambda bi, si: (bi, si, 0)),
        ],
        out_shape=[
            jax.ShapeDtypeStruct((b, s, wide), BF16),
            jax.ShapeDtypeStruct((b, wide, s), BF16),
            jax.ShapeDtypeStruct((b, s, wide), BF16),
            jax.ShapeDtypeStruct((b, s, 2 * FNET_W), BF16),
        ],
        compiler_params=pltpu.CompilerParams(
            dimension_semantics=("arbitrary", "arbitrary"), vmem_limit_bytes=VMEM_LIMIT),
        name="input_projection",
    )(x, row(attn_norm_g), w_in_p, row(q_lat_g), w_uq_p, row(kv_lat_g), w_uk_p, w_uv_p,
      g_qh, g_kh, cos_t, sin_t, ones, perm, v_one, _channel_dft(s))


def _mix_kernel(x_ref, a_ref, fm_ref, g_a_ref, g_f_ref, w_a_ref, w_f_ref, g_ffn_ref,
                x1_ref, hn_ref):
    a = (_row_rms(a_ref[...], ATTN_W) * g_a_ref[...]).astype(BF16)
    fm = (_row_rms(fm_ref[...], FNET_W) * g_f_ref[...]).astype(BF16)
    x1 = (x_ref[...] + jnp.dot(a, w_a_ref[...], preferred_element_type=F32)
          + jnp.dot(fm, w_f_ref[...], preferred_element_type=F32))
    x1_ref[...] = x1
    hn_ref[...] = (_row_rms(x1, D_MODEL) * g_ffn_ref[...]).astype(BF16)


def _mix(x, a, fm, attn_out_g, fnet_out_g, w_out, ffn_norm_g, *, tm=512):
    t = x.shape[0]
    row = lambda g: g.reshape(1, -1)
    tok = lambda w: pl.BlockSpec((tm, w), lambda ti: (ti, 0))
    const = lambda shape: pl.BlockSpec(shape, lambda ti: (0, 0))
    return pl.pallas_call(
        _mix_kernel,
        grid=(t // tm,),
        in_specs=[tok(D_MODEL), tok(ATTN_W), tok(FNET_W), const((1, ATTN_W)), const((1, FNET_W)),
                  const((ATTN_W, D_MODEL)), const((FNET_W, D_MODEL)), const((1, D_MODEL))],
        out_specs=[tok(D_MODEL), tok(D_MODEL)],
        out_shape=[jax.ShapeDtypeStruct((t, D_MODEL), F32), jax.ShapeDtypeStruct((t, D_MODEL), BF16)],
        compiler_params=pltpu.CompilerParams(
            dimension_semantics=("arbitrary",), vmem_limit_bytes=VMEM_LIMIT),
        name="output_mix",
    )(x, a, fm, row(attn_out_g), row(fnet_out_g), w_out[:ATTN_W].astype(BF16),
      w_out[ATTN_W:].astype(BF16), row(ffn_norm_g))


FFT_S1 = 64


def _dft(n):
    ang = -2.0 * np.pi * np.outer(np.arange(n), np.arange(n)) / n
    return np.cos(ang), np.sin(ang)


def _channel_dft(s):
    c, si = _dft(FNET_CH)
    eye = np.eye(FNET_GROUPS)
    scale = (FNET_CH * s) ** -0.5
    return jnp.asarray(np.concatenate([np.kron(eye, c), np.kron(eye, si)], axis=1) * scale, BF16)


def _fft_stage1_kernel(x_ref, dr_ref, di_ref, twr_ref, twi_ref, o_ref):
    x = x_ref[0]
    yr = jnp.dot(dr_ref[...], x, preferred_element_type=F32)
    yi = jnp.dot(di_ref[...], x, preferred_element_type=F32)
    w = FNET_W
    for j in range(x.shape[1] // (2 * w)):
        re, im = slice(2 * j * w, (2 * j + 1) * w), slice((2 * j + 1) * w, (2 * j + 2) * w)
        ar = yr[:, re] - yi[:, im]
        ai = yr[:, im] + yi[:, re]
        tw = slice(j * LANES, (j + 1) * LANES)
        twr = jnp.concatenate([twr_ref[:, tw]] * (w // LANES), axis=1)
        twi = jnp.concatenate([twi_ref[:, tw]] * (w // LANES), axis=1)
        o_ref[0, :, re] = (ar * twr - ai * twi).astype(BF16)
        o_ref[0, :, im] = (ar * twi + ai * twr).astype(BF16)


def _fft_stage2_kernel(a_ref, m_ref, o_ref):
    w = FNET_W
    for j in range(a_ref.shape[1]):
        xa = a_ref[0, j]
        stacked = jnp.concatenate([xa[:, :w], xa[:, w:]], axis=0)
        o_ref[0, j] = jnp.dot(m_ref[...], stacked, preferred_element_type=F32)


def _fnet(fx, *, tn2=8, kb=8):
    b, s, wide = fx.shape
    s1, s2 = FFT_S1, s // FFT_S1
    d1r, d1i = _dft(s1)
    d2r, d2i = _dft(s2)
    ang = -2.0 * np.pi / s * jnp.outer(jnp.arange(s1, dtype=F32), jnp.arange(s2, dtype=F32))
    twr = jnp.repeat(jnp.cos(ang), LANES, axis=1)
    twi = jnp.repeat(jnp.sin(ang), LANES, axis=1)
    cols = tn2 * wide
    x2 = fx.reshape(b, s1, s2 * wide)
    a = pl.pallas_call(
        _fft_stage1_kernel,
        grid=(b, s2 // tn2),
        in_specs=[
            pl.BlockSpec((1, s1, cols), lambda bi, ci: (bi, 0, ci)),
            pl.BlockSpec((s1, s1), lambda bi, ci: (0, 0)),
            pl.BlockSpec((s1, s1), lambda bi, ci: (0, 0)),
            pl.BlockSpec((s1, tn2 * LANES), lambda bi, ci: (0, ci)),
            pl.BlockSpec((s1, tn2 * LANES), lambda bi, ci: (0, ci)),
        ],
        out_specs=pl.BlockSpec((1, s1, cols), lambda bi, ci: (bi, 0, ci)),
        out_shape=jax.ShapeDtypeStruct((b, s1, s2 * wide), BF16),
        compiler_params=pltpu.CompilerParams(
            dimension_semantics=("arbitrary", "arbitrary"), vmem_limit_bytes=VMEM_LIMIT),
        name="fft_stage1",
    )(x2, jnp.asarray(d1r, BF16), jnp.asarray(d1i, BF16), twr, twi)
    a4 = a.reshape(b, s1, s2, wide)
    m2 = jnp.asarray(np.concatenate([d2r, -d2i], axis=1), BF16)
    y = pl.pallas_call(
        _fft_stage2_kernel,
        grid=(b, s1 // kb),
        in_specs=[
            pl.BlockSpec((1, kb, s2, wide), lambda bi, ki: (bi, ki, 0, 0)),
            pl.BlockSpec((s2, 2 * s2), lambda bi, ki: (0, 0)),
        ],
        out_specs=pl.BlockSpec((1, kb, s2, FNET_W), lambda bi, ki: (bi, ki, 0, 0)),
        out_shape=jax.ShapeDtypeStruct((b, s1, s2, FNET_W), F32),
        compiler_params=pltpu.CompilerParams(
            dimension_semantics=("arbitrary", "arbitrary"), vmem_limit_bytes=VMEM_LIMIT),
        name="fft_stage2",
    )(a4, m2)
    return y.transpose(0, 2, 1, 3).reshape(b, s, FNET_W)


def kernel(x_prompt, x_sample, attn_norm_g, w_in, q_lat_g, w_uq, kv_lat_g, w_ukv, q_head_g,
           k_head_g, attn_out_g, fnet_out_g, w_out, ffn_norm_g, peer_w_q, peer_sub_keys,
           peer_u, peer_v):
    l = 0
    keys = peer_sub_keys[l].reshape(2 * PEER_HEADS, PEER_NKEYS, PEER_HALF).astype(BF16)
    w_pq, u, v_tab = peer_w_q[l].astype(BF16), peer_u[l].astype(BF16), peer_v[l].astype(BF16)
    outs = []
    for x in (x_prompt, x_sample):
        q, kt, v, f = _project(x, attn_norm_g[l], w_in[l], q_lat_g[l], w_uq[l], kv_lat_g[l],
                               w_ukv[l], q_head_g[l], k_head_g[l])
        a = _flash_attention(q, kt, v).reshape(-1, ATTN_W)
        fm = _fnet(f).reshape(-1, FNET_W)
        x1, hn = _mix(x.reshape(-1, D_MODEL), a, fm, attn_out_g[l], fnet_out_g[l], w_out[l],
                      ffn_norm_g[l])
        rank2t, e2t, n1t, e1t = _peer_gate(hn, w_pq, keys)
        outs.append(_peer_dense(hn, u, v_tab, n1t, e1t, rank2t, e2t, x1).reshape(x.shape))
    return tuple(outs)
```

```python
import functools

import jax
import jax.numpy as jnp
import numpy as np
from jax import lax
from jax.experimental import pallas as pl
from jax.experimental.pallas import tpu as pltpu

D_MODEL = 1024
N_HEADS = 8
QK_NOPE = 64
QK_ROPE = 32
QK_HEAD = QK_NOPE + QK_ROPE
V_HEAD = 64
Q_LORA = 384
KV_LORA = 256
ATTN_W = N_HEADS * V_HEAD
ROPE_THETA = 10000.0
FNET_W = D_MODEL - ATTN_W
FNET_GROUPS = 4
FNET_CH = FNET_W // FNET_GROUPS
PEER_HEADS = 8
PEER_NKEYS = 128
PEER_EXPERTS = PEER_NKEYS * PEER_NKEYS
PEER_HALF = 128
PEER_TOPK = 16
EPS = 1e-6
LOG2_E = 1.4426950408889634

LANES = 128
BF16_ROWS = 16
GATE_GROUP = 4
VMEM_LIMIT = 48 * 1024 * 1024

F32 = jnp.float32
BF16 = jnp.bfloat16


def _flash_kernel(q_ref, kt_ref, v_ref, o_ref, m_scr, acc_scr, *, tk):
    seq = v_ref.shape[1]
    tq = q_ref.shape[1]
    nkv = seq // tk
    m_scr[...] = jnp.full(m_scr.shape, -jnp.inf, F32)
    acc_scr[...] = jnp.zeros(acc_scr.shape, F32)

    def body(j, carry):
        start = pl.multiple_of(j * tk, tk)
        heads = [slice(hh * LANES, (hh + 1) * LANES) for hh in range(2)]
        scores = [jnp.dot(q_ref[0, :, hs], kt_ref[0, hs, pl.ds(start, tk)],
                          preferred_element_type=F32) for hs in heads]
        for hh, hs in enumerate(heads):
            cols = [scores[hh][:, c * LANES:(c + 1) * LANES] for c in range(tk // LANES)]
            m_old = m_scr[hh]
            m_new = jnp.maximum(m_old, jnp.max(functools.reduce(jnp.maximum, cols),
                                               axis=1, keepdims=True))
            p = jnp.concatenate([jnp.exp2(c - m_new) for c in cols], axis=1).astype(BF16)
            pv = jnp.dot(p, v_ref[0, pl.ds(start, tk), hs], preferred_element_type=F32)
            acc_scr[hh] = jnp.exp2(m_old - m_new) * acc_scr[hh] + pv
            m_scr[hh] = m_new
        return carry

    lax.fori_loop(0, nkv, body, 0, unroll=2)
    outs = []
    for hh in range(2):
        acc = acc_scr[hh]
        outs.append(acc[:, :V_HEAD] / acc[:, V_HEAD:V_HEAD + 1])
    o_ref[0] = jnp.concatenate(outs, axis=1)


def _flash_attention(q, kt, v, *, tq=512, tk=2048):
    b, s, _ = q.shape
    grid = (b, N_HEADS // 2, s // tq)
    return pl.pallas_call(
        functools.partial(_flash_kernel, tk=tk),
        grid=grid,
        in_specs=[
            pl.BlockSpec((1, tq, 2 * LANES), lambda bi, hp, qi: (bi, qi, hp)),
            pl.BlockSpec((1, 2 * LANES, s), lambda bi, hp, qi: (bi, hp, 0)),
            pl.BlockSpec((1, s, 2 * LANES), lambda bi, hp, qi: (bi, 0, hp)),
        ],
        out_specs=pl.BlockSpec((1, tq, LANES), lambda bi, hp, qi: (bi, qi, hp)),
        out_shape=jax.ShapeDtypeStruct((b, s, ATTN_W), F32),
        scratch_shapes=[pltpu.VMEM((2, tq, LANES), F32), pltpu.VMEM((2, tq, LANES), F32)],
        compiler_params=pltpu.CompilerParams(
            dimension_semantics=("arbitrary", "arbitrary", "arbitrary"),
            vmem_limit_bytes=VMEM_LIMIT),
        name="flash_attention",
    )(q, kt, v)


_CAND_PAIRS = tuple((a, b) for a in range(PEER_TOPK) for b in range(PEER_TOPK)
                    if (a + 1) * (b + 1) <= PEER_TOPK)
_NEG_INF = float("-inf")


def _tree_max(xs):
    xs = list(xs)
    while len(xs) > 1:
        nxt = [jnp.maximum(xs[i], xs[i + 1]) for i in range(0, len(xs) - 1, 2)]
        if len(xs) % 2:
            nxt.append(xs[-1])
        xs = nxt
    return xs[0]


def _tree_min(xs):
    xs = list(xs)
    while len(xs) > 1:
        nxt = [jnp.minimum(xs[i], xs[i + 1]) for i in range(0, len(xs) - 1, 2)]
        if len(xs) % 2:
            nxt.append(xs[-1])
        xs = nxt
    return xs[0]


def _sort_network(n):
    pairs = []
    p = 1
    while p < n:
        k = p
        while k >= 1:
            for j in range(k % p, n - k, 2 * k):
                for i in range(min(k, n - j - k)):
                    if (i + j) // (2 * p) == (i + j + k) // (2 * p):
                        pairs.append((i + j, i + j + k))
            k //= 2
        p *= 2
    return tuple(pairs)


_SUBLANES = 8
_SORT_ROWS = _sort_network(PEER_NKEYS // _SUBLANES)


def _top_values(scores):
    rows = [scores[r * _SUBLANES:(r + 1) * _SUBLANES, :] for r in range(PEER_NKEYS // _SUBLANES)]
    for a, b in _SORT_ROWS:
        rows[a], rows[b] = jnp.maximum(rows[a], rows[b]), jnp.minimum(rows[a], rows[b])
    vals = []
    for t in range(PEER_TOPK):
        m = jnp.max(rows[0], axis=0, keepdims=True)
        vals.append(m)
        if t + 1 < PEER_TOPK:
            hit = rows[0] == m
            for r in range(PEER_TOPK - 1 - t):
                rows[r] = jnp.where(hit, rows[r + 1], rows[r])
    return vals


def _gate_kernel(hn_ref, wq_ref, keys_ref, rank2_ref, e2_ref, n1_ref, e1_ref, s_scr, vals_scr):
    tm = hn_ref.shape[0]
    ncol = tm // LANES
    q = jnp.dot(hn_ref[...], wq_ref[...], preferred_element_type=F32).astype(BF16)
    for hc in range(2 * PEER_HEADS):
        s_scr[hc] = lax.dot_general(keys_ref[hc], q[:, hc * PEER_HALF:(hc + 1) * PEER_HALF],
                                    (((1,), (1,)), ((), ())), preferred_element_type=F32)

    for hc in range(2 * PEER_HEADS):
        h, c = divmod(hc, 2)

        def col_body(col, carry, hc=hc, h=h, c=c):
            cs = pl.ds(pl.multiple_of(col * LANES, LANES), LANES)
            cur = s_scr[hc, :, cs]
            vals = _top_values(cur)
            for r in range(PEER_TOPK):
                vals_scr[c, r, h:h + 1, cs] = vals[r]
            if c == 1:
                rank = jnp.zeros((PEER_NKEYS, LANES), F32)
                for r in range(PEER_TOPK):
                    rank = rank + jnp.where(cur < vals[r], 1.0, 0.0)
                rank2_ref[h, :, cs] = rank
            return carry

        lax.fori_loop(0, ncol, col_body, 0, unroll=True)

    def fin_body(col, carry):
        cs = pl.ds(pl.multiple_of(col * LANES, LANES), LANES)
        v1 = [vals_scr[0, a, :, cs] for a in range(PEER_TOPK)]
        v2 = [vals_scr[1, b, :, cs] for b in range(PEER_TOPK)]
        cands = [v1[a] + v2[b] for a, b in _CAND_PAIRS]
        top = cands[0]
        z = jnp.zeros_like(top)
        m = top
        for r in range(PEER_TOPK):
            m = _tree_max(cands)
            z = z + jnp.exp(m - top)
            if r + 1 < PEER_TOPK:
                cands = [jnp.where(cd == m, _NEG_INF, cd) for cd in cands]
        thr = m
        rz = 1.0 / z
        cnt = [sum(jnp.where(v1[a] + v2[b] >= thr, 1.0, 0.0) for aa, b in _CAND_PAIRS if aa == a)
               for a in range(PEER_TOPK)]
        bound = [_tree_min([jnp.where(cnt[a] >= float(k), v1[a], float("inf"))
                            for a in range(PEER_TOPK // k)])
                 for k in range(1, PEER_TOPK + 1)]
        for h in range(PEER_HEADS):
            s1 = s_scr[2 * h, :, cs]
            n1 = jnp.zeros((PEER_NKEYS, LANES), F32)
            for k in range(PEER_TOPK):
                n1 = n1 + jnp.where(s1 >= bound[k][h:h + 1, :], 1.0, 0.0)
            n1_ref[h, :, cs] = n1
            e1_ref[h, :, cs] = jnp.exp(s1 - v1[0][h:h + 1, :])
            s2 = s_scr[2 * h + 1, :, cs]
            e2_ref[h, :, cs] = jnp.exp(s2 - v2[0][h:h + 1, :]) * rz[h:h + 1, :]
        return carry

    lax.fori_loop(0, ncol, fin_body, 0)


def _peer_gate(hn, wq, keys, *, tm=256):
    t = hn.shape[0]
    out = jax.ShapeDtypeStruct((PEER_HEADS, PEER_NKEYS, t), F32)
    ospec = pl.BlockSpec((PEER_HEADS, PEER_NKEYS, tm), lambda ti: (0, 0, ti))
    return pl.pallas_call(
        _gate_kernel,
        grid=(t // tm,),
        in_specs=[
            pl.BlockSpec((tm, D_MODEL), lambda ti: (ti, 0)),
            pl.BlockSpec((D_MODEL, 2 * PEER_HEADS * PEER_HALF), lambda ti: (0, 0)),
            pl.BlockSpec((2 * PEER_HEADS, PEER_NKEYS, PEER_HALF), lambda ti: (0, 0, 0)),
        ],
        out_specs=[ospec, ospec, ospec, ospec],
        out_shape=[out, out, out, out],
        scratch_shapes=[pltpu.VMEM((2 * PEER_HEADS, PEER_NKEYS, tm), F32),
                        pltpu.VMEM((2, PEER_TOPK, PEER_HEADS, tm), F32)],
        compiler_params=pltpu.CompilerParams(
            dimension_semantics=("arbitrary",), vmem_limit_bytes=VMEM_LIMIT),
        name="peer_gate",
    )(hn, wq, keys)


def _gelu_tanh(x):
    k1 = float(np.sqrt(2.0 / np.pi))
    k2 = k1 * 0.044715
    hx = 0.5 * x
    return hx + hx * jnp.tanh(x * (k1 + k2 * (x * x)))


def _peer_kernel(hn_ref, u_ref, vt_ref, n1_ref, e1_ref, rank2_ref, e2_ref, x_ref,
                 y_ref, acc_ref, a_scr, g_scr, r2_scr, e2_scr, *, n_i1):
    eb = pl.program_id(1)
    tm = hn_ref.shape[0]
    rows = PEER_NKEYS // BF16_ROWS

    @pl.when(eb == 0)
    def _():
        acc_ref[...] = jnp.zeros_like(acc_ref)
        r2_scr[...] = rank2_ref[...].astype(BF16)
        e2_scr[...] = e2_ref[...].astype(BF16)

    a_scr[...] = lax.dot_general(u_ref[...], hn_ref[...], (((1,), (1,)), ((), ())),
                                 preferred_element_type=F32).astype(BF16)

    for c in range(tm // LANES):
        cs = slice(c * LANES, (c + 1) * LANES)
        for j0 in range(0, n_i1, GATE_GROUP):
            group = range(j0, j0 + GATE_GROUP)
            gs = {j: jnp.zeros((rows, BF16_ROWS, LANES), BF16) for j in group}
            for h in range(PEER_HEADS):
                r2 = r2_scr[h, :, cs].reshape(rows, BF16_ROWS, LANES)
                e2 = e2_scr[h, :, cs].reshape(rows, BF16_ROWS, LANES)
                for j in group:
                    n1row = jnp.broadcast_to(n1_ref[h, 0, j:j + 1, cs], (BF16_ROWS, LANES)).astype(BF16)
                    e1row = jnp.broadcast_to(e1_ref[h, 0, j:j + 1, cs], (BF16_ROWS, LANES)).astype(BF16)
                    gs[j] = gs[j] + jnp.where(r2 < n1row[None], e2, jnp.zeros_like(e2)) * e1row[None]
            for j in group:
                g_scr[j * PEER_NKEYS:(j + 1) * PEER_NKEYS, cs] = gs[j].reshape(PEER_NKEYS, LANES)

    p = _gelu_tanh(a_scr[...]) * g_scr[...]
    acc_ref[...] += jnp.dot(vt_ref[0], p, preferred_element_type=F32)

    @pl.when(eb == pl.num_programs(1) - 1)
    def _():
        y_ref[...] = x_ref[...] + acc_ref[...].T


def _peer_dense(hn, u, v, n1t, e1t, rank2t, e2t, x1, *, tm=512, n_i1=16):
    t = hn.shape[0]
    te = n_i1 * PEER_NKEYS
    vt = v.reshape(PEER_EXPERTS // te, te, D_MODEL).transpose(0, 2, 1)
    nblk = PEER_NKEYS // n_i1
    n1r = n1t.reshape(PEER_HEADS, nblk, n_i1, t)
    e1r = e1t.reshape(PEER_HEADS, nblk, n_i1, t)
    return pl.pallas_call(
        functools.partial(_peer_kernel, n_i1=n_i1),
        grid=(t // tm, nblk),
        in_specs=[
            pl.BlockSpec((tm, D_MODEL), lambda ti, eb: (ti, 0)),
            pl.BlockSpec((te, D_MODEL), lambda ti, eb: (eb, 0)),
            pl.BlockSpec((1, D_MODEL, te), lambda ti, eb: (eb, 0, 0)),
            pl.BlockSpec((PEER_HEADS, 1, n_i1, tm), lambda ti, eb: (0, eb, 0, ti)),
            pl.BlockSpec((PEER_HEADS, 1, n_i1, tm), lambda ti, eb: (0, eb, 0, ti)),
            pl.BlockSpec((PEER_HEADS, PEER_NKEYS, tm), lambda ti, eb: (0, 0, ti)),
            pl.BlockSpec((PEER_HEADS, PEER_NKEYS, tm), lambda ti, eb: (0, 0, ti)),
            pl.BlockSpec((tm, D_MODEL), lambda ti, eb: (ti, 0)),
        ],
        out_specs=pl.BlockSpec((tm, D_MODEL), lambda ti, eb: (ti, 0)),
        out_shape=jax.ShapeDtypeStruct((t, D_MODEL), F32),
        scratch_shapes=[pltpu.VMEM((D_MODEL, tm), F32),
                        pltpu.VMEM((te, tm), BF16), pltpu.VMEM((te, tm), BF16),
                        pltpu.VMEM((PEER_HEADS, PEER_NKEYS, tm), BF16),
                        pltpu.VMEM((PEER_HEADS, PEER_NKEYS, tm), BF16)],
        compiler_params=pltpu.CompilerParams(
            dimension_semantics=("arbitrary", "arbitrary"),
            vmem_limit_bytes=VMEM_LIMIT),
        name="peer_dense",
    )(hn, u, vt, n1r, e1r, rank2t, e2t, x1)


def _row_rms(x, width):
    return x * lax.rsqrt(jnp.sum(x * x, axis=-1, keepdims=True) * (1.0 / width) + EPS)


def _proj_kernel(x_ref, g_in_ref, w_in_ref, g_q_ref, w_uq_ref, g_kv_ref, w_uk_ref, w_uv_ref,
                 g_qh_ref, g_kh_ref, cos_ref, sin_ref, ones_ref, perm_ref, v_one_ref, dft_c_ref,
                 q_ref, kt_ref, v_ref, f_ref):
    o0, o1, o2 = Q_LORA, Q_LORA + KV_LORA, Q_LORA + KV_LORA + LANES
    h = (_row_rms(x_ref[0], D_MODEL) * g_in_ref[...]).astype(BF16)
    p = jnp.dot(h, w_in_ref[...], preferred_element_type=F32)
    c_q = (_row_rms(p[:, :o0], Q_LORA) * g_q_ref[...]).astype(BF16)
    c_kv = (_row_rms(p[:, o0:o1], KV_LORA) * g_kv_ref[...]).astype(BF16)
    k_rope = p[:, o1:o2]
    f_ref[0] = jnp.dot(p[:, o2:].astype(BF16), dft_c_ref[...],
                       preferred_element_type=F32).astype(BF16)
    q = jnp.dot(c_q, w_uq_ref[...], preferred_element_type=F32)
    k = jnp.dot(c_kv, w_uk_ref[...], preferred_element_type=F32)
    v = jnp.dot(c_kv, w_uv_ref[...], preferred_element_type=F32)
    v_ref[0] = (v + v_one_ref[...]).astype(BF16)

    cos, sin = cos_ref[...], sin_ref[...]

    def select_sum(val, mat):
        return jnp.dot(val.astype(BF16), mat, preferred_element_type=F32)

    def head_norm_rope(slot, gain):
        ss = select_sum(slot * slot, ones_ref[...])
        y = slot * lax.rsqrt(ss * (1.0 / QK_HEAD) + EPS) * gain
        return y * cos + select_sum(y, perm_ref[...]) * sin

    q_slots, k_slots = [], []
    for hd in range(N_HEADS):
        hs = slice(hd * LANES, (hd + 1) * LANES)
        q_slots.append(head_norm_rope(q[:, hs], g_qh_ref[...]))
        k_slots.append(head_norm_rope(k[:, hs] + k_rope, g_kh_ref[...]))
    q_ref[0] = jnp.concatenate(q_slots, axis=1).astype(BF16)
    kt_ref[0] = jnp.concatenate(k_slots, axis=1).T.astype(BF16)


def _slot_cols(w, width, n_in):
    w = w.reshape(n_in, N_HEADS, width)
    return jnp.pad(w, ((0, 0), (0, 0), (0, LANES - width))).reshape(n_in, N_HEADS * LANES)


def _project(x, attn_norm_g, w_in, q_lat_g, w_uq, kv_lat_g, w_ukv, q_head_g, k_head_g, *, tm=1024):
    b, s, _ = x.shape
    o0, o1, o2 = Q_LORA, Q_LORA + KV_LORA, Q_LORA + KV_LORA + QK_ROPE
    k_rope_cols = jnp.pad(w_in[:, o1:o2], ((0, 0), (QK_NOPE, LANES - QK_HEAD)))
    w_in_p = jnp.concatenate([w_in[:, :o1], k_rope_cols, w_in[:, o2:]], axis=1).astype(BF16)
    w_uq_p = _slot_cols(w_uq, QK_HEAD, Q_LORA).astype(BF16)
    w_ukv_h = w_ukv.reshape(KV_LORA, N_HEADS, QK_NOPE + V_HEAD)
    w_uk_p = _slot_cols(w_ukv_h[:, :, :QK_NOPE].reshape(KV_LORA, -1), QK_NOPE, KV_LORA).astype(BF16)
    w_uv_p = _slot_cols(w_ukv_h[:, :, QK_NOPE:].reshape(KV_LORA, -1), V_HEAD, KV_LORA).astype(BF16)
    pad_gain = lambda g: jnp.pad(g, (0, LANES - QK_HEAD)).reshape(1, LANES)
    g_qh = pad_gain(q_head_g) * (QK_HEAD ** -0.5 * LOG2_E)
    g_kh = pad_gain(k_head_g)
    half = QK_ROPE // 2
    freqs = 1.0 / (ROPE_THETA ** (jnp.arange(half, dtype=F32) / half))
    ang = jnp.arange(s, dtype=F32)[:, None] * freqs[None, :]
    cos, sin = jnp.cos(ang), jnp.sin(ang)
    zeros = lambda n: jnp.zeros((s, n), F32)
    cos_t = jnp.concatenate([jnp.ones((s, QK_NOPE), F32), cos, cos, zeros(LANES - QK_HEAD)], axis=1)
    sin_t = jnp.concatenate([zeros(QK_NOPE), -sin, sin, zeros(LANES - QK_HEAD)], axis=1)
    lane = np.arange(LANES)
    partner = np.where((lane >= QK_NOPE) & (lane < QK_NOPE + half), lane + half,
                       np.where((lane >= QK_NOPE + half) & (lane < QK_HEAD), lane - half, -1))
    perm = jnp.asarray(lane[:, None] == partner[None, :], BF16)
    ones = jnp.ones((LANES, LANES), BF16)
    v_one = jnp.tile((jnp.arange(LANES) == V_HEAD).astype(F32), N_HEADS).reshape(1, -1)
    row = lambda g: g.reshape(1, -1)

    const = lambda shape: pl.BlockSpec(shape, lambda bi, si: (0,) * len(shape))
    tab = pl.BlockSpec((tm, LANES), lambda bi, si: (si, 0))
    wide = N_HEADS * LANES
    return pl.pallas_call(
        _proj_kernel,
        grid=(b, s // tm),
        in_specs=[
            pl.BlockSpec((1, tm, D_MODEL), lambda bi, si: (bi, si, 0)),
            const((1, D_MODEL)), const(w_in_p.shape), const((1, Q_LORA)), const(w_uq_p.shape),
            const((1, KV_LORA)), const(w_uk_p.shape), const(w_uv_p.shape),
            const((1, LANES)), const((1, LANES)), tab, tab, const((LANES, LANES)),
            const((LANES, LANES)), const((1, wide)),
            const((FNET_W, 2 * FNET_W)),
        ],
        out_specs=[
            pl.BlockSpec((1, tm, wide), lambda bi, si: (bi, si, 0)),
            pl.BlockSpec((1, wide, tm), lambda bi, si: (bi, 0, si)),
            pl.BlockSpec((1, tm, wide), lambda bi, si: (bi, si, 0)),
            pl.BlockSpec((1, tm, 2 * FNET_W), lambda bi, si: (bi, si, 0)),
        ],
        out_shape=[
            jax.ShapeDtypeStruct((b, s, wide), BF16),
            jax.ShapeDtypeStruct((b, wide, s), BF16),
            jax.ShapeDtypeStruct((b, s, wide), BF16),
            jax.ShapeDtypeStruct((b, s, 2 * FNET_W), BF16),
        ],
        compiler_params=pltpu.CompilerParams(
            dimension_semantics=("arbitrary", "arbitrary"), vmem_limit_bytes=VMEM_LIMIT),
        name="input_projection",
    )(x, row(attn_norm_g), w_in_p, row(q_lat_g), w_uq_p, row(kv_lat_g), w_uk_p, w_uv_p,
      g_qh, g_kh, cos_t, sin_t, ones, perm, v_one, _channel_dft(s))


def _mix_kernel(x_ref, a_ref, fm_ref, g_a_ref, g_f_ref, w_a_ref, w_f_ref, g_ffn_ref,
                x1_ref, hn_ref):
    a = (_row_rms(a_ref[...], ATTN_W) * g_a_ref[...]).astype(BF16)
    fm = (_row_rms(fm_ref[...], FNET_W) * g_f_ref[...]).astype(BF16)
    x1 = (x_ref[...] + jnp.dot(a, w_a_ref[...], preferred_element_type=F32)
          + jnp.dot(fm, w_f_ref[...], preferred_element_type=F32))
    x1_ref[...] = x1
    hn_ref[...] = (_row_rms(x1, D_MODEL) * g_ffn_ref[...]).astype(BF16)


def _mix(x, a, fm, attn_out_g, fnet_out_g, w_out, ffn_norm_g, *, tm=1024):
    t = x.shape[0]
    row = lambda g: g.reshape(1, -1)
    tok = lambda w: pl.BlockSpec((tm, w), lambda ti: (ti, 0))
    const = lambda shape: pl.BlockSpec(shape, lambda ti: (0, 0))
    return pl.pallas_call(
        _mix_kernel,
        grid=(t // tm,),
        in_specs=[tok(D_MODEL), tok(ATTN_W), tok(FNET_W), const((1, ATTN_W)), const((1, FNET_W)),
                  const((ATTN_W, D_MODEL)), const((FNET_W, D_MODEL)), const((1, D_MODEL))],
        out_specs=[tok(D_MODEL), tok(D_MODEL)],
        out_shape=[jax.ShapeDtypeStruct((t, D_MODEL), F32), jax.ShapeDtypeStruct((t, D_MODEL), BF16)],
        compiler_params=pltpu.CompilerParams(
            dimension_semantics=("arbitrary",), vmem_limit_bytes=VMEM_LIMIT),
        name="output_mix",
    )(x, a, fm, row(attn_out_g), row(fnet_out_g), w_out[:ATTN_W].astype(BF16),
      w_out[ATTN_W:].astype(BF16), row(ffn_norm_g))


FFT_S1 = 64


def _dft(n):
    ang = -2.0 * np.pi * np.outer(np.arange(n), np.arange(n)) / n
    return np.cos(ang), np.sin(ang)


def _channel_dft(s):
    c, si = _dft(FNET_CH)
    eye = np.eye(FNET_GROUPS)
    scale = (FNET_CH * s) ** -0.5
    return jnp.asarray(np.concatenate([np.kron(eye, c), np.kron(eye, si)], axis=1) * scale, BF16)


def _fft_stage1_kernel(x_ref, dr_ref, di_ref, twr_ref, twi_ref, o_ref):
    x = x_ref[0]
    yr = jnp.dot(dr_ref[...], x, preferred_element_type=F32)
    yi = jnp.dot(di_ref[...], x, preferred_element_type=F32)
    w = FNET_W
    for j in range(x.shape[1] // (2 * w)):
        re, im = slice(2 * j * w, (2 * j + 1) * w), slice((2 * j + 1) * w, (2 * j + 2) * w)
        ar = yr[:, re] - yi[:, im]
        ai = yr[:, im] + yi[:, re]
        tw = slice(j * LANES, (j + 1) * LANES)
        twr = jnp.concatenate([twr_ref[:, tw]] * (w // LANES), axis=1)
        twi = jnp.concatenate([twi_ref[:, tw]] * (w // LANES), axis=1)
        o_ref[0, :, re] = (ar * twr - ai * twi).astype(BF16)
        o_ref[0, :, im] = (ar * twi + ai * twr).astype(BF16)


def _fft_stage2_kernel(a_ref, m_ref, o_ref):
    w = FNET_W
    for j in range(a_ref.shape[1]):
        xa = a_ref[0, j]
        stacked = jnp.concatenate([xa[:, :w], xa[:, w:]], axis=0)
        o_ref[0, j] = jnp.dot(m_ref[...], stacked, preferred_element_type=F32)


def _fnet(fx, *, tn2=8, kb=8):
    b, s, wide = fx.shape
    s1, s2 = FFT_S1, s // FFT_S1
    d1r, d1i = _dft(s1)
    d2r, d2i = _dft(s2)
    ang = -2.0 * np.pi / s * jnp.outer(jnp.arange(s1, dtype=F32), jnp.arange(s2, dtype=F32))
    twr = jnp.repeat(jnp.cos(ang), LANES, axis=1)
    twi = jnp.repeat(jnp.sin(ang), LANES, axis=1)
    cols = tn2 * wide
    x2 = fx.reshape(b, s1, s2 * wide)
    a = pl.pallas_call(
        _fft_stage1_kernel,
        grid=(b, s2 // tn2),
        in_specs=[
            pl.BlockSpec((1, s1, cols), lambda bi, ci: (bi, 0, ci)),
            pl.BlockSpec((s1, s1), lambda bi, ci: (0, 0)),
            pl.BlockSpec((s1, s1), lambda bi, ci: (0, 0)),
            pl.BlockSpec((s1, tn2 * LANES), lambda bi, ci: (0, ci)),
            pl.BlockSpec((s1, tn2 * LANES), lambda bi, ci: (0, ci)),
        ],
        out_specs=pl.BlockSpec((1, s1, cols), lambda bi, ci: (bi, 0, ci)),
        out_shape=jax.ShapeDtypeStruct((b, s1, s2 * wide), BF16),
        compiler_params=pltpu.CompilerParams(
            dimension_semantics=("arbitrary", "arbitrary"), vmem_limit_bytes=VMEM_LIMIT),
        name="fft_stage1",
    )(x2, jnp.asarray(d1r, BF16), jnp.asarray(d1i, BF16), twr, twi)
    a4 = a.reshape(b, s1, s2, wide)
    m2 = jnp.asarray(np.concatenate([d2r, -d2i], axis=1), BF16)
    y = pl.pallas_call(
        _fft_stage2_kernel,
        grid=(b, s1 // kb),
        in_specs=[
            pl.BlockSpec((1, kb, s2, wide), lambda bi, ki: (bi, ki, 0, 0)),
            pl.BlockSpec((s2, 2 * s2), lambda bi, ki: (0, 0)),
        ],
        out_specs=pl.BlockSpec((1, kb, s2, FNET_W), lambda bi, ki: (bi, ki, 0, 0)),
        out_shape=jax.ShapeDtypeStruct((b, s1, s2, FNET_W), F32),
        compiler_params=pltpu.CompilerParams(
            dimension_semantics=("arbitrary", "arbitrary"), vmem_limit_bytes=VMEM_LIMIT),
        name="fft_stage2",
    )(a4, m2)
    return y.transpose(0, 2, 1, 3).reshape(b, s, FNET_W)


def kernel(x_prompt, x_sample, attn_norm_g, w_in, q_lat_g, w_uq, kv_lat_g, w_ukv, q_head_g,
           k_head_g, attn_out_g, fnet_out_g, w_out, ffn_norm_g, peer_w_q, peer_sub_keys,
           peer_u, peer_v):
    l = 0
    keys = peer_sub_keys[l].reshape(2 * PEER_HEADS, PEER_NKEYS, PEER_HALF).astype(BF16)
    w_pq, u, v_tab = peer_w_q[l].astype(BF16), peer_u[l].astype(BF16), peer_v[l].astype(BF16)
    outs = []
    for x in (x_prompt, x_sample):
        q, kt, v, f = _project(x, attn_norm_g[l], w_in[l], q_lat_g[l], w_uq[l], kv_lat_g[l],
                               w_ukv[l], q_head_g[l], k_head_g[l])
        a = _flash_attention(q, kt, v).reshape(-1, ATTN_W)
        fm = _fnet(f).reshape(-1, FNET_W)
        x1, hn = _mix(x.reshape(-1, D_MODEL), a, fm, attn_out_g[l], fnet_out_g[l], w_out[l],
                      ffn_norm_g[l])
        rank2t, e2t, n1t, e1t = _peer_gate(hn, w_pq, keys)
        outs.append(_peer_dense(hn, u, v_tab, n1t, e1t, rank2t, e2t, x1).reshape(x.shape))
    return tuple(outs)
```
